```python
import jax, jax.numpy as jnp
from jax import lax
import numpy as np

D_MODEL = 1024
BATCH = 8
SEQ = 4096
DEPTH = 1

HEAD_DIM = 64
D_MIX = D_MODEL
NSA_WIDTH = D_MIX // 2
NSA_HEADS = NSA_WIDTH // HEAD_DIM
NSA_KV_HEADS = 2
NSA_GQA = NSA_HEADS // NSA_KV_HEADS
KV_WIDTH = NSA_KV_HEADS * HEAD_DIM
CMP_BLOCK = 32
CMP_STRIDE = 16
CMP_HIDDEN = 2 * HEAD_DIM
SEL_BLOCK = 64
SEL_TOPK = 16
WINDOW = 512
Q_BLOCK = 64
N_BRANCH = 3
RWKV_WIDTH = D_MIX - NSA_WIDTH
RWKV_HEADS = RWKV_WIDTH // HEAD_DIM
DECAY_LORA = 64
AAA_LORA = 64
GATE_LORA = 128
D_FF = 4 * D_MODEL
RMS_EPS = 1e-6
GN_EPS = HEAD_DIM * 1e-5
NEG_BIG = -1e30
FORCED_SCORE = 1e6
NSA_SIZES = (NSA_WIDTH,) + (KV_WIDTH,) * 6 + (NSA_HEADS * N_BRANCH,)
RWKV_SIZES = (RWKV_WIDTH,) * 3 + (DECAY_LORA, AAA_LORA, GATE_LORA)
NSA_COLS = sum(NSA_SIZES)
RWKV_COLS = sum(RWKV_SIZES)
D_IN_PROJ = NSA_COLS + RWKV_COLS

kernel_name = 'hybrid_nsa_rwkv7_layer'


def _split_points(sizes):
    return [int(v) for v in np.cumsum(sizes)[:-1]]


def rmsnorm(u, g):
    uf = u.astype(jnp.float32)
    y = uf * lax.rsqrt(jnp.mean(uf * uf, axis=-1, keepdims=True) + RMS_EPS)
    return (y * g.astype(jnp.float32)).astype(u.dtype)


def alibi_slopes(n):
    start = 2.0 ** (-8.0 / n)
    return (start ** np.arange(1, n + 1)).astype(np.float32)


def softmax_masked(s, valid):
    return jax.nn.softmax(jnp.where(valid, s, NEG_BIG), axis=-1) * valid


def compress_blocks(kv, pos, w1, w2):
    B, T = kv.shape[0], kv.shape[1]
    n_cmp = (T - CMP_BLOCK) // CMP_STRIDE + 1
    idx = np.arange(n_cmp)[:, None] * CMP_STRIDE + np.arange(CMP_BLOCK)[None, :]
    blocks = kv[:, idx] + pos[None, None, :, None, :]
    flat = blocks.transpose(0, 1, 3, 2, 4).reshape(B, n_cmp, NSA_KV_HEADS, CMP_BLOCK * HEAD_DIM)
    return jax.nn.gelu(flat @ w1) @ w2


def nsa_mixer(cols, slopes, gate_b, q_g, kc_g, ks_g, kw_g, ck_pos, ck_w1, ck_w2, cv_pos, cv_w1, cv_w2):
    B, T, _ = cols.shape
    H, KVH, G, dh = NSA_HEADS, NSA_KV_HEADS, NSA_GQA, HEAD_DIM
    q, kc, vc, ks, vs, kw, vw, gl = jnp.split(cols, _split_points(NSA_SIZES), axis=-1)
    q = rmsnorm(q.reshape(B, T, H, dh), q_g) * (dh ** -0.5)
    kc = rmsnorm(compress_blocks(kc.reshape(B, T, KVH, dh), ck_pos, ck_w1, ck_w2), kc_g)
    vc = compress_blocks(vc.reshape(B, T, KVH, dh), cv_pos, cv_w1, cv_w2)
    ks = rmsnorm(ks.reshape(B, T, KVH, dh), ks_g)
    vs = vs.reshape(B, T, KVH, dh)
    kw = rmsnorm(kw.reshape(B, T, KVH, dh), kw_g)
    vw = vw.reshape(B, T, KVH, dh)
    gates = jax.nn.sigmoid(gl + gate_b).reshape(B, T, KVH, G, N_BRANCH)

    n_cmp = kc.shape[1]
    n_sel = T // SEL_BLOCK
    top_k = min(SEL_TOPK, n_sel)
    cmp_end = jnp.arange(n_cmp) * CMP_STRIDE + (CMP_BLOCK - 1)
    ci = np.arange(n_cmp)[:, None] * CMP_STRIDE
    sj = np.arange(n_sel)[None, :] * SEL_BLOCK
    overlap = jnp.asarray(((ci <= sj + SEL_BLOCK - 1) & (ci + CMP_BLOCK - 1 >= sj)).astype(np.float32))
    ks_blk = ks.reshape(B, n_sel, SEL_BLOCK, KVH, dh).transpose(0, 3, 1, 2, 4)
    vs_blk = vs.reshape(B, n_sel, SEL_BLOCK, KVH, dh).transpose(0, 3, 1, 2, 4)
    kw_pad = jnp.pad(kw, ((0, 0), (WINDOW, 0), (0, 0), (0, 0)))
    vw_pad = jnp.pad(vw, ((0, 0), (WINDOW, 0), (0, 0), (0, 0)))
    m = slopes.reshape(KVH, G)
    b_idx = jnp.arange(B)[:, None, None, None]
    h_idx = jnp.arange(KVH)[None, :, None, None]
    sel_ids = jnp.arange(n_sel)
    blk_off = jnp.arange(SEL_BLOCK)
    win_off = jnp.arange(WINDOW + Q_BLOCK)

    def query_block(q0):
        t = q0 + jnp.arange(Q_BLOCK)
        qb = lax.dynamic_slice_in_dim(q, q0, Q_BLOCK, axis=1).reshape(B, Q_BLOCK, KVH, G, dh)
        gb = lax.dynamic_slice_in_dim(gates, q0, Q_BLOCK, axis=1)
        d_c = (t[:, None] - cmp_end[None, :]).astype(jnp.float32)
        s_c = jnp.einsum('bqhgd,bnhd->bhgqn', qb, kc).astype(jnp.float32) - m[:, :, None, None] * d_c
        p_c = softmax_masked(s_c, d_c >= 0)
        o_c = jnp.einsum('bhgqn,bnhd->bqhgd', p_c.astype(vc.dtype), vc)
        imp = jnp.einsum('bhgqn,nj->bhqj', p_c, overlap)
        cur = (t // SEL_BLOCK)[:, None]
        forced = (sel_ids == 0) | (sel_ids == cur) | (sel_ids == cur - 1)
        imp = jnp.where(forced, FORCED_SCORE, jnp.where(sel_ids > cur, NEG_BIG, imp))
        _, sel = lax.top_k(imp, top_k)
        k_g = ks_blk[b_idx, h_idx, sel]
        v_g = vs_blk[b_idx, h_idx, sel]
        d_s = (t[None, None, :, None, None] - (sel[..., None] * SEL_BLOCK + blk_off)).astype(jnp.float32)
        s_s = jnp.einsum('bqhgd,bhqksd->bhgqks', qb, k_g).astype(jnp.float32) - m[:, :, None, None, None] * d_s[:, :, None]
        valid_s = (d_s >= 0)[:, :, None].reshape(B, KVH, 1, Q_BLOCK, top_k * SEL_BLOCK)
        p_s = softmax_masked(s_s.reshape(B, KVH, G, Q_BLOCK, top_k * SEL_BLOCK), valid_s)
        p_s = p_s.reshape(B, KVH, G, Q_BLOCK, top_k, SEL_BLOCK)
        o_s = jnp.einsum('bhgqks,bhqksd->bqhgd', p_s.astype(v_g.dtype), v_g)
        kwb = lax.dynamic_slice_in_dim(kw_pad, q0, WINDOW + Q_BLOCK, axis=1)
        vwb = lax.dynamic_slice_in_dim(vw_pad, q0, WINDOW + Q_BLOCK, axis=1)
        s_pos = q0 - WINDOW + win_off
        d_w = t[:, None] - s_pos[None, :]
        valid_w = (d_w >= 0) & (d_w < WINDOW) & (s_pos[None, :] >= 0)
        s_w = jnp.einsum('bqhgd,bshd->bhgqs', qb, kwb).astype(jnp.float32) - m[:, :, None, None] * d_w.astype(jnp.float32)
        p_w = softmax_masked(s_w, valid_w)
        o_w = jnp.einsum('bhgqs,bshd->bqhgd', p_w.astype(vwb.dtype), vwb)
        o = gb[..., 0:1] * o_c + gb[..., 1:2] * o_s + gb[..., 2:3] * o_w
        return o.reshape(B, Q_BLOCK, H * dh)

    out = lax.map(query_block, jnp.arange(T // Q_BLOCK) * Q_BLOCK)
    return out.transpose(1, 0, 2, 3).reshape(B, T, H * dh)


def wkv7_scan(r, w, k, v, a, b):
    B, T, H, N = r.shape

    def step(state, inp):
        r_t, w_t, k_t, v_t, a_t, b_t = inp
        sa = jnp.einsum('bhvk,bhk->bhv', state, a_t)
        state = state * w_t[:, :, None, :] + sa[..., None] * b_t[:, :, None, :] + v_t[..., None] * k_t[:, :, None, :]
        return state, jnp.einsum('bhvk,bhk->bhv', state, r_t)

    xs = tuple(jnp.swapaxes(u, 0, 1) for u in (r, w, k, v, a, b))
    _, ys = lax.scan(step, jnp.zeros((B, H, N, N), jnp.float32), xs)
    return jnp.swapaxes(ys, 0, 1)


def rwkv7_mixer(cols, mu, w0, w2, a0, a2, g2, k_k, k_a, r_k, lnx_w, lnx_b):
    B, T, _ = cols.shape
    H, N = RWKV_HEADS, HEAD_DIM
    prev = jnp.pad(cols, ((0, 0), (1, 0), (0, 0)))[:, :-1]
    z = cols + (prev - cols) * mu
    r, k, v, xw, xa, xg = jnp.split(z, _split_points(RWKV_SIZES), axis=-1)
    w_log = -jax.nn.softplus(-(w0 + jnp.tanh(xw) @ w2)) - 0.5
    decay = jnp.exp(-jnp.exp(w_log.astype(jnp.float32)))
    a = jax.nn.sigmoid(a0 + xa @ a2)
    g = jax.nn.sigmoid(xg) @ g2
    heads = lambda u: u.reshape(B, T, H, N).astype(jnp.float32)
    kk = heads(k * k_k)
    kk = kk * lax.rsqrt(jnp.maximum(jnp.sum(kk * kk, axis=-1, keepdims=True), 1e-24))
    k = k * (1 + (a - 1) * k_a)
    rh, kh, vh, ah = heads(r), heads(k), heads(v), heads(a)
    y = wkv7_scan(rh, decay.reshape(B, T, H, N), kh, vh, -kk, kk * ah)
    mean = jnp.mean(y, axis=-1, keepdims=True)
    var = jnp.mean(jnp.square(y - mean), axis=-1, keepdims=True)
    y = ((y - mean) * lax.rsqrt(var + GN_EPS)).reshape(B, T, RWKV_WIDTH) * lnx_w + lnx_b
    bonus = jnp.sum(rh * kh * r_k, axis=-1, keepdims=True) * vh
    y = y + bonus.reshape(B, T, RWKV_WIDTH)
    return (y * g).astype(cols.dtype)


def setup_inputs(seed: int = 0) -> dict:
    key = jax.random.key(seed)
    ks = jax.random.split(key, 32)
    L, dh = DEPTH, HEAD_DIM
    nrm = lambda k, shape, scale: jax.random.normal(k, shape, jnp.float32) * scale
    gain = lambda k, shape: 1.0 + 0.02 * jax.random.normal(k, shape, jnp.float32)
    return {
        'x': jax.random.normal(ks[0], (BATCH, SEQ, D_MODEL), jnp.float32),
        'ln_mix_g': gain(ks[1], (L, D_MODEL)),
        'w_in': nrm(ks[2], (L, D_MODEL, D_IN_PROJ), D_MODEL ** -0.5),
        'nsa_gate_b': nrm(ks[3], (L, NSA_HEADS * N_BRANCH), 0.1),
        'q_norm_g': gain(ks[4], (L, dh)),
        'kc_norm_g': gain(ks[5], (L, dh)),
        'ks_norm_g': gain(ks[6], (L, dh)),
        'kw_norm_g': gain(ks[7], (L, dh)),
        'cmp_k_pos': nrm(ks[8], (L, CMP_BLOCK, dh), 0.1),
        'cmp_k_w1': nrm(ks[9], (L, CMP_BLOCK * dh, CMP_HIDDEN), (CMP_BLOCK * dh) ** -0.5),
        'cmp_k_w2': nrm(ks[10], (L, CMP_HIDDEN, dh), CMP_HIDDEN ** -0.5),
        'cmp_v_pos': nrm(ks[11], (L, CMP_BLOCK, dh), 0.1),
        'cmp_v_w1': nrm(ks[12], (L, CMP_BLOCK * dh, CMP_HIDDEN), (CMP_BLOCK * dh) ** -0.5),
        'cmp_v_w2': nrm(ks[13], (L, CMP_HIDDEN, dh), CMP_HIDDEN ** -0.5),
        'rwkv_mu': jax.random.uniform(ks[14], (L, RWKV_COLS), jnp.float32),
        'rwkv_w0': jax.random.uniform(ks[15], (L, RWKV_WIDTH), jnp.float32, -4.0, 1.0),
        'rwkv_w2': nrm(ks[16], (L, DECAY_LORA, RWKV_WIDTH), 0.1),
        'rwkv_a0': nrm(ks[17], (L, RWKV_WIDTH), 0.1),
        'rwkv_a2': nrm(ks[18], (L, AAA_LORA, RWKV_WIDTH), 0.1),
        'rwkv_g2': nrm(ks[19], (L, GATE_LORA, RWKV_WIDTH), GATE_LORA ** -0.5),
        'rwkv_k_k': 0.85 + 0.02 * jax.random.normal(ks[20], (L, RWKV_WIDTH), jnp.float32),
        'rwkv_k_a': gain(ks[21], (L, RWKV_WIDTH)),
        'rwkv_r_k': nrm(ks[22], (L, RWKV_HEADS, dh), 0.1),
        'rwkv_lnx_w': gain(ks[23], (L, RWKV_WIDTH)),
        'rwkv_lnx_b': nrm(ks[24], (L, RWKV_WIDTH), 0.02),
        'w_out': nrm(ks[25], (L, D_MIX, D_MODEL), D_MIX ** -0.5),
        'ln_ffn_g': gain(ks[26], (L, D_MODEL)),
        'w_ff1': nrm(ks[27], (L, D_MODEL, D_FF), D_MODEL ** -0.5),
        'w_ff2': nrm(ks[28], (L, D_FF, D_MODEL), D_FF ** -0.5),
    }


def reference(x, ln_mix_g, w_in, nsa_gate_b, q_norm_g, kc_norm_g, ks_norm_g, kw_norm_g,
              cmp_k_pos, cmp_k_w1, cmp_k_w2, cmp_v_pos, cmp_v_w1, cmp_v_w2,
              rwkv_mu, rwkv_w0, rwkv_w2, rwkv_a0, rwkv_a2, rwkv_g2, rwkv_k_k, rwkv_k_a, rwkv_r_k,
              rwkv_lnx_w, rwkv_lnx_b, w_out, ln_ffn_g, w_ff1, w_ff2):
    slopes = jnp.asarray(alibi_slopes(NSA_HEADS))
    for l in range(DEPTH):
        proj = rmsnorm(x, ln_mix_g[l]) @ w_in[l]
        y_nsa = nsa_mixer(proj[..., :NSA_COLS], slopes, nsa_gate_b[l], q_norm_g[l], kc_norm_g[l],
                          ks_norm_g[l], kw_norm_g[l], cmp_k_pos[l], cmp_k_w1[l], cmp_k_w2[l],
                          cmp_v_pos[l], cmp_v_w1[l], cmp_v_w2[l])
        y_rwkv = rwkv7_mixer(proj[..., NSA_COLS:], rwkv_mu[l], rwkv_w0[l], rwkv_w2[l], rwkv_a0[l],
                             rwkv_a2[l], rwkv_g2[l], rwkv_k_k[l], rwkv_k_a[l], rwkv_r_k[l],
                             rwkv_lnx_w[l], rwkv_lnx_b[l])
        x = x + jnp.concatenate([y_nsa, y_rwkv], axis=-1) @ w_out[l]
        hidden = jnp.square(jax.nn.relu(rmsnorm(x, ln_ffn_g[l]) @ w_ff1[l]))
        x = x + hidden @ w_ff2[l]
    return x
```

```python
import functools

import numpy as np
import jax
import jax.numpy as jnp
from jax import lax
from jax.experimental import pallas as pl
from jax.experimental.pallas import tpu as pltpu

F32 = jnp.float32
BF16 = jnp.bfloat16

HEAD_DIM = 64
NSA_KV_HEADS = 2
CMP_BLOCK = 32
CMP_STRIDE = 16
CMP_HIDDEN = 2 * HEAD_DIM
SEL_BLOCK = 64
SEL_SHIFT = 6
SEL_TOPK = 16
WINDOW = 512
N_BRANCH = 3
DECAY_LORA = 64
AAA_LORA = 64
GATE_LORA = 128
RMS_EPS = 1e-6
GN_EPS = HEAD_DIM * 1e-5
NEG_BIG = -1e30
FORCED_SCORE = 1e6

LANES = 128
SEL_LANES = 64
MASK_BIG = 32768.0
VMEM_LIMIT = 56 * 1024 * 1024

ROW_TILE = 512
Q_TILE = 128
K_TILE = 512
SCAN_CHUNK = 64


def _rms(u, g):
    return u * lax.rsqrt(jnp.mean(u * u, axis=-1, keepdims=True) + RMS_EPS) * g


def _sigmoid(u):
    return 1.0 / (1.0 + jnp.exp(-u))


def _dot(a, b):
    return jnp.dot(a, b, preferred_element_type=F32)


def _dot_nt(a, b):
    return lax.dot_general(a, b, (((1,), (1,)), ((), ())), preferred_element_type=F32)


def _const_spec(shape):
    nd = len(shape)
    return pl.BlockSpec(shape, lambda *_: (0,) * nd, pipeline_mode=pl.Buffered(1))


def _inproj_kernel(x_ref, g_ref, wn_ref, wr_ref, qg_ref, ksg_ref, kwg_ref, gb_ref,
                   q_ref, kvc_ref, ksa_ref, vs_ref, kw_ref, vw_ref, gates_ref, rw_ref):
    tm = x_ref.shape[1]
    n_q_heads = q_ref.shape[1]
    nsa_w = n_q_heads * HEAD_DIM
    kvw = NSA_KV_HEADS * HEAD_DIM
    xn = _rms(x_ref[0], g_ref[...]).astype(BF16)
    pn = _dot(xn, wn_ref[...])
    rw_ref[0] = _dot(xn, wr_ref[...])

    zeros_hd = jnp.zeros((tm, HEAD_DIM), F32)
    for h in range(n_q_heads):
        qh = _rms(pn[:, h * HEAD_DIM:(h + 1) * HEAD_DIM], qg_ref[...]) * (HEAD_DIM ** -0.5)
        q_ref[0, h] = jnp.concatenate([qh, zeros_hd], axis=-1).astype(BF16)

    kvc_ref[0] = pn[:, nsa_w:nsa_w + 2 * kvw]

    tok = pl.program_id(1) * tm + lax.broadcasted_iota(jnp.int32, (tm, 1), 0)
    blk = lax.shift_right_logical(tok, SEL_SHIFT)
    onehot = jnp.where(blk == lax.broadcasted_iota(jnp.int32, (1, SEL_LANES), 1), MASK_BIG, 0.0)
    off = nsa_w + 2 * kvw
    for h in range(NSA_KV_HEADS):
        ksh = _rms(pn[:, off + h * HEAD_DIM:off + (h + 1) * HEAD_DIM], ksg_ref[...])
        ksa_ref[0, h] = jnp.concatenate([ksh, onehot], axis=-1).astype(BF16)
        o2 = off + kvw
        vs_ref[0, h] = pn[:, o2 + h * HEAD_DIM:o2 + (h + 1) * HEAD_DIM].astype(BF16)
        o3 = off + 2 * kvw
        kwh = _rms(pn[:, o3 + h * HEAD_DIM:o3 + (h + 1) * HEAD_DIM], kwg_ref[...])
        kw_ref[0, h] = jnp.concatenate([kwh, zeros_hd], axis=-1).astype(BF16)
        o4 = off + 3 * kvw
        vw_ref[0, h] = pn[:, o4 + h * HEAD_DIM:o4 + (h + 1) * HEAD_DIM].astype(BF16)
        o5 = off + 4 * kvw + h * LANES
        gates_ref[0, h] = _sigmoid(pn[:, o5:o5 + LANES] + gb_ref[:, h * LANES:(h + 1) * LANES])


def _inproj(x, ln_g, w_nsa, w_rwkv, q_g, ks_g, kw_g, gate_b2):
    B, T, D = x.shape
    nsa_cols = w_nsa.shape[1]
    rw_cols = w_rwkv.shape[1]
    n_q_heads = (nsa_cols - 2 * LANES - 6 * NSA_KV_HEADS * HEAD_DIM) // HEAD_DIM
    tm = ROW_TILE
    kvw = NSA_KV_HEADS * HEAD_DIM
    out_shape = (
        jax.ShapeDtypeStruct((B, n_q_heads, T, LANES), BF16),
        jax.ShapeDtypeStruct((B, T, 2 * kvw), F32),
        jax.ShapeDtypeStruct((B, NSA_KV_HEADS, T, LANES), BF16),
        jax.ShapeDtypeStruct((B, NSA_KV_HEADS, T, HEAD_DIM), BF16),
        jax.ShapeDtypeStruct((B, NSA_KV_HEADS, T, LANES), BF16),
        jax.ShapeDtypeStruct((B, NSA_KV_HEADS, T, HEAD_DIM), BF16),
        jax.ShapeDtypeStruct((B, NSA_KV_HEADS, T, LANES), F32),
        jax.ShapeDtypeStruct((B, T, rw_cols), F32),
    )
    head_spec = lambda n, w: pl.BlockSpec((1, n, tm, w), lambda b, i: (b, 0, i, 0))
    return pl.pallas_call(
        _inproj_kernel,
        grid=(B, T // tm),
        in_specs=[
            pl.BlockSpec((1, tm, D), lambda b, i: (b, i, 0)),
            _const_spec((1, D)),
            _const_spec((D, nsa_cols)),
            _const_spec((D, rw_cols)),
            _const_spec((1, HEAD_DIM)),
            _const_spec((1, HEAD_DIM)),
            _const_spec((1, HEAD_DIM)),
            _const_spec((1, NSA_KV_HEADS * LANES)),
        ],
        out_specs=(
            head_spec(n_q_heads, LANES),
            pl.BlockSpec((1, tm, 2 * kvw), lambda b, i: (b, i, 0)),
            head_spec(NSA_KV_HEADS, LANES),
            head_spec(NSA_KV_HEADS, HEAD_DIM),
            head_spec(NSA_KV_HEADS, LANES),
            head_spec(NSA_KV_HEADS, HEAD_DIM),
            head_spec(NSA_KV_HEADS, LANES),
            pl.BlockSpec((1, tm, rw_cols), lambda b, i: (b, i, 0)),
        ),
        out_shape=out_shape,
        compiler_params=pltpu.CompilerParams(
            dimension_semantics=("parallel", "parallel"), vmem_limit_bytes=VMEM_LIMIT),
        name="inproj",
    )(x, ln_g, w_nsa, w_rwkv, q_g, ks_g, kw_g, gate_b2)


def _gelu_tanh(u):
    return 0.5 * u * (1.0 + jnp.tanh(np.sqrt(2.0 / np.pi).astype(np.float32) * (u + 0.044715 * (u * u * u))))


def _compress_kernel(seg_ref, wa_ref, wb_ref, pa_ref, pb_ref, w2_ref, kcg_ref, kc_ref, vc_ref):
    ncp = seg_ref.shape[1]
    seg = seg_ref[0]
    first = _dot((seg + pa_ref[...]).astype(BF16), wa_ref[...])
    second = _dot((seg + pb_ref[...]).astype(BF16), wb_ref[...])
    row = lax.broadcasted_iota(jnp.int32, (ncp, 1), 0)
    real = row < ncp - 1
    second = jnp.where(real, pltpu.roll(second, ncp - 1, axis=0), 0.0)
    hid = _gelu_tanh(first + second)
    out = jnp.where(real, _dot(hid.astype(BF16), w2_ref[...]), 0.0)
    zeros_hd = jnp.zeros((ncp, HEAD_DIM), F32)
    for h in range(NSA_KV_HEADS):
        kch = _rms(out[:, h * HEAD_DIM:(h + 1) * HEAD_DIM], kcg_ref[...])
        kc_ref[0, h] = jnp.concatenate([kch, zeros_hd], axis=-1).astype(BF16)
        o = (NSA_KV_HEADS + h) * HEAD_DIM
        vc_ref[0, h] = out[:, o:o + HEAD_DIM].astype(BF16)


def _compress(seg, wa, wb, pa, pb, w2, kc_g):
    B, ncp, width = seg.shape
    hid = wa.shape[1]
    return pl.pallas_call(
        _compress_kernel,
        grid=(B,),
        in_specs=[
            pl.BlockSpec((1, ncp, width), lambda b: (b, 0, 0)),
            _const_spec((width, hid)),
            _const_spec((width, hid)),
            _const_spec((1, width)),
            _const_spec((1, width)),
            _const_spec((hid, 2 * NSA_KV_HEADS * HEAD_DIM)),
            _const_spec((1, HEAD_DIM)),
        ],
        out_specs=(
            pl.BlockSpec((1, NSA_KV_HEADS, ncp, LANES), lambda b: (b, 0, 0, 0)),
            pl.BlockSpec((1, NSA_KV_HEADS, ncp, HEAD_DIM), lambda b: (b, 0, 0, 0)),
        ),
        out_shape=(
            jax.ShapeDtypeStruct((B, NSA_KV_HEADS, ncp, LANES), BF16),
            jax.ShapeDtypeStruct((B, NSA_KV_HEADS, ncp, HEAD_DIM), BF16),
        ),
        compiler_params=pltpu.CompilerParams(
            dimension_semantics=("parallel",), vmem_limit_bytes=VMEM_LIMIT),
        name="compress",
    )(seg, wa, wb, pa, pb, w2, kc_g)


def _split3_bf16(u):
    hi = u.astype(BF16)
    r1 = u - hi.astype(F32)
    mid = r1.astype(BF16)
    lo = (r1 - mid.astype(F32)).astype(BF16)
    return hi, mid, lo


def _nsa_kernel(q_ref, kc_ref, vc_ref, ksa_ref, vs_ref, kw_ref, vw_ref, gates_ref, slope_ref, ovt_ref,
                o_ref, *, top_k):
    gqa, tq = q_ref.shape[1], q_ref.shape[2]
    ncp = kc_ref.shape[2]
    rows = gqa * tq
    q0 = pl.program_id(2) * tq
    q = q_ref[0].reshape(rows, LANES)
    slope = slope_ref[0]
    row = lax.broadcasted_iota(jnp.int32, (rows, 1), 0)
    t_rows = q0 + (row & (tq - 1))

    s_c = _dot_nt(q, kc_ref[0, 0])
    cend = lax.broadcasted_iota(jnp.int32, (1, ncp), 1) * CMP_STRIDE + (CMP_BLOCK - 1)
    d_c = t_rows - cend
    valid_c = d_c >= 0
    s_c = jnp.where(valid_c, s_c - slope * d_c.astype(F32), NEG_BIG)
    e_c = jnp.where(valid_c, jnp.exp(s_c - jnp.max(s_c, axis=-1, keepdims=True)), 0.0)
    l_c = jnp.sum(e_c, axis=-1, keepdims=True)
    p_c = e_c * (1.0 / jnp.where(l_c > 0.0, l_c, 1.0))
    o_c = _dot(p_c.astype(BF16), vc_ref[0, 0])

    p_sum = p_c[0:tq]
    for g in range(1, gqa):
        p_sum = p_sum + p_c[g * tq:(g + 1) * tq]
    imp_t = _dot_nt(ovt_ref[...], jnp.concatenate(_split3_bf16(p_sum), axis=-1))

    jb = lax.broadcasted_iota(jnp.int32, (SEL_LANES, 1), 0)
    jb_f = jb.astype(F32)
    cur = lax.shift_right_logical(q0 + lax.broadcasted_iota(jnp.int32, (1, tq), 1), SEL_SHIFT)
    forced = (jb == 0) | (jb == cur) | (jb == cur - 1)
    work = jnp.where(forced, FORCED_SCORE, jnp.where(jb > cur, NEG_BIG, imp_t))
    sel_t = jnp.zeros((SEL_LANES, tq), F32)
    for _ in range(top_k):
        mx = jnp.max(work, axis=0, keepdims=True)
        idx = jnp.min(jnp.where(work == mx, jb_f, float(SEL_LANES)), axis=0, keepdims=True)
        pick = jb_f == idx
        sel_t = jnp.where(pick, 1.0, sel_t)
        work = jnp.where(pick, -jnp.inf, work)
    pad_t = jnp.concatenate([jnp.zeros((SEL_LANES, tq), F32), sel_t - 1.0], axis=0)
    sel_q = pad_t.T
    q_aug = (q.astype(F32) + jnp.concatenate([sel_q] * gqa, axis=0)).astype(BF16)

    tk = K_TILE
    n_kt = (q0 + tq - 1) // tk + 1
    col = lax.broadcasted_iota(jnp.int32, (1, tk), 1)

    def sel_step(kt, carry):
        m_run, l_run, acc = carry
        k0 = pl.multiple_of(kt * tk, tk)
        s = _dot_nt(q_aug, ksa_ref[0, 0, pl.ds(k0, tk), :])
        d = (t_rows - k0) - col
        valid = d >= 0
        s = jnp.where(valid, s - slope * d.astype(F32), NEG_BIG)
        m_new = jnp.maximum(m_run, jnp.max(s, axis=-1, keepdims=True))
        alpha = jnp.exp(m_run - m_new)
        p = jnp.exp(s - m_new)
        l_new = alpha * l_run + jnp.sum(p, axis=-1, keepdims=True)
        acc = alpha * acc + _dot(p.astype(BF16), vs_ref[0, 0, pl.ds(k0, tk), :])
        return m_new, l_new, acc

    init = (jnp.full((rows, 1), NEG_BIG, F32), jnp.zeros((rows, 1), F32), jnp.zeros((rows, HEAD_DIM), F32))
    _, l_s, acc_s = lax.fori_loop(0, n_kt, sel_step, init)
    o_s = acc_s * (1.0 / l_s)

    wk = WINDOW + tq
    start = pl.multiple_of(jnp.maximum(q0 - WINDOW, 0), tq)
    s_w = _dot_nt(q, kw_ref[0, 0, pl.ds(start, wk), :])
    d_w = (t_rows - start) - lax.broadcasted_iota(jnp.int32, (1, wk), 1)
    valid_w = (d_w >= 0) & (d_w < WINDOW)
    s_w = jnp.where(valid_w, s_w - slope * d_w.astype(F32), NEG_BIG)
    e_w = jnp.where(valid_w, jnp.exp(s_w - jnp.max(s_w, axis=-1, keepdims=True)), 0.0)
    l_w = jnp.sum(e_w, axis=-1, keepdims=True)
    o_w = _dot(e_w.astype(BF16), vw_ref[0, 0, pl.ds(start, wk), :]) * (1.0 / l_w)

    gt = gates_ref[0, 0]
    outs = []
    for g in range(gqa):
        sl = slice(g * tq, (g + 1) * tq)
        c = g * N_BRANCH
        outs.append(gt[:, c:c + 1] * o_c[sl] + gt[:, c + 1:c + 2] * o_s[sl] + gt[:, c + 2:c + 3] * o_w[sl])
    o_ref[0] = jnp.concatenate(outs, axis=-1).astype(o_ref.dtype)


def _nsa_attention(q, kc, vc, ksa, vs, kw, vw, gates, slopes, ovt, top_k):
    B, n_heads, T, _ = q.shape
    gqa = n_heads // NSA_KV_HEADS
    ncp = kc.shape[2]
    tq = Q_TILE
    rows = gqa * tq
    kv_spec = lambda n, w: pl.BlockSpec((1, 1, n, w), lambda b, h, i: (b, h, 0, 0))
    return pl.pallas_call(
        functools.partial(_nsa_kernel, top_k=top_k),
        grid=(B, NSA_KV_HEADS, T // tq),
        in_specs=[
            pl.BlockSpec((1, gqa, tq, LANES), lambda b, h, i: (b, h, i, 0)),
            kv_spec(ncp, LANES),
            kv_spec(ncp, HEAD_DIM),
            kv_spec(T, LANES),
            kv_spec(T, HEAD_DIM),
            kv_spec(T, LANES),
            kv_spec(T, HEAD_DIM),
            pl.BlockSpec((1, 1, tq, LANES), lambda b, h, i: (b, h, i, 0)),
            pl.BlockSpec((1, rows, 1), lambda b, h, i: (h, 0, 0)),
            pl.BlockSpec((SEL_LANES, 3 * ncp), lambda b, h, i: (0, 0)),
        ],
        out_specs=pl.BlockSpec((1, tq, gqa * HEAD_DIM), lambda b, h, i: (b, i, h)),
        out_shape=jax.ShapeDtypeStruct((B, T, n_heads * HEAD_DIM), BF16),
        compiler_params=pltpu.CompilerParams(
            dimension_semantics=("parallel", "parallel", "arbitrary"), vmem_limit_bytes=VMEM_LIMIT),
        name="nsa_attention",
    )(q, kc, vc, ksa, vs, kw, vw, gates, slopes, ovt)


def _rwkv_prep_kernel(c_ref, cprev_ref, mu_ref, wwa_ref, g2_ref, w0_ref, a0_ref, kk_ref, ka_ref, rk_ref,
                      r_ref, w_ref, k_ref, v_ref, a_ref, b_ref, g_ref, bonus_ref):
    tt = c_ref.shape[1]
    width = r_ref.shape[2]
    n_heads = width // HEAD_DIM
    c = c_ref[0]
    last = jnp.where(pl.program_id(1) > 0, cprev_ref[0][7:8, :], 0.0)
    row = lax.broadcasted_iota(jnp.int32, (tt, 1), 0)
    prev = jnp.where(row == 0, last, pltpu.roll(c, 1, axis=0))
    z = c + (prev - c) * mu_ref[...]
    r = z[:, 0:width]
    k = z[:, width:2 * width]
    v = z[:, 2 * width:3 * width]
    xwa = z[:, 3 * width:3 * width + LANES]
    xg = z[:, 3 * width + LANES:3 * width + 2 * LANES]
    lane = lax.broadcasted_iota(jnp.int32, (1, LANES), 1)
    lora = _dot(jnp.where(lane < DECAY_LORA, jnp.tanh(xwa), xwa).astype(BF16), wwa_ref[...])
    y = w0_ref[...] + lora[:, 0:width]
    softplus_neg = jnp.maximum(-y, 0.0) + jnp.log(1.0 + jnp.exp(-jnp.abs(y)))
    decay = jnp.exp(-jnp.exp(-softplus_neg - 0.5))
    a = _sigmoid(a0_ref[...] + lora[:, width:2 * width])
    g_ref[0] = _dot(_sigmoid(xg).astype(BF16), g2_ref[...])
    kk = k * kk_ref[...]
    k2 = k * (1.0 + (a - 1.0) * ka_ref[...])
    rk = r * k2 * rk_ref[...]
    r_ref[0] = r
    w_ref[0] = decay
    k_ref[0] = k2
    v_ref[0] = v
    for h in range(n_heads):
        sl = slice(h * HEAD_DIM, (h + 1) * HEAD_DIM)
        kkh = kk[:, sl]
        kkh = kkh * lax.rsqrt(jnp.maximum(jnp.sum(kkh * kkh, axis=-1, keepdims=True), 1e-24))
        a_ref[0, :, sl] = -kkh
        b_ref[0, :, sl] = kkh * a[:, sl]
        bonus_ref[0, :, sl] = jnp.sum(rk[:, sl], axis=-1, keepdims=True) * v[:, sl]


def _rwkv_prep(cols, mu, wwa, g2, w0, a0, k_k, k_a, r_k):
    B, T, ncols = cols.shape
    width = w0.shape[1]
    tt = ROW_TILE
    big = pl.BlockSpec((1, tt, width), lambda b, i: (b, i, 0))
    vec = _const_spec((1, width))
    return pl.pallas_call(
        _rwkv_prep_kernel,
        grid=(B, T // tt),
        in_specs=[
            pl.BlockSpec((1, tt, ncols), lambda b, i: (b, i, 0)),
            pl.BlockSpec((1, 8, ncols), lambda b, i: (b, jnp.maximum(i * (tt // 8) - 1, 0), 0)),
            _const_spec((1, ncols)),
            _const_spec((LANES, 2 * width)),
            _const_spec((GATE_LORA, width)),
            vec, vec, vec, vec, vec,
        ],
        out_specs=(big,) * 8,
        out_shape=(jax.ShapeDtypeStruct((B, T, width), F32),) * 8,
        compiler_params=pltpu.CompilerParams(
            dimension_semantics=("parallel", "parallel"), vmem_limit_bytes=VMEM_LIMIT),
        name="rwkv_prep",
    )(cols, cols, mu, wwa, g2, w0, a0, k_k, k_a, r_k)


def _tree_sum(terms):
    while len(terms) > 1:
        nxt = [terms[i] + terms[i + 1] for i in range(0, len(terms) - 1, 2)]
        if len(terms) % 2:
            nxt.append(terms[-1])
        terms = nxt
    return terms[0]


def _wkv_scan_kernel(r_ref, w_ref, k_ref, v_ref, a_ref, b_ref, y_ref, state_ref):
    nkp, tc = r_ref.shape[0], r_ref.shape[1]
    n_vg = v_ref.shape[0]
    half = LANES // 2

    @pl.when(pl.program_id(0) == 0)
    def _():
        state_ref[...] = jnp.zeros_like(state_ref)

    def value_group(vg, _):
        def step(t, state):
            vt = v_ref[vg, t]
            part = _tree_sum([state[kp] * a_ref[kp, pl.ds(t, 1), :] for kp in range(nkp)])
            sa = part + pltpu.roll(part, half, axis=1)
            new = []
            for kp in range(nkp):
                new.append(state[kp] * w_ref[kp, pl.ds(t, 1), :] + sa * b_ref[kp, pl.ds(t, 1), :]
                           + vt * k_ref[kp, pl.ds(t, 1), :])
            part_y = _tree_sum([new[kp] * r_ref[kp, pl.ds(t, 1), :] for kp in range(nkp)])
            y_ref[vg, t] = part_y + pltpu.roll(part_y, half, axis=1)
            return tuple(new)

        state = lax.fori_loop(0, tc, step, tuple(state_ref[vg, kp] for kp in range(nkp)))
        for kp in range(nkp):
            state_ref[vg, kp] = state[kp]
        return 0

    lax.fori_loop(0, n_vg, value_group, 0)


def _wkv_scan(r, w, k, v, a, b):
    nkp, T, _ = r.shape
    n_vg = v.shape[0]
    tc = SCAN_CHUNK
    kspec = pl.BlockSpec((nkp, tc, LANES), lambda i: (0, i, 0))
    vspec = pl.BlockSpec((n_vg, tc, 8, LANES), lambda i: (0, i, 0, 0))
    return pl.pallas_call(
        _wkv_scan_kernel,
        grid=(T // tc,),
        in_specs=[kspec, kspec, kspec, vspec, kspec, kspec],
        out_specs=vspec,
        out_shape=jax.ShapeDtypeStruct((n_vg, T, 8, LANES), F32),
        scratch_shapes=[pltpu.VMEM((n_vg, nkp, 8, LANES), F32)],
        compiler_params=pltpu.CompilerParams(
            dimension_semantics=("arbitrary",), vmem_limit_bytes=VMEM_LIMIT),
        name="wkv_scan",
    )(r, w, k, v, a, b)


def _out_ffn_kernel(x_ref, ynsa_ref, ys_ref, g_ref, bonus_ref, lnw_ref, lnb_ref, wo_ref, fg_ref, w1_ref, w2_ref,
                    o_ref):
    nsa_w = ynsa_ref.shape[1]
    width = ys_ref.shape[1]
    ys = ys_ref[...]
    pieces = []
    for h in range(width // HEAD_DIM):
        yh = ys[:, h * HEAD_DIM:(h + 1) * HEAD_DIM]
        dev = yh - jnp.mean(yh, axis=-1, keepdims=True)
        pieces.append(dev * lax.rsqrt(jnp.mean(dev * dev, axis=-1, keepdims=True) + GN_EPS))
    yn = jnp.concatenate(pieces, axis=-1) * lnw_ref[...] + lnb_ref[...]
    y_rwkv = ((yn + bonus_ref[...]) * g_ref[...]).astype(BF16)
    x1 = x_ref[...] + (_dot(ynsa_ref[...], wo_ref[0:nsa_w, :]) + _dot(y_rwkv, wo_ref[nsa_w:nsa_w + width, :]))
    xn = _rms(x1, fg_ref[...]).astype(BF16)
    d_ff = w1_ref.shape[1]
    chunk = 1024
    ffn = None
    for c in range(d_ff // chunk):
        hid = _dot(xn, w1_ref[:, c * chunk:(c + 1) * chunk])
        hid = jnp.square(jnp.maximum(hid, 0.0)).astype(BF16)
        part = _dot(hid, w2_ref[c * chunk:(c + 1) * chunk, :])
        ffn = part if ffn is None else ffn + part
    o_ref[...] = x1 + ffn


def _out_ffn(x2, ynsa, ys, g, bonus, lnx_w, lnx_b, w_out, ffn_g, w1, w2):
    N, D = x2.shape
    nsa_w = ynsa.shape[1]
    width = ys.shape[1]
    d_ff = w1.shape[1]
    tm = ROW_TILE
    rowspec = lambda w: pl.BlockSpec((tm, w), lambda i: (i, 0))
    return pl.pallas_call(
        _out_ffn_kernel,
        grid=(N // tm,),
        in_specs=[
            rowspec(D), rowspec(nsa_w), rowspec(width), rowspec(width), rowspec(width),
            _const_spec((1, width)), _const_spec((1, width)),
            _const_spec((nsa_w + width, D)),
            _const_spec((1, D)),
            _const_spec((D, d_ff)),
            _const_spec((d_ff, D)),
        ],
        out_specs=rowspec(D),
        out_shape=jax.ShapeDtypeStruct((N, D), F32),
        compiler_params=pltpu.CompilerParams(
            dimension_semantics=("parallel",), vmem_limit_bytes=VMEM_LIMIT),
        name="out_ffn",
    )(x2, ynsa, ys, g, bonus, lnx_w, lnx_b, w_out, ffn_g, w1, w2)


def _alibi_slopes(n):
    start = 2.0 ** (-8.0 / n)
    return (start ** np.arange(1, n + 1)).astype(np.float32)


def _overlap_t(ncp, n_cmp, n_sel):
    ci = np.arange(ncp)[None, :] * CMP_STRIDE
    sj = np.arange(SEL_LANES)[:, None] * SEL_BLOCK
    ov = (ci <= sj + SEL_BLOCK - 1) & (ci + CMP_BLOCK - 1 >= sj)
    ov &= (np.arange(ncp)[None, :] < n_cmp) & (np.arange(SEL_LANES)[:, None] < n_sel)
    return jnp.asarray(np.tile(ov.astype(np.float32), (1, 3)), dtype=BF16)


def _compress_weights(k_pos, k_w1, k_w2, v_pos, v_w1, v_w2):
    half = CMP_BLOCK // 2
    groups = 2 * NSA_KV_HEADS

    def first_layer(lo):
        w = jnp.zeros((half, groups, HEAD_DIM, groups, CMP_HIDDEN), F32)
        for grp in range(groups):
            w1 = k_w1 if grp < NSA_KV_HEADS else v_w1
            w = w.at[:, grp, :, grp, :].set(w1.reshape(CMP_BLOCK, HEAD_DIM, CMP_HIDDEN)[lo:lo + half])
        return w.reshape(half * groups * HEAD_DIM, groups * CMP_HIDDEN).astype(BF16)

    def pos_row(lo):
        rows = [(k_pos if grp < NSA_KV_HEADS else v_pos)[lo:lo + half] for grp in range(groups)]
        return jnp.stack(rows, axis=1).reshape(1, half * groups * HEAD_DIM)

    w2 = jnp.zeros((groups, CMP_HIDDEN, groups, HEAD_DIM), F32)
    for grp in range(groups):
        w2 = w2.at[grp, :, grp, :].set(k_w2 if grp < NSA_KV_HEADS else v_w2)
    w2 = w2.reshape(groups * CMP_HIDDEN, groups * HEAD_DIM).astype(BF16)
    return first_layer(0), first_layer(half), pos_row(0), pos_row(half), w2


def _to_scan_keys(u, B, T, n_heads):
    u = u.reshape(B, T, n_heads, 2, HEAD_DIM // 2)
    return u.transpose(4, 1, 3, 0, 2).reshape(HEAD_DIM // 2, T, 2 * B * n_heads)


def _to_scan_values(u, B, T, n_heads):
    u = u.reshape(B, T, n_heads, HEAD_DIM // 8, 8).transpose(3, 1, 4, 0, 2).reshape(HEAD_DIM // 8, T, 8, B * n_heads)
    return jnp.concatenate([u, u], axis=-1)


def _from_scan_values(y, B, T, n_heads):
    y = y[..., :B * n_heads].reshape(HEAD_DIM // 8, T, 8, B, n_heads)
    return y.transpose(3, 1, 4, 0, 2).reshape(B, T, n_heads * HEAD_DIM)


def _layer(x, ln_mix_g, w_in, nsa_gate_b, q_norm_g, kc_norm_g, ks_norm_g, kw_norm_g,
           cmp_k_pos, cmp_k_w1, cmp_k_w2, cmp_v_pos, cmp_v_w1, cmp_v_w2,
           rwkv_mu, rwkv_w0, rwkv_w2, rwkv_a0, rwkv_a2, rwkv_g2, rwkv_k_k, rwkv_k_a, rwkv_r_k,
           rwkv_lnx_w, rwkv_lnx_b, w_out, ln_ffn_g, w_ff1, w_ff2):
    B, T, D = x.shape
    nsa_w = D // 2
    n_heads = nsa_w // HEAD_DIM
    gqa = n_heads // NSA_KV_HEADS
    kvw = NSA_KV_HEADS * HEAD_DIM
    rw_w = D - nsa_w
    rw_heads = rw_w // HEAD_DIM
    n_sel = T // SEL_BLOCK
    ncp = T // CMP_STRIDE
    n_cmp = (T - CMP_BLOCK) // CMP_STRIDE + 1
    top_k = min(SEL_TOPK, n_sel)
    assert T % K_TILE == 0 and T % ROW_TILE == 0 and n_sel <= SEL_LANES and n_cmp == ncp - 1
    assert B * rw_heads * 2 == LANES and gqa * N_BRANCH <= LANES and T > WINDOW
    assert DECAY_LORA + AAA_LORA == LANES and GATE_LORA == LANES

    row2 = lambda u: u.reshape(1, -1)
    nsa_main = nsa_w + 6 * kvw
    gl = w_in[:, nsa_main:nsa_main + n_heads * N_BRANCH].reshape(D, NSA_KV_HEADS, gqa * N_BRANCH)
    gl = jnp.pad(gl, ((0, 0), (0, 0), (0, LANES - gqa * N_BRANCH))).reshape(D, NSA_KV_HEADS * LANES)
    w_nsa = jnp.concatenate([w_in[:, :nsa_main], gl], axis=1).astype(BF16)
    w_rwkv = w_in[:, nsa_main + n_heads * N_BRANCH:].astype(BF16)
    gate_b2 = jnp.pad(nsa_gate_b.reshape(NSA_KV_HEADS, gqa * N_BRANCH),
                      ((0, 0), (0, LANES - gqa * N_BRANCH))).reshape(1, NSA_KV_HEADS * LANES)

    q, kvc, ksa, vs, kw, vw, gates, rw_cols = _inproj(
        x, row2(ln_mix_g), w_nsa, w_rwkv, row2(q_norm_g), row2(ks_norm_g), row2(kw_norm_g), gate_b2)

    wa, wb, pa, pb, cw2 = _compress_weights(cmp_k_pos, cmp_k_w1, cmp_k_w2, cmp_v_pos, cmp_v_w1, cmp_v_w2)
    kc, vc = _compress(kvc.reshape(B, ncp, CMP_STRIDE * 2 * kvw), wa, wb, pa, pb, cw2, row2(kc_norm_g))
    slopes = np.repeat(_alibi_slopes(n_heads).reshape(NSA_KV_HEADS, gqa), Q_TILE, axis=1)[..., None]
    y_nsa = _nsa_attention(q, kc, vc, ksa, vs, kw, vw, gates, jnp.asarray(slopes),
                           _overlap_t(ncp, n_cmp, n_sel), top_k)

    wwa = jnp.zeros((LANES, 2 * rw_w), F32)
    wwa = wwa.at[:DECAY_LORA, :rw_w].set(rwkv_w2).at[DECAY_LORA:, rw_w:].set(rwkv_a2).astype(BF16)
    r, w, k, v, a, b, g, bonus = _rwkv_prep(
        rw_cols, row2(rwkv_mu), wwa, rwkv_g2.astype(BF16), row2(rwkv_w0), row2(rwkv_a0),
        row2(rwkv_k_k), row2(rwkv_k_a), row2(rwkv_r_k))
    keys = [_to_scan_keys(u, B, T, rw_heads) for u in (r, w, k, a, b)]
    y_scan = _wkv_scan(keys[0], keys[1], keys[2], _to_scan_values(v, B, T, rw_heads), keys[3], keys[4])
    ys = _from_scan_values(y_scan, B, T, rw_heads)

    N = B * T
    out = _out_ffn(x.reshape(N, D), y_nsa.reshape(N, nsa_w), ys.reshape(N, rw_w), g.reshape(N, rw_w),
                   bonus.reshape(N, rw_w), row2(rwkv_lnx_w), row2(rwkv_lnx_b), w_out.astype(BF16),
                   row2(ln_ffn_g), w_ff1.astype(BF16), w_ff2.astype(BF16))
    return out.reshape(B, T, D)


def kernel(x, ln_mix_g, w_in, nsa_gate_b, q_norm_g, kc_norm_g, ks_norm_g, kw_norm_g, cmp_k_pos, cmp_k_w1, cmp_k_w2, cmp_v_pos, cmp_v_w1, cmp_v_w2, rwkv_mu, rwkv_w0, rwkv_w2, rwkv_a0, rwkv_a2, rwkv_g2, rwkv_k_k, rwkv_k_a, rwkv_r_k, rwkv_lnx_w, rwkv_lnx_b, w_out, ln_ffn_g, w_ff1, w_ff2):
    params = (ln_mix_g, w_in, nsa_gate_b, q_norm_g, kc_norm_g, ks_norm_g, kw_norm_g, cmp_k_pos, cmp_k_w1,
              cmp_k_w2, cmp_v_pos, cmp_v_w1, cmp_v_w2, rwkv_mu, rwkv_w0, rwkv_w2, rwkv_a0, rwkv_a2, rwkv_g2,
              rwkv_k_k, rwkv_k_a, rwkv_r_k, rwkv_lnx_w, rwkv_lnx_b, w_out, ln_ffn_g, w_ff1, w_ff2)
    for layer in range(ln_mix_g.shape[0]):
        x = _layer(x, *(p[layer] for p in params))
    return x
```

```python
import functools

import numpy as np
import jax
import jax.numpy as jnp
from jax import lax
from jax.experimental import pallas as pl
from jax.experimental.pallas import tpu as pltpu

F32 = jnp.float32
BF16 = jnp.bfloat16

HEAD_DIM = 64
NSA_KV_HEADS = 2
CMP_BLOCK = 32
CMP_STRIDE = 16
CMP_HIDDEN = 2 * HEAD_DIM
SEL_BLOCK = 64
SEL_SHIFT = 6
SEL_TOPK = 16
WINDOW = 512
N_BRANCH = 3
DECAY_LORA = 64
AAA_LORA = 64
GATE_LORA = 128
RMS_EPS = 1e-6
GN_EPS = HEAD_DIM * 1e-5
NEG_BIG = -1e30
FORCED_SCORE = 1e6

LANES = 128
SEL_LANES = 64
MASK_BIG = 32768.0
VMEM_LIMIT = 56 * 1024 * 1024

ROW_TILE = 512
Q_TILE = 128
K_TILE = 512
SCAN_CHUNK = 64


def _rms(u, g):
    return u * lax.rsqrt(jnp.mean(u * u, axis=-1, keepdims=True) + RMS_EPS) * g


def _sigmoid(u):
    return 1.0 / (1.0 + jnp.exp(-u))


def _dot(a, b):
    return jnp.dot(a, b, preferred_element_type=F32)


def _dot_nt(a, b):
    return lax.dot_general(a, b, (((1,), (1,)), ((), ())), preferred_element_type=F32)


def _const_spec(shape):
    nd = len(shape)
    return pl.BlockSpec(shape, lambda *_: (0,) * nd, pipeline_mode=pl.Buffered(1))


def _inproj_kernel(x_ref, g_ref, wn_ref, wr_ref, qg_ref, ksg_ref, kwg_ref, gb_ref,
                   q_ref, kvc_ref, ksa_ref, vs_ref, kw_ref, vw_ref, gates_ref, rw_ref):
    tm = x_ref.shape[1]
    n_q_heads = q_ref.shape[1]
    nsa_w = n_q_heads * HEAD_DIM
    kvw = NSA_KV_HEADS * HEAD_DIM
    xn = _rms(x_ref[0], g_ref[...]).astype(BF16)
    pn = _dot(xn, wn_ref[...])
    rw_ref[0] = _dot(xn, wr_ref[...])

    zeros_hd = jnp.zeros((tm, HEAD_DIM), F32)
    for h in range(n_q_heads):
        qh = _rms(pn[:, h * HEAD_DIM:(h + 1) * HEAD_DIM], qg_ref[...]) * (HEAD_DIM ** -0.5)
        q_ref[0, h] = jnp.concatenate([qh, zeros_hd], axis=-1).astype(BF16)

    kvc_ref[0] = pn[:, nsa_w:nsa_w + 2 * kvw]

    tok = pl.program_id(1) * tm + lax.broadcasted_iota(jnp.int32, (tm, 1), 0)
    blk = lax.shift_right_logical(tok, SEL_SHIFT)
    onehot = jnp.where(blk == lax.broadcasted_iota(jnp.int32, (1, SEL_LANES), 1), MASK_BIG, 0.0)
    off = nsa_w + 2 * kvw
    for h in range(NSA_KV_HEADS):
        ksh = _rms(pn[:, off + h * HEAD_DIM:off + (h + 1) * HEAD_DIM], ksg_ref[...])
        ksa_ref[0, h] = jnp.concatenate([ksh, onehot], axis=-1).astype(BF16)
        o2 = off + kvw
        vs_ref[0, h] = pn[:, o2 + h * HEAD_DIM:o2 + (h + 1) * HEAD_DIM].astype(BF16)
        o3 = off + 2 * kvw
        kwh = _rms(pn[:, o3 + h * HEAD_DIM:o3 + (h + 1) * HEAD_DIM], kwg_ref[...])
        kw_ref[0, h] = jnp.concatenate([kwh, zeros_hd], axis=-1).astype(BF16)
        o4 = off + 3 * kvw
        vw_ref[0, h] = pn[:, o4 + h * HEAD_DIM:o4 + (h + 1) * HEAD_DIM].astype(BF16)
        o5 = off + 4 * kvw + h * LANES
        gates_ref[0, h] = _sigmoid(pn[:, o5:o5 + LANES] + gb_ref[:, h * LANES:(h + 1) * LANES])


def _inproj(x, ln_g, w_nsa, w_rwkv, q_g, ks_g, kw_g, gate_b2):
    B, T, D = x.shape
    nsa_cols = w_nsa.shape[1]
    rw_cols = w_rwkv.shape[1]
    n_q_heads = (nsa_cols - 2 * LANES - 6 * NSA_KV_HEADS * HEAD_DIM) // HEAD_DIM
    tm = ROW_TILE
    kvw = NSA_KV_HEADS * HEAD_DIM
    out_shape = (
        jax.ShapeDtypeStruct((B, n_q_heads, T, LANES), BF16),
        jax.ShapeDtypeStruct((B, T, 2 * kvw), F32),
        jax.ShapeDtypeStruct((B, NSA_KV_HEADS, T, LANES), BF16),
        jax.ShapeDtypeStruct((B, NSA_KV_HEADS, T, HEAD_DIM), BF16),
        jax.ShapeDtypeStruct((B, NSA_KV_HEADS, T, LANES), BF16),
        jax.ShapeDtypeStruct((B, NSA_KV_HEADS, T, HEAD_DIM), BF16),
        jax.ShapeDtypeStruct((B, NSA_KV_HEADS, T, LANES), F32),
        jax.ShapeDtypeStruct((B, T, rw_cols), F32),
    )
    head_spec = lambda n, w: pl.BlockSpec((1, n, tm, w), lambda b, i: (b, 0, i, 0))
    return pl.pallas_call(
        _inproj_kernel,
        grid=(B, T // tm),
        in_specs=[
            pl.BlockSpec((1, tm, D), lambda b, i: (b, i, 0)),
            _const_spec((1, D)),
            _const_spec((D, nsa_cols)),
            _const_spec((D, rw_cols)),
            _const_spec((1, HEAD_DIM)),
            _const_spec((1, HEAD_DIM)),
            _const_spec((1, HEAD_DIM)),
            _const_spec((1, NSA_KV_HEADS * LANES)),
        ],
        out_specs=(
            head_spec(n_q_heads, LANES),
            pl.BlockSpec((1, tm, 2 * kvw), lambda b, i: (b, i, 0)),
            head_spec(NSA_KV_HEADS, LANES),
            head_spec(NSA_KV_HEADS, HEAD_DIM),
            head_spec(NSA_KV_HEADS, LANES),
            head_spec(NSA_KV_HEADS, HEAD_DIM),
            head_spec(NSA_KV_HEADS, LANES),
            pl.BlockSpec((1, tm, rw_cols), lambda b, i: (b, i, 0)),
        ),
        out_shape=out_shape,
        compiler_params=pltpu.CompilerParams(
            dimension_semantics=("parallel", "parallel"), vmem_limit_bytes=VMEM_LIMIT),
        name="inproj",
    )(x, ln_g, w_nsa, w_rwkv, q_g, ks_g, kw_g, gate_b2)


def _gelu_tanh(u):
    return 0.5 * u * (1.0 + jnp.tanh(np.sqrt(2.0 / np.pi).astype(np.float32) * (u + 0.044715 * (u * u * u))))


def _compress_kernel(seg_ref, wa_ref, wb_ref, pa_ref, pb_ref, w2_ref, kcg_ref, kc_ref, vc_ref):
    ncp = seg_ref.shape[1]
    seg = seg_ref[0]
    first = _dot((seg + pa_ref[...]).astype(BF16), wa_ref[...])
    second = _dot((seg + pb_ref[...]).astype(BF16), wb_ref[...])
    row = lax.broadcasted_iota(jnp.int32, (ncp, 1), 0)
    real = row < ncp - 1
    second = jnp.where(real, pltpu.roll(second, ncp - 1, axis=0), 0.0)
    hid = _gelu_tanh(first + second)
    out = jnp.where(real, _dot(hid.astype(BF16), w2_ref[...]), 0.0)
    zeros_hd = jnp.zeros((ncp, HEAD_DIM), F32)
    for h in range(NSA_KV_HEADS):
        kch = _rms(out[:, h * HEAD_DIM:(h + 1) * HEAD_DIM], kcg_ref[...])
        kc_ref[0, h] = jnp.concatenate([kch, zeros_hd], axis=-1).astype(BF16)
        o = (NSA_KV_HEADS + h) * HEAD_DIM
        vc_ref[0, h] = out[:, o:o + HEAD_DIM].astype(BF16)


def _compress(seg, wa, wb, pa, pb, w2, kc_g):
    B, ncp, width = seg.shape
    hid = wa.shape[1]
    return pl.pallas_call(
        _compress_kernel,
        grid=(B,),
        in_specs=[
            pl.BlockSpec((1, ncp, width), lambda b: (b, 0, 0)),
            _const_spec((width, hid)),
            _const_spec((width, hid)),
            _const_spec((1, width)),
            _const_spec((1, width)),
            _const_spec((hid, 2 * NSA_KV_HEADS * HEAD_DIM)),
            _const_spec((1, HEAD_DIM)),
        ],
        out_specs=(
            pl.BlockSpec((1, NSA_KV_HEADS, ncp, LANES), lambda b: (b, 0, 0, 0)),
            pl.BlockSpec((1, NSA_KV_HEADS, ncp, HEAD_DIM), lambda b: (b, 0, 0, 0)),
        ),
        out_shape=(
            jax.ShapeDtypeStruct((B, NSA_KV_HEADS, ncp, LANES), BF16),
            jax.ShapeDtypeStruct((B, NSA_KV_HEADS, ncp, HEAD_DIM), BF16),
        ),
        compiler_params=pltpu.CompilerParams(
            dimension_semantics=("parallel",), vmem_limit_bytes=VMEM_LIMIT),
        name="compress",
    )(seg, wa, wb, pa, pb, w2, kc_g)


def _split3_bf16(u):
    hi = u.astype(BF16)
    r1 = u - hi.astype(F32)
    mid = r1.astype(BF16)
    lo = (r1 - mid.astype(F32)).astype(BF16)
    return hi, mid, lo


def _nsa_kernel(q_ref, kc_ref, vc_ref, ksa_ref, vs_ref, kw_ref, vw_ref, gates_ref, slope_ref, ovt_ref,
                o_ref, *, top_k):
    gqa, tq = q_ref.shape[1], q_ref.shape[2]
    ncp = kc_ref.shape[2]
    rows = gqa * tq
    q0 = pl.program_id(2) * tq
    q = q_ref[0].reshape(rows, LANES)
    slope = slope_ref[0]
    row = lax.broadcasted_iota(jnp.int32, (rows, 1), 0)
    t_rows = q0 + (row & (tq - 1))

    s_c = _dot_nt(q, kc_ref[0, 0])
    cend = lax.broadcasted_iota(jnp.int32, (1, ncp), 1) * CMP_STRIDE + (CMP_BLOCK - 1)
    d_c = t_rows - cend
    valid_c = d_c >= 0
    s_c = jnp.where(valid_c, s_c - slope * d_c.astype(F32), NEG_BIG)
    e_c = jnp.where(valid_c, jnp.exp(s_c - jnp.max(s_c, axis=-1, keepdims=True)), 0.0)
    l_c = jnp.sum(e_c, axis=-1, keepdims=True)
    p_c = e_c * (1.0 / jnp.where(l_c > 0.0, l_c, 1.0))
    o_c = _dot(p_c.astype(BF16), vc_ref[0, 0])

    p_sum = p_c[0:tq]
    for g in range(1, gqa):
        p_sum = p_sum + p_c[g * tq:(g + 1) * tq]
    imp_t = _dot_nt(ovt_ref[...], jnp.concatenate(_split3_bf16(p_sum), axis=-1))

    jb = lax.broadcasted_iota(jnp.int32, (SEL_LANES, 1), 0)
    jb_f = jb.astype(F32)
    cur = lax.shift_right_logical(q0 + lax.broadcasted_iota(jnp.int32, (1, tq), 1), SEL_SHIFT)
    forced = (jb == 0) | (jb == cur) | (jb == cur - 1)
    work = jnp.where(forced, FORCED_SCORE, jnp.where(jb > cur, NEG_BIG, imp_t))
    sel_t = jnp.zeros((SEL_LANES, tq), F32)
    for _ in range(top_k):
        mx = jnp.max(work, axis=0, keepdims=True)
        idx = jnp.min(jnp.where(work == mx, jb_f, float(SEL_LANES)), axis=0, keepdims=True)
        pick = jb_f == idx
        sel_t = jnp.where(pick, 1.0, sel_t)
        work = jnp.where(pick, -jnp.inf, work)
    pad_t = jnp.concatenate([jnp.zeros((SEL_LANES, tq), F32), sel_t - 1.0], axis=0)
    sel_q = pad_t.T
    q_aug = (q.astype(F32) + jnp.concatenate([sel_q] * gqa, axis=0)).astype(BF16)

    tk = K_TILE
    n_kt = (q0 + tq - 1) // tk + 1
    col = lax.broadcasted_iota(jnp.int32, (1, tk), 1)

    def sel_step(kt, carry):
        m_run, l_run, acc = carry
        k0 = pl.multiple_of(kt * tk, tk)
        s = _dot_nt(q_aug, ksa_ref[0, 0, pl.ds(k0, tk), :])
        d = (t_rows - k0) - col
        valid = d >= 0
        s = jnp.where(valid, s - slope * d.astype(F32), NEG_BIG)
        m_new = jnp.maximum(m_run, jnp.max(s, axis=-1, keepdims=True))
        alpha = jnp.exp(m_run - m_new)
        p = jnp.exp(s - m_new)
        l_new = alpha * l_run + jnp.sum(p, axis=-1, keepdims=True)
        acc = alpha * acc + _dot(p.astype(BF16), vs_ref[0, 0, pl.ds(k0, tk), :])
        return m_new, l_new, acc

    init = (jnp.full((rows, 1), NEG_BIG, F32), jnp.zeros((rows, 1), F32), jnp.zeros((rows, HEAD_DIM), F32))
    _, l_s, acc_s = lax.fori_loop(0, n_kt, sel_step, init)
    o_s = acc_s * (1.0 / l_s)

    wk = WINDOW + tq
    start = pl.multiple_of(jnp.maximum(q0 - WINDOW, 0), tq)
    s_w = _dot_nt(q, kw_ref[0, 0, pl.ds(start, wk), :])
    d_w = (t_rows - start) - lax.broadcasted_iota(jnp.int32, (1, wk), 1)
    valid_w = (d_w >= 0) & (d_w < WINDOW)
    s_w = jnp.where(valid_w, s_w - slope * d_w.astype(F32), NEG_BIG)
    e_w = jnp.where(valid_w, jnp.exp(s_w - jnp.max(s_w, axis=-1, keepdims=True)), 0.0)
    l_w = jnp.sum(e_w, axis=-1, keepdims=True)
    o_w = _dot(e_w.astype(BF16), vw_ref[0, 0, pl.ds(start, wk), :]) * (1.0 / l_w)

    gt = gates_ref[0, 0]
    outs = []
    for g in range(gqa):
        sl = slice(g * tq, (g + 1) * tq)
        c = g * N_BRANCH
        outs.append(gt[:, c:c + 1] * o_c[sl] + gt[:, c + 1:c + 2] * o_s[sl] + gt[:, c + 2:c + 3] * o_w[sl])
    o_ref[0] = jnp.concatenate(outs, axis=-1).astype(o_ref.dtype)


def _nsa_attention(q, kc, vc, ksa, vs, kw, vw, gates, slopes, ovt, top_k):
    B, n_heads, T, _ = q.shape
    gqa = n_heads // NSA_KV_HEADS
    ncp = kc.shape[2]
    tq = Q_TILE
    rows = gqa * tq
    kv_spec = lambda n, w: pl.BlockSpec((1, 1, n, w), lambda b, h, i: (b, h, 0, 0))
    return pl.pallas_call(
        functools.partial(_nsa_kernel, top_k=top_k),
        grid=(B, NSA_KV_HEADS, T // tq),
        in_specs=[
            pl.BlockSpec((1, gqa, tq, LANES), lambda b, h, i: (b, h, i, 0)),
            kv_spec(ncp, LANES),
            kv_spec(ncp, HEAD_DIM),
            kv_spec(T, LANES),
            kv_spec(T, HEAD_DIM),
            kv_spec(T, LANES),
            kv_spec(T, HEAD_DIM),
            pl.BlockSpec((1, 1, tq, LANES), lambda b, h, i: (b, h, i, 0)),
            pl.BlockSpec((1, rows, 1), lambda b, h, i: (h, 0, 0)),
            pl.BlockSpec((SEL_LANES, 3 * ncp), lambda b, h, i: (0, 0)),
        ],
        out_specs=pl.BlockSpec((1, tq, gqa * HEAD_DIM), lambda b, h, i: (b, i, h)),
        out_shape=jax.ShapeDtypeStruct((B, T, n_heads * HEAD_DIM), BF16),
        compiler_params=pltpu.CompilerParams(
            dimension_semantics=("parallel", "parallel", "arbitrary"), vmem_limit_bytes=VMEM_LIMIT),
        name="nsa_attention",
    )(q, kc, vc, ksa, vs, kw, vw, gates, slopes, ovt)


def _rwkv_prep_kernel(c_ref, cprev_ref, mu_ref, wwa_ref, g2_ref, w0_ref, a0_ref, kk_ref, ka_ref, rk_ref,
                      r_ref, w_ref, k_ref, v_ref, a_ref, b_ref, g_ref, bonus_ref):
    tt = c_ref.shape[1]
    width = r_ref.shape[2]
    n_heads = width // HEAD_DIM
    c = c_ref[0]
    last = jnp.where(pl.program_id(1) > 0, cprev_ref[0][7:8, :], 0.0)
    row = lax.broadcasted_iota(jnp.int32, (tt, 1), 0)
    prev = jnp.where(row == 0, last, pltpu.roll(c, 1, axis=0))
    z = c + (prev - c) * mu_ref[...]
    r = z[:, 0:width]
    k = z[:, width:2 * width]
    v = z[:, 2 * width:3 * width]
    xwa = z[:, 3 * width:3 * width + LANES]
    xg = z[:, 3 * width + LANES:3 * width + 2 * LANES]
    lane = lax.broadcasted_iota(jnp.int32, (1, LANES), 1)
    lora = _dot(jnp.where(lane < DECAY_LORA, jnp.tanh(xwa), xwa).astype(BF16), wwa_ref[...])
    y = w0_ref[...] + lora[:, 0:width]
    softplus_neg = jnp.maximum(-y, 0.0) + jnp.log(1.0 + jnp.exp(-jnp.abs(y)))
    decay = jnp.exp(-jnp.exp(-softplus_neg - 0.5))
    a = _sigmoid(a0_ref[...] + lora[:, width:2 * width])
    g_ref[0] = _dot(_sigmoid(xg).astype(BF16), g2_ref[...])
    kk = k * kk_ref[...]
    k2 = k * (1.0 + (a - 1.0) * ka_ref[...])
    rk = r * k2 * rk_ref[...]
    r_ref[0] = r
    w_ref[0] = decay
    k_ref[0] = k2
    v_ref[0] = v
    for h in range(n_heads):
        sl = slice(h * HEAD_DIM, (h + 1) * HEAD_DIM)
        kkh = kk[:, sl]
        kkh = kkh * lax.rsqrt(jnp.maximum(jnp.sum(kkh * kkh, axis=-1, keepdims=True), 1e-24))
        a_ref[0, :, sl] = -kkh
        b_ref[0, :, sl] = kkh * a[:, sl]
        bonus_ref[0, :, sl] = jnp.sum(rk[:, sl], axis=-1, keepdims=True) * v[:, sl]


def _rwkv_prep(cols, mu, wwa, g2, w0, a0, k_k, k_a, r_k):
    B, T, ncols = cols.shape
    width = w0.shape[1]
    tt = ROW_TILE
    big = pl.BlockSpec((1, tt, width), lambda b, i: (b, i, 0))
    vec = _const_spec((1, width))
    return pl.pallas_call(
        _rwkv_prep_kernel,
        grid=(B, T // tt),
        in_specs=[
            pl.BlockSpec((1, tt, ncols), lambda b, i: (b, i, 0)),
            pl.BlockSpec((1, 8, ncols), lambda b, i: (b, jnp.maximum(i * (tt // 8) - 1, 0), 0)),
            _const_spec((1, ncols)),
            _const_spec((LANES, 2 * width)),
            _const_spec((GATE_LORA, width)),
            vec, vec, vec, vec, vec,
        ],
        out_specs=(big,) * 8,
        out_shape=(jax.ShapeDtypeStruct((B, T, width), F32),) * 8,
        compiler_params=pltpu.CompilerParams(
            dimension_semantics=("parallel", "parallel"), vmem_limit_bytes=VMEM_LIMIT),
        name="rwkv_prep",
    )(cols, cols, mu, wwa, g2, w0, a0, k_k, k_a, r_k)


def _tree_sum(terms):
    while len(terms) > 1:
        nxt = [terms[i] + terms[i + 1] for i in range(0, len(terms) - 1, 2)]
        if len(terms) % 2:
            nxt.append(terms[-1])
        terms = nxt
    return terms[0]


def _sum_sublanes(u):
    u = u + pltpu.roll(u, 4, axis=0)
    u = u + pltpu.roll(u, 2, axis=0)
    return u + pltpu.roll(u, 1, axis=0)


SCAN_ROWS = 4


def _wkv_scan_kernel(r_ref, w_ref, k_ref, v_ref, a_ref, b_ref, y_ref, state_ref):
    tc, nkg = r_ref.shape[0], r_ref.shape[1]
    n_rows = v_ref.shape[1]

    @pl.when(pl.program_id(0) == 0)
    def _():
        state_ref[...] = jnp.zeros_like(state_ref)

    def row_group(gi, _):
        u0 = gi * SCAN_ROWS

        def step(t, state):
            new_state = []
            for j in range(SCAN_ROWS):
                s = state[j * nkg:(j + 1) * nkg]
                v_row = v_ref[t, pl.ds(u0 + j, 1), :]
                sa = _sum_sublanes(_tree_sum([s[g] * a_ref[t, g] for g in range(nkg)]))
                new = [s[g] * w_ref[t, g] + sa * b_ref[t, g] + v_row * k_ref[t, g] for g in range(nkg)]
                y = _sum_sublanes(_tree_sum([new[g] * r_ref[t, g] for g in range(nkg)]))
                y_ref[t, pl.ds(u0 + j, 1), :] = y[0:1]
                new_state.extend(new)
            return tuple(new_state)

        init = tuple(state_ref[u0 + j, g] for j in range(SCAN_ROWS) for g in range(nkg))
        state = lax.fori_loop(0, tc, step, init)
        for j in range(SCAN_ROWS):
            for g in range(nkg):
                state_ref[u0 + j, g] = state[j * nkg + g]
        return 0

    lax.fori_loop(0, n_rows // SCAN_ROWS, row_group, 0)


def _wkv_scan(r, w, k, v, a, b):
    T, nkg = r.shape[0], r.shape[1]
    n_rows = v.shape[1]
    tc = SCAN_CHUNK
    kspec = pl.BlockSpec((tc, nkg, 8, LANES), lambda i: (i, 0, 0, 0))
    vspec = pl.BlockSpec((tc, n_rows, LANES), lambda i: (i, 0, 0))
    return pl.pallas_call(
        _wkv_scan_kernel,
        grid=(T // tc,),
        in_specs=[kspec, kspec, kspec, vspec, kspec, kspec],
        out_specs=vspec,
        out_shape=jax.ShapeDtypeStruct((T, n_rows, LANES), F32),
        scratch_shapes=[pltpu.VMEM((n_rows, nkg, 8, LANES), F32)],
        compiler_params=pltpu.CompilerParams(
            dimension_semantics=("arbitrary",), vmem_limit_bytes=VMEM_LIMIT),
        name="wkv_scan",
    )(r, w, k, v, a, b)


def _out_ffn_kernel(x_ref, ynsa_ref, ys_ref, g_ref, bonus_ref, lnw_ref, lnb_ref, wo_ref, fg_ref, w1_ref, w2_ref,
                    o_ref):
    nsa_w = ynsa_ref.shape[1]
    width = ys_ref.shape[1]
    ys = ys_ref[...]
    pieces = []
    for h in range(width // HEAD_DIM):
        yh = ys[:, h * HEAD_DIM:(h + 1) * HEAD_DIM]
        dev = yh - jnp.mean(yh, axis=-1, keepdims=True)
        pieces.append(dev * lax.rsqrt(jnp.mean(dev * dev, axis=-1, keepdims=True) + GN_EPS))
    yn = jnp.concatenate(pieces, axis=-1) * lnw_ref[...] + lnb_ref[...]
    y_rwkv = ((yn + bonus_ref[...]) * g_ref[...]).astype(BF16)
    x1 = x_ref[...] + (_dot(ynsa_ref[...], wo_ref[0:nsa_w, :]) + _dot(y_rwkv, wo_ref[nsa_w:nsa_w + width, :]))
    xn = _rms(x1, fg_ref[...]).astype(BF16)
    d_ff = w1_ref.shape[1]
    chunk = 1024
    ffn = None
    for c in range(d_ff // chunk):
        hid = _dot(xn, w1_ref[:, c * chunk:(c + 1) * chunk])
        hid = jnp.square(jnp.maximum(hid, 0.0)).astype(BF16)
        part = _dot(hid, w2_ref[c * chunk:(c + 1) * chunk, :])
        ffn = part if ffn is None else ffn + part
    o_ref[...] = x1 + ffn


def _out_ffn(x2, ynsa, ys, g, bonus, lnx_w, lnx_b, w_out, ffn_g, w1, w2):
    N, D = x2.shape
    nsa_w = ynsa.shape[1]
    width = ys.shape[1]
    d_ff = w1.shape[1]
    tm = ROW_TILE
    rowspec = lambda w: pl.BlockSpec((tm, w), lambda i: (i, 0))
    return pl.pallas_call(
        _out_ffn_kernel,
        grid=(N // tm,),
        in_specs=[
            rowspec(D), rowspec(nsa_w), rowspec(width), rowspec(width), rowspec(width),
            _const_spec((1, width)), _const_spec((1, width)),
            _const_spec((nsa_w + width, D)),
            _const_spec((1, D)),
            _const_spec((D, d_ff)),
            _const_spec((d_ff, D)),
        ],
        out_specs=rowspec(D),
        out_shape=jax.ShapeDtypeStruct((N, D), F32),
        compiler_params=pltpu.CompilerParams(
            dimension_semantics=("parallel",), vmem_limit_bytes=VMEM_LIMIT),
        name="out_ffn",
    )(x2, ynsa, ys, g, bonus, lnx_w, lnx_b, w_out, ffn_g, w1, w2)


def _alibi_slopes(n):
    start = 2.0 ** (-8.0 / n)
    return (start ** np.arange(1, n + 1)).astype(np.float32)


def _overlap_t(ncp, n_cmp, n_sel):
    ci = np.arange(ncp)[None, :] * CMP_STRIDE
    sj = np.arange(SEL_LANES)[:, None] * SEL_BLOCK
    ov = (ci <= sj + SEL_BLOCK - 1) & (ci + CMP_BLOCK - 1 >= sj)
    ov &= (np.arange(ncp)[None, :] < n_cmp) & (np.arange(SEL_LANES)[:, None] < n_sel)
    return jnp.asarray(np.tile(ov.astype(np.float32), (1, 3)), dtype=BF16)


def _compress_weights(k_pos, k_w1, k_w2, v_pos, v_w1, v_w2):
    half = CMP_BLOCK // 2
    groups = 2 * NSA_KV_HEADS

    def first_layer(lo):
        w = jnp.zeros((half, groups, HEAD_DIM, groups, CMP_HIDDEN), F32)
        for grp in range(groups):
            w1 = k_w1 if grp < NSA_KV_HEADS else v_w1
            w = w.at[:, grp, :, grp, :].set(w1.reshape(CMP_BLOCK, HEAD_DIM, CMP_HIDDEN)[lo:lo + half])
        return w.reshape(half * groups * HEAD_DIM, groups * CMP_HIDDEN).astype(BF16)

    def pos_row(lo):
        rows = [(k_pos if grp < NSA_KV_HEADS else v_pos)[lo:lo + half] for grp in range(groups)]
        return jnp.stack(rows, axis=1).reshape(1, half * groups * HEAD_DIM)

    w2 = jnp.zeros((groups, CMP_HIDDEN, groups, HEAD_DIM), F32)
    for grp in range(groups):
        w2 = w2.at[grp, :, grp, :].set(k_w2 if grp < NSA_KV_HEADS else v_w2)
    w2 = w2.reshape(groups * CMP_HIDDEN, groups * HEAD_DIM).astype(BF16)
    return first_layer(0), first_layer(half), pos_row(0), pos_row(half), w2


def _to_scan_keys(u, B, T, n_heads):
    u = u.reshape(B, T, n_heads, HEAD_DIM // 8, 8).transpose(1, 3, 4, 0, 2).reshape(T, HEAD_DIM // 8, 8, B * n_heads)
    return jnp.concatenate([u, u], axis=-1)


def _to_scan_values(u, B, T, n_heads):
    u = u.reshape(B, T, n_heads, 2, HEAD_DIM // 2)
    return u.transpose(1, 4, 3, 0, 2).reshape(T, HEAD_DIM // 2, 2 * B * n_heads)


def _from_scan_values(y, B, T, n_heads):
    y = y.reshape(T, HEAD_DIM // 2, 2, B, n_heads)
    return y.transpose(3, 0, 4, 2, 1).reshape(B, T, n_heads * HEAD_DIM)


def _layer(x, ln_mix_g, w_in, nsa_gate_b, q_norm_g, kc_norm_g, ks_norm_g, kw_norm_g,
           cmp_k_pos, cmp_k_w1, cmp_k_w2, cmp_v_pos, cmp_v_w1, cmp_v_w2,
           rwkv_mu, rwkv_w0, rwkv_w2, rwkv_a0, rwkv_a2, rwkv_g2, rwkv_k_k, rwkv_k_a, rwkv_r_k,
           rwkv_lnx_w, rwkv_lnx_b, w_out, ln_ffn_g, w_ff1, w_ff2):
    B, T, D = x.shape
    nsa_w = D // 2
    n_heads = nsa_w // HEAD_DIM
    gqa = n_heads // NSA_KV_HEADS
    kvw = NSA_KV_HEADS * HEAD_DIM
    rw_w = D - nsa_w
    rw_heads = rw_w // HEAD_DIM
    n_sel = T // SEL_BLOCK
    ncp = T // CMP_STRIDE
    n_cmp = (T - CMP_BLOCK) // CMP_STRIDE + 1
    top_k = min(SEL_TOPK, n_sel)
    assert T % K_TILE == 0 and T % ROW_TILE == 0 and n_sel <= SEL_LANES and n_cmp == ncp - 1
    assert B * rw_heads * 2 == LANES and gqa * N_BRANCH <= LANES and T > WINDOW
    assert DECAY_LORA + AAA_LORA == LANES and GATE_LORA == LANES

    row2 = lambda u: u.reshape(1, -1)
    nsa_main = nsa_w + 6 * kvw
    gl = w_in[:, nsa_main:nsa_main + n_heads * N_BRANCH].reshape(D, NSA_KV_HEADS, gqa * N_BRANCH)
    gl = jnp.pad(gl, ((0, 0), (0, 0), (0, LANES - gqa * N_BRANCH))).reshape(D, NSA_KV_HEADS * LANES)
    w_nsa = jnp.concatenate([w_in[:, :nsa_main], gl], axis=1).astype(BF16)
    w_rwkv = w_in[:, nsa_main + n_heads * N_BRANCH:].astype(BF16)
    gate_b2 = jnp.pad(nsa_gate_b.reshape(NSA_KV_HEADS, gqa * N_BRANCH),
                      ((0, 0), (0, LANES - gqa * N_BRANCH))).reshape(1, NSA_KV_HEADS * LANES)

    q, kvc, ksa, vs, kw, vw, gates, rw_cols = _inproj(
        x, row2(ln_mix_g), w_nsa, w_rwkv, row2(q_norm_g), row2(ks_norm_g), row2(kw_norm_g), gate_b2)

    wa, wb, pa, pb, cw2 = _compress_weights(cmp_k_pos, cmp_k_w1, cmp_k_w2, cmp_v_pos, cmp_v_w1, cmp_v_w2)
    kc, vc = _compress(kvc.reshape(B, ncp, CMP_STRIDE * 2 * kvw), wa, wb, pa, pb, cw2, row2(kc_norm_g))
    slopes = np.repeat(_alibi_slopes(n_heads).reshape(NSA_KV_HEADS, gqa), Q_TILE, axis=1)[..., None]
    y_nsa = _nsa_attention(q, kc, vc, ksa, vs, kw, vw, gates, jnp.asarray(slopes),
                           _overlap_t(ncp, n_cmp, n_sel), top_k)

    wwa = jnp.zeros((LANES, 2 * rw_w), F32)
    wwa = wwa.at[:DECAY_LORA, :rw_w].set(rwkv_w2).at[DECAY_LORA:, rw_w:].set(rwkv_a2).astype(BF16)
    r, w, k, v, a, b, g, bonus = _rwkv_prep(
        rw_cols, row2(rwkv_mu), wwa, rwkv_g2.astype(BF16), row2(rwkv_w0), row2(rwkv_a0),
        row2(rwkv_k_k), row2(rwkv_k_a), row2(rwkv_r_k))
    keys = [_to_scan_keys(u, B, T, rw_heads) for u in (r, w, k, a, b)]
    y_scan = _wkv_scan(keys[0], keys[1], keys[2], _to_scan_values(v, B, T, rw_heads), keys[3], keys[4])
    ys = _from_scan_values(y_scan, B, T, rw_heads)

    N = B * T
    out = _out_ffn(x.reshape(N, D), y_nsa.reshape(N, nsa_w), ys.reshape(N, rw_w), g.reshape(N, rw_w),
                   bonus.reshape(N, rw_w), row2(rwkv_lnx_w), row2(rwkv_lnx_b), w_out.astype(BF16),
                   row2(ln_ffn_g), w_ff1.astype(BF16), w_ff2.astype(BF16))
    return out.reshape(B, T, D)


def kernel(x, ln_mix_g, w_in, nsa_gate_b, q_norm_g, kc_norm_g, ks_norm_g, kw_norm_g, cmp_k_pos, cmp_k_w1, cmp_k_w2, cmp_v_pos, cmp_v_w1, cmp_v_w2, rwkv_mu, rwkv_w0, rwkv_w2, rwkv_a0, rwkv_a2, rwkv_g2, rwkv_k_k, rwkv_k_a, rwkv_r_k, rwkv_lnx_w, rwkv_lnx_b, w_out, ln_ffn_g, w_ff1, w_ff2):
    params = (ln_mix_g, w_in, nsa_gate_b, q_norm_g, kc_norm_g, ks_norm_g, kw_norm_g, cmp_k_pos, cmp_k_w1,
              cmp_k_w2, cmp_v_pos, cmp_v_w1, cmp_v_w2, rwkv_mu, rwkv_w0, rwkv_w2, rwkv_a0, rwkv_a2, rwkv_g2,
              rwkv_k_k, rwkv_k_a, rwkv_r_k, rwkv_lnx_w, rwkv_lnx_b, w_out, ln_ffn_g, w_ff1, w_ff2)
    for layer in range(ln_mix_g.shape[0]):
        x = _layer(x, *(p[layer] for p in params))
    return x
```

```python
import functools

import numpy as np
import jax
import jax.numpy as jnp
from jax import lax
from jax.experimental import pallas as pl
from jax.experimental.pallas import tpu as pltpu

F32 = jnp.float32
BF16 = jnp.bfloat16

HEAD_DIM = 64
NSA_KV_HEADS = 2
CMP_BLOCK = 32
CMP_STRIDE = 16
CMP_HIDDEN = 2 * HEAD_DIM
SEL_BLOCK = 64
SEL_SHIFT = 6
SEL_TOPK = 16
WINDOW = 512
N_BRANCH = 3
DECAY_LORA = 64
AAA_LORA = 64
GATE_LORA = 128
RMS_EPS = 1e-6
GN_EPS = HEAD_DIM * 1e-5
NEG_BIG = -1e30
FORCED_SCORE = 1e6

LANES = 128
SEL_LANES = 64
MASK_BIG = 32768.0
VMEM_LIMIT = 56 * 1024 * 1024

ROW_TILE = 512
Q_TILE = 256
K_TILE = 512
SCAN_CHUNK = 64


def _rms(u, g):
    return u * lax.rsqrt(jnp.mean(u * u, axis=-1, keepdims=True) + RMS_EPS) * g


def _sigmoid(u):
    return 1.0 / (1.0 + jnp.exp(-u))


def _dot(a, b):
    return jnp.dot(a, b, preferred_element_type=F32)


def _dot_nt(a, b):
    return lax.dot_general(a, b, (((1,), (1,)), ((), ())), preferred_element_type=F32)


def _const_spec(shape):
    nd = len(shape)
    return pl.BlockSpec(shape, lambda *_: (0,) * nd, pipeline_mode=pl.Buffered(1))


def _inproj_kernel(x_ref, g_ref, wn_ref, wr_ref, qg_ref, ksg_ref, kwg_ref, gb_ref,
                   q_ref, kvc_ref, ksa_ref, vs_ref, kw_ref, vw_ref, gates_ref, rw_ref):
    tm = x_ref.shape[1]
    n_q_heads = q_ref.shape[1]
    nsa_w = n_q_heads * HEAD_DIM
    kvw = NSA_KV_HEADS * HEAD_DIM
    xn = _rms(x_ref[0], g_ref[...]).astype(BF16)
    pn = _dot(xn, wn_ref[...])
    rw_ref[0] = _dot(xn, wr_ref[...])

    zeros_hd = jnp.zeros((tm, HEAD_DIM), F32)
    for h in range(n_q_heads):
        qh = _rms(pn[:, h * HEAD_DIM:(h + 1) * HEAD_DIM], qg_ref[...]) * (HEAD_DIM ** -0.5)
        q_ref[0, h] = jnp.concatenate([qh, zeros_hd], axis=-1).astype(BF16)

    kvc_ref[0] = pn[:, nsa_w:nsa_w + 2 * kvw]

    tok = pl.program_id(1) * tm + lax.broadcasted_iota(jnp.int32, (tm, 1), 0)
    blk = lax.shift_right_logical(tok, SEL_SHIFT)
    onehot = jnp.where(blk == lax.broadcasted_iota(jnp.int32, (1, SEL_LANES), 1), MASK_BIG, 0.0)
    pos = _pos_tile(tok)
    off = nsa_w + 2 * kvw
    for h in range(NSA_KV_HEADS):
        ksh = _rms(pn[:, off + h * HEAD_DIM:off + (h + 1) * HEAD_DIM], ksg_ref[...])
        ksa_ref[0, h] = jnp.concatenate([ksh, onehot, pos], axis=-1).astype(BF16)
        o2 = off + kvw
        vs_ref[0, h] = pn[:, o2 + h * HEAD_DIM:o2 + (h + 1) * HEAD_DIM].astype(BF16)
        o3 = off + 2 * kvw
        kwh = _rms(pn[:, o3 + h * HEAD_DIM:o3 + (h + 1) * HEAD_DIM], kwg_ref[...])
        kw_ref[0, h] = (jnp.concatenate([kwh, zeros_hd], axis=-1) + pos).astype(BF16)
        o4 = off + 3 * kvw
        vw_ref[0, h] = pn[:, o4 + h * HEAD_DIM:o4 + (h + 1) * HEAD_DIM].astype(BF16)
        o5 = off + 4 * kvw + h * LANES
        gates_ref[0, h] = _sigmoid(pn[:, o5:o5 + LANES] + gb_ref[:, h * LANES:(h + 1) * LANES])


def _inproj(x, ln_g, w_nsa, w_rwkv, q_g, ks_g, kw_g, gate_b2):
    B, T, D = x.shape
    nsa_cols = w_nsa.shape[1]
    rw_cols = w_rwkv.shape[1]
    n_q_heads = (nsa_cols - 2 * LANES - 6 * NSA_KV_HEADS * HEAD_DIM) // HEAD_DIM
    tm = ROW_TILE
    kvw = NSA_KV_HEADS * HEAD_DIM
    out_shape = (
        jax.ShapeDtypeStruct((B, n_q_heads, T, LANES), BF16),
        jax.ShapeDtypeStruct((B, T, 2 * kvw), F32),
        jax.ShapeDtypeStruct((B, NSA_KV_HEADS, T, 2 * LANES), BF16),
        jax.ShapeDtypeStruct((B, NSA_KV_HEADS, T, HEAD_DIM), BF16),
        jax.ShapeDtypeStruct((B, NSA_KV_HEADS, T, LANES), BF16),
        jax.ShapeDtypeStruct((B, NSA_KV_HEADS, T, HEAD_DIM), BF16),
        jax.ShapeDtypeStruct((B, NSA_KV_HEADS, T, LANES), F32),
        jax.ShapeDtypeStruct((B, T, rw_cols), F32),
    )
    head_spec = lambda n, w: pl.BlockSpec((1, n, tm, w), lambda b, i: (b, 0, i, 0))
    return pl.pallas_call(
        _inproj_kernel,
        grid=(B, T // tm),
        in_specs=[
            pl.BlockSpec((1, tm, D), lambda b, i: (b, i, 0)),
            _const_spec((1, D)),
            _const_spec((D, nsa_cols)),
            _const_spec((D, rw_cols)),
            _const_spec((1, HEAD_DIM)),
            _const_spec((1, HEAD_DIM)),
            _const_spec((1, HEAD_DIM)),
            _const_spec((1, NSA_KV_HEADS * LANES)),
        ],
        out_specs=(
            head_spec(n_q_heads, LANES),
            pl.BlockSpec((1, tm, 2 * kvw), lambda b, i: (b, i, 0)),
            head_spec(NSA_KV_HEADS, 2 * LANES),
            head_spec(NSA_KV_HEADS, HEAD_DIM),
            head_spec(NSA_KV_HEADS, LANES),
            head_spec(NSA_KV_HEADS, HEAD_DIM),
            head_spec(NSA_KV_HEADS, LANES),
            pl.BlockSpec((1, tm, rw_cols), lambda b, i: (b, i, 0)),
        ),
        out_shape=out_shape,
        compiler_params=pltpu.CompilerParams(
            dimension_semantics=("parallel", "parallel"), vmem_limit_bytes=VMEM_LIMIT),
        name="inproj",
    )(x, ln_g, w_nsa, w_rwkv, q_g, ks_g, kw_g, gate_b2)


def _gelu_tanh(u):
    return 0.5 * u * (1.0 + jnp.tanh(np.sqrt(2.0 / np.pi).astype(np.float32) * (u + 0.044715 * (u * u * u))))


def _compress_kernel(seg_ref, wa_ref, wb_ref, pa_ref, pb_ref, w2_ref, kcg_ref, kc_ref, vc_ref):
    ncp = seg_ref.shape[1]
    seg = seg_ref[0]
    first = _dot((seg + pa_ref[...]).astype(BF16), wa_ref[...])
    second = _dot((seg + pb_ref[...]).astype(BF16), wb_ref[...])
    row = lax.broadcasted_iota(jnp.int32, (ncp, 1), 0)
    real = row < ncp - 1
    second = jnp.where(real, pltpu.roll(second, ncp - 1, axis=0), 0.0)
    hid = _gelu_tanh(first + second)
    out = jnp.where(real, _dot(hid.astype(BF16), w2_ref[...]), 0.0)
    zeros_hd = jnp.zeros((ncp, HEAD_DIM), F32)
    pos = _pos_tile(row * CMP_STRIDE + (CMP_BLOCK - 1))
    for h in range(NSA_KV_HEADS):
        kch = _rms(out[:, h * HEAD_DIM:(h + 1) * HEAD_DIM], kcg_ref[...])
        kc_ref[0, h] = (jnp.concatenate([kch, zeros_hd], axis=-1) + pos).astype(BF16)
        o = (NSA_KV_HEADS + h) * HEAD_DIM
        vc_ref[0, h] = out[:, o:o + HEAD_DIM].astype(BF16)


def _compress(seg, wa, wb, pa, pb, w2, kc_g):
    B, ncp, width = seg.shape
    hid = wa.shape[1]
    return pl.pallas_call(
        _compress_kernel,
        grid=(B,),
        in_specs=[
            pl.BlockSpec((1, ncp, width), lambda b: (b, 0, 0)),
            _const_spec((width, hid)),
            _const_spec((width, hid)),
            _const_spec((1, width)),
            _const_spec((1, width)),
            _const_spec((hid, 2 * NSA_KV_HEADS * HEAD_DIM)),
            _const_spec((1, HEAD_DIM)),
        ],
        out_specs=(
            pl.BlockSpec((1, NSA_KV_HEADS, ncp, LANES), lambda b: (b, 0, 0, 0)),
            pl.BlockSpec((1, NSA_KV_HEADS, ncp, HEAD_DIM), lambda b: (b, 0, 0, 0)),
        ),
        out_shape=(
            jax.ShapeDtypeStruct((B, NSA_KV_HEADS, ncp, LANES), BF16),
            jax.ShapeDtypeStruct((B, NSA_KV_HEADS, ncp, HEAD_DIM), BF16),
        ),
        compiler_params=pltpu.CompilerParams(
            dimension_semantics=("parallel",), vmem_limit_bytes=VMEM_LIMIT),
        name="compress",
    )(seg, wa, wb, pa, pb, w2, kc_g)


def _split3_bf16(u):
    hi = u.astype(BF16)
    r1 = u - hi.astype(F32)
    mid = r1.astype(BF16)
    lo = (r1 - mid.astype(F32)).astype(BF16)
    return hi, mid, lo


def _pos_tile(pos):
    lane = lax.broadcasted_iota(jnp.int32, (1, LANES), 1)
    hi = lax.shift_left(lax.shift_right_logical(pos, SEL_SHIFT), SEL_SHIFT).astype(F32)
    lo = (pos & (SEL_BLOCK - 1)).astype(F32)
    first = HEAD_DIM
    return jnp.where((lane >= first) & (lane < first + 3), hi,
                     jnp.where((lane >= first + 3) & (lane < first + 6), lo, 0.0))


def _nsa_kernel(q_ref, kc_ref, vc_ref, ksa_ref, vs_ref, kw_ref, vw_ref, gates_ref, slopeq_ref, ovt_ref,
                o_ref, *, top_k):
    gqa, tq = q_ref.shape[1], q_ref.shape[2]
    ncp = kc_ref.shape[2]
    rows = gqa * tq
    q0 = pl.program_id(2) * tq
    q = q_ref[0].reshape(rows, LANES)
    slope_q = slopeq_ref[0]
    q_pos = q + slope_q
    row = lax.broadcasted_iota(jnp.int32, (rows, 1), 0)
    t_rows = q0 + (row & (tq - 1))

    cend = lax.broadcasted_iota(jnp.int32, (1, ncp), 1) * CMP_STRIDE + (CMP_BLOCK - 1)
    valid_c = t_rows >= cend
    s_c = jnp.where(valid_c, _dot_nt(q_pos, kc_ref[0, 0]), NEG_BIG)
    e_c = jnp.where(valid_c, jnp.exp(s_c - jnp.max(s_c, axis=-1, keepdims=True)), 0.0)
    l_c = jnp.sum(e_c, axis=-1, keepdims=True)
    p_c = e_c * (1.0 / jnp.where(l_c > 0.0, l_c, 1.0))
    o_c = _dot(p_c.astype(BF16), vc_ref[0, 0])

    wk = WINDOW + tq
    start = pl.multiple_of(jnp.maximum(q0 - WINDOW, 0), tq)
    d_w = (t_rows - start) - lax.broadcasted_iota(jnp.int32, (1, wk), 1)
    valid_w = lax.bitcast_convert_type(d_w, jnp.uint32) < jnp.uint32(WINDOW)
    s_w = jnp.where(valid_w, _dot_nt(q_pos, kw_ref[0, 0, pl.ds(start, wk), :]), NEG_BIG)
    e_w = jnp.exp(s_w - jnp.max(s_w, axis=-1, keepdims=True))
    l_w = jnp.sum(e_w, axis=-1, keepdims=True)
    o_w = _dot(e_w.astype(BF16), vw_ref[0, 0, pl.ds(start, wk), :]) * (1.0 / l_w)

    p_sum = p_c[0:tq]
    for g in range(1, gqa):
        p_sum = p_sum + p_c[g * tq:(g + 1) * tq]
    imp_t = _dot_nt(ovt_ref[...], jnp.concatenate(_split3_bf16(p_sum), axis=-1))

    jb = lax.broadcasted_iota(jnp.int32, (SEL_LANES, 1), 0)
    jb_f = jb.astype(F32)
    cur = lax.shift_right_logical(q0 + lax.broadcasted_iota(jnp.int32, (1, tq), 1), SEL_SHIFT)
    forced = (jb == 0) | (jb == cur) | (jb == cur - 1)
    work = jnp.where(forced, FORCED_SCORE, jnp.where(jb > cur, NEG_BIG, imp_t))
    sel_t = jnp.zeros((SEL_LANES, tq), F32)
    for _ in range(top_k):
        mx = jnp.max(work, axis=0, keepdims=True)
        idx = jnp.min(jnp.where(work == mx, jb_f, float(SEL_LANES)), axis=0, keepdims=True)
        pick = jb_f == idx
        sel_t = jnp.where(pick, 1.0, sel_t)
        work = jnp.where(pick, -jnp.inf, work)
    pad_t = jnp.concatenate([jnp.zeros((SEL_LANES, tq), F32), sel_t - 1.0], axis=0)
    sel_q = pad_t.T
    q_sel = (q.astype(F32) + jnp.concatenate([sel_q] * gqa, axis=0)).astype(BF16)
    q_aug = jnp.concatenate([q_sel, slope_q], axis=-1)

    tk = K_TILE
    n_full = q0 // tk

    def scores(kt):
        return _dot_nt(q_aug, ksa_ref[0, 0, pl.ds(pl.multiple_of(kt * tk, tk), tk), :])

    def update(s, kt, carry, causal):
        m_run, l_run, acc = carry
        k0 = pl.multiple_of(kt * tk, tk)
        if causal:
            s = jnp.where(lax.broadcasted_iota(jnp.int32, (1, tk), 1) <= t_rows - k0, s, NEG_BIG)
        m_new = jnp.maximum(m_run, jnp.max(s, axis=-1, keepdims=True))
        alpha = jnp.exp(m_run - m_new)
        p = jnp.exp(s - m_new)
        l_new = alpha * l_run + jnp.sum(p, axis=-1, keepdims=True)
        return m_new, l_new, alpha * acc + _dot(p.astype(BF16), vs_ref[0, 0, pl.ds(k0, tk), :])

    def pair(kt, carry, causal_second):
        s_a, s_b = scores(kt), scores(kt + 1)
        return update(s_b, kt + 1, update(s_a, kt, carry, False), causal_second)

    init = (jnp.full((rows, 1), NEG_BIG, F32), jnp.zeros((rows, 1), F32), jnp.zeros((rows, HEAD_DIM), F32))
    carry = lax.fori_loop(0, n_full // 2, lambda j, c: pair(2 * j, c, False), init)
    _, l_s, acc_s = lax.cond(
        (n_full & 1) == 1,
        lambda c: pair(n_full - 1, c, True),
        lambda c: update(scores(n_full), n_full, c, True),
        carry)
    o_s = acc_s * (1.0 / l_s)

    gt = gates_ref[0, 0]
    outs = []
    for g in range(gqa):
        sl = slice(g * tq, (g + 1) * tq)
        c = g * N_BRANCH
        outs.append(gt[:, c:c + 1] * o_c[sl] + gt[:, c + 1:c + 2] * o_s[sl] + gt[:, c + 2:c + 3] * o_w[sl])
    o_ref[0] = jnp.concatenate(outs, axis=-1).astype(o_ref.dtype)


def _nsa_attention(q, kc, vc, ksa, vs, kw, vw, gates, slope_q, ovt, top_k):
    B, n_heads, T, _ = q.shape
    gqa = n_heads // NSA_KV_HEADS
    ncp = kc.shape[2]
    tq = Q_TILE
    rows = gqa * tq
    kv_spec = lambda n, w: pl.BlockSpec((1, 1, n, w), lambda b, h, i: (b, h, 0, 0))
    return pl.pallas_call(
        functools.partial(_nsa_kernel, top_k=top_k),
        grid=(B, NSA_KV_HEADS, T // tq),
        in_specs=[
            pl.BlockSpec((1, gqa, tq, LANES), lambda b, h, i: (b, h, i, 0)),
            kv_spec(ncp, LANES),
            kv_spec(ncp, HEAD_DIM),
            kv_spec(T, 2 * LANES),
            kv_spec(T, HEAD_DIM),
            kv_spec(T, LANES),
            kv_spec(T, HEAD_DIM),
            pl.BlockSpec((1, 1, tq, LANES), lambda b, h, i: (b, h, i, 0)),
            pl.BlockSpec((1, rows, LANES), lambda b, h, i: (h, 0, 0)),
            pl.BlockSpec((SEL_LANES, 3 * ncp), lambda b, h, i: (0, 0)),
        ],
        out_specs=pl.BlockSpec((1, tq, gqa * HEAD_DIM), lambda b, h, i: (b, i, h)),
        out_shape=jax.ShapeDtypeStruct((B, T, n_heads * HEAD_DIM), BF16),
        compiler_params=pltpu.CompilerParams(
            dimension_semantics=("parallel", "parallel", "arbitrary"), vmem_limit_bytes=VMEM_LIMIT),
        name="nsa_attention",
    )(q, kc, vc, ksa, vs, kw, vw, gates, slope_q, ovt)


def _rwkv_prep_kernel(c_ref, cprev_ref, mu_ref, wwa_ref, g2_ref, w0_ref, a0_ref, kk_ref, ka_ref, rk_ref,
                      r_ref, w_ref, k_ref, v_ref, a_ref, b_ref, g_ref, bonus_ref):
    tt = c_ref.shape[1]
    width = r_ref.shape[2]
    n_heads = width // HEAD_DIM
    c = c_ref[0]
    last = jnp.where(pl.program_id(1) > 0, cprev_ref[0][7:8, :], 0.0)
    row = lax.broadcasted_iota(jnp.int32, (tt, 1), 0)
    prev = jnp.where(row == 0, last, pltpu.roll(c, 1, axis=0))
    z = c + (prev - c) * mu_ref[...]
    r = z[:, 0:width]
    k = z[:, width:2 * width]
    v = z[:, 2 * width:3 * width]
    xwa = z[:, 3 * width:3 * width + LANES]
    xg = z[:, 3 * width + LANES:3 * width + 2 * LANES]
    lane = lax.broadcasted_iota(jnp.int32, (1, LANES), 1)
    lora = _dot(jnp.where(lane < DECAY_LORA, jnp.tanh(xwa), xwa).astype(BF16), wwa_ref[...])
    y = w0_ref[...] + lora[:, 0:width]
    softplus_neg = jnp.maximum(-y, 0.0) + jnp.log(1.0 + jnp.exp(-jnp.abs(y)))
    decay = jnp.exp(-jnp.exp(-softplus_neg - 0.5))
    a = _sigmoid(a0_ref[...] + lora[:, width:2 * width])
    g_ref[0] = _dot(_sigmoid(xg).astype(BF16), g2_ref[...])
    kk = k * kk_ref[...]
    k2 = k * (1.0 + (a - 1.0) * ka_ref[...])
    rk = r * k2 * rk_ref[...]
    r_ref[0] = r
    w_ref[0] = decay
    k_ref[0] = k2
    v_ref[0] = v
    for h in range(n_heads):
        sl = slice(h * HEAD_DIM, (h + 1) * HEAD_DIM)
        kkh = kk[:, sl]
        kkh = kkh * lax.rsqrt(jnp.maximum(jnp.sum(kkh * kkh, axis=-1, keepdims=True), 1e-24))
        a_ref[0, :, sl] = -kkh
        b_ref[0, :, sl] = kkh * a[:, sl]
        bonus_ref[0, :, sl] = jnp.sum(rk[:, sl], axis=-1, keepdims=True) * v[:, sl]


def _rwkv_prep(cols, mu, wwa, g2, w0, a0, k_k, k_a, r_k):
    B, T, ncols = cols.shape
    width = w0.shape[1]
    tt = ROW_TILE
    big = pl.BlockSpec((1, tt, width), lambda b, i: (b, i, 0))
    vec = _const_spec((1, width))
    return pl.pallas_call(
        _rwkv_prep_kernel,
        grid=(B, T // tt),
        in_specs=[
            pl.BlockSpec((1, tt, ncols), lambda b, i: (b, i, 0)),
            pl.BlockSpec((1, 8, ncols), lambda b, i: (b, jnp.maximum(i * (tt // 8) - 1, 0), 0)),
            _const_spec((1, ncols)),
            _const_spec((LANES, 2 * width)),
            _const_spec((GATE_LORA, width)),
            vec, vec, vec, vec, vec,
        ],
        out_specs=(big,) * 8,
        out_shape=(jax.ShapeDtypeStruct((B, T, width), F32),) * 8,
        compiler_params=pltpu.CompilerParams(
            dimension_semantics=("parallel", "parallel"), vmem_limit_bytes=VMEM_LIMIT),
        name="rwkv_prep",
    )(cols, cols, mu, wwa, g2, w0, a0, k_k, k_a, r_k)


def _tree_sum(terms):
    while len(terms) > 1:
        nxt = [terms[i] + terms[i + 1] for i in range(0, len(terms) - 1, 2)]
        if len(terms) % 2:
            nxt.append(terms[-1])
        terms = nxt
    return terms[0]


def _sum_sublanes(u):
    u = u + pltpu.roll(u, 4, axis=0)
    u = u + pltpu.roll(u, 2, axis=0)
    return u + pltpu.roll(u, 1, axis=0)


SCAN_ROWS = 4


def _wkv_scan_kernel(r_in, w_in, k_in, v_ref, a_in, b_in, y_ref, state_ref, r_ref, w_ref, k_ref, a_ref, b_ref):
    tc, nkg = r_in.shape[0], r_in.shape[1]
    n_rows = v_ref.shape[1]

    @pl.when(pl.program_id(0) == 0)
    def _():
        state_ref[...] = jnp.zeros_like(state_ref)

    def widen(t, _):
        for src, dst in ((r_in, r_ref), (w_in, w_ref), (k_in, k_ref), (a_in, a_ref), (b_in, b_ref)):
            u = src[t]
            dst[t] = jnp.concatenate([u, u], axis=-1)
        return 0

    lax.fori_loop(0, tc, widen, 0, unroll=8)

    def row_group(gi, _):
        u0 = gi * SCAN_ROWS

        def step(t, state):
            new_state = []
            for j in range(SCAN_ROWS):
                s = state[j * nkg:(j + 1) * nkg]
                v_row = v_ref[t, pl.ds(u0 + j, 1), :]
                sa = _sum_sublanes(_tree_sum([s[g] * a_ref[t, g] for g in range(nkg)]))
                new = [s[g] * w_ref[t, g] + sa * b_ref[t, g] + v_row * k_ref[t, g] for g in range(nkg)]
                y = _sum_sublanes(_tree_sum([new[g] * r_ref[t, g] for g in range(nkg)]))
                y_ref[t, pl.ds(u0 + j, 1), :] = y[0:1]
                new_state.extend(new)
            return tuple(new_state)

        init = tuple(state_ref[u0 + j, g] for j in range(SCAN_ROWS) for g in range(nkg))
        state = lax.fori_loop(0, tc, step, init)
        for j in range(SCAN_ROWS):
            for g in range(nkg):
                state_ref[u0 + j, g] = state[j * nkg + g]
        return 0

    lax.fori_loop(0, n_rows // SCAN_ROWS, row_group, 0)


def _wkv_scan(r, w, k, v, a, b):
    T, nkg = r.shape[0], r.shape[1]
    n_rows = v.shape[1]
    tc = SCAN_CHUNK
    kspec = pl.BlockSpec((tc, nkg, 8, LANES // 2), lambda i: (i, 0, 0, 0))
    vspec = pl.BlockSpec((tc, n_rows, LANES), lambda i: (i, 0, 0))
    wide = pltpu.VMEM((tc, nkg, 8, LANES), F32)
    return pl.pallas_call(
        _wkv_scan_kernel,
        grid=(T // tc,),
        in_specs=[kspec, kspec, kspec, vspec, kspec, kspec],
        out_specs=vspec,
        out_shape=jax.ShapeDtypeStruct((T, n_rows, LANES), F32),
        scratch_shapes=[pltpu.VMEM((n_rows, nkg, 8, LANES), F32), wide, wide, wide, wide, wide],
        compiler_params=pltpu.CompilerParams(
            dimension_semantics=("arbitrary",), vmem_limit_bytes=VMEM_LIMIT),
        name="wkv_scan",
    )(r, w, k, v, a, b)


def _out_ffn_kernel(x_ref, ynsa_ref, ys_ref, g_ref, bonus_ref, lnw_ref, lnb_ref, wo_ref, fg_ref, w1_ref, w2_ref,
                    o_ref):
    nsa_w = ynsa_ref.shape[1]
    width = ys_ref.shape[1]
    ys = ys_ref[...]
    pieces = []
    for h in range(width // HEAD_DIM):
        yh = ys[:, h * HEAD_DIM:(h + 1) * HEAD_DIM]
        dev = yh - jnp.mean(yh, axis=-1, keepdims=True)
        pieces.append(dev * lax.rsqrt(jnp.mean(dev * dev, axis=-1, keepdims=True) + GN_EPS))
    yn = jnp.concatenate(pieces, axis=-1) * lnw_ref[...] + lnb_ref[...]
    y_rwkv = ((yn + bonus_ref[...]) * g_ref[...]).astype(BF16)
    x1 = x_ref[...] + (_dot(ynsa_ref[...], wo_ref[0:nsa_w, :]) + _dot(y_rwkv, wo_ref[nsa_w:nsa_w + width, :]))
    xn = _rms(x1, fg_ref[...]).astype(BF16)
    d_ff = w1_ref.shape[1]
    chunk = 1024
    ffn = None
    for c in range(d_ff // chunk):
        hid = _dot(xn, w1_ref[:, c * chunk:(c + 1) * chunk])
        hid = jnp.square(jnp.maximum(hid, 0.0)).astype(BF16)
        part = _dot(hid, w2_ref[c * chunk:(c + 1) * chunk, :])
        ffn = part if ffn is None else ffn + part
    o_ref[...] = x1 + ffn


def _out_ffn(x2, ynsa, ys, g, bonus, lnx_w, lnx_b, w_out, ffn_g, w1, w2):
    N, D = x2.shape
    nsa_w = ynsa.shape[1]
    width = ys.shape[1]
    d_ff = w1.shape[1]
    tm = ROW_TILE
    rowspec = lambda w: pl.BlockSpec((tm, w), lambda i: (i, 0))
    return pl.pallas_call(
        _out_ffn_kernel,
        grid=(N // tm,),
        in_specs=[
            rowspec(D), rowspec(nsa_w), rowspec(width), rowspec(width), rowspec(width),
            _const_spec((1, width)), _const_spec((1, width)),
            _const_spec((nsa_w + width, D)),
            _const_spec((1, D)),
            _const_spec((D, d_ff)),
            _const_spec((d_ff, D)),
        ],
        out_specs=rowspec(D),
        out_shape=jax.ShapeDtypeStruct((N, D), F32),
        compiler_params=pltpu.CompilerParams(
            dimension_semantics=("parallel",), vmem_limit_bytes=VMEM_LIMIT),
        name="out_ffn",
    )(x2, ynsa, ys, g, bonus, lnx_w, lnx_b, w_out, ffn_g, w1, w2)


def _alibi_slopes(n):
    start = 2.0 ** (-8.0 / n)
    return (start ** np.arange(1, n + 1)).astype(np.float32)


def _slope_lanes(n_heads):
    m = jnp.asarray(_alibi_slopes(n_heads))
    pieces = jnp.stack(_split3_bf16(m), axis=-1).astype(F32)
    lanes = jnp.zeros((n_heads, LANES), F32).at[:, HEAD_DIM:HEAD_DIM + 3].set(pieces)
    lanes = lanes.at[:, HEAD_DIM + 3:HEAD_DIM + 6].set(pieces)
    gqa = n_heads // NSA_KV_HEADS
    return jnp.repeat(lanes.reshape(NSA_KV_HEADS, gqa, LANES), Q_TILE, axis=1).astype(BF16)


def _overlap_t(ncp, n_cmp, n_sel):
    ci = np.arange(ncp)[None, :] * CMP_STRIDE
    sj = np.arange(SEL_LANES)[:, None] * SEL_BLOCK
    ov = (ci <= sj + SEL_BLOCK - 1) & (ci + CMP_BLOCK - 1 >= sj)
    ov &= (np.arange(ncp)[None, :] < n_cmp) & (np.arange(SEL_LANES)[:, None] < n_sel)
    return jnp.asarray(np.tile(ov.astype(np.float32), (1, 3)), dtype=BF16)


def _compress_weights(k_pos, k_w1, k_w2, v_pos, v_w1, v_w2):
    half = CMP_BLOCK // 2
    groups = 2 * NSA_KV_HEADS

    def first_layer(lo):
        w = jnp.zeros((half, groups, HEAD_DIM, groups, CMP_HIDDEN), F32)
        for grp in range(groups):
            w1 = k_w1 if grp < NSA_KV_HEADS else v_w1
            w = w.at[:, grp, :, grp, :].set(w1.reshape(CMP_BLOCK, HEAD_DIM, CMP_HIDDEN)[lo:lo + half])
        return w.reshape(half * groups * HEAD_DIM, groups * CMP_HIDDEN).astype(BF16)

    def pos_row(lo):
        rows = [(k_pos if grp < NSA_KV_HEADS else v_pos)[lo:lo + half] for grp in range(groups)]
        return jnp.stack(rows, axis=1).reshape(1, half * groups * HEAD_DIM)

    w2 = jnp.zeros((groups, CMP_HIDDEN, groups, HEAD_DIM), F32)
    for grp in range(groups):
        w2 = w2.at[grp, :, grp, :].set(k_w2 if grp < NSA_KV_HEADS else v_w2)
    w2 = w2.reshape(groups * CMP_HIDDEN, groups * HEAD_DIM).astype(BF16)
    return first_layer(0), first_layer(half), pos_row(0), pos_row(half), w2


def _to_scan_keys(u, B, T, n_heads):
    return u.reshape(B, T, n_heads, HEAD_DIM // 8, 8).transpose(1, 3, 4, 0, 2).reshape(T, HEAD_DIM // 8, 8, B * n_heads)


def _to_scan_values(u, B, T, n_heads):
    u = u.reshape(B, T, n_heads, 2, HEAD_DIM // 2)
    return u.transpose(1, 4, 3, 0, 2).reshape(T, HEAD_DIM // 2, 2 * B * n_heads)


def _from_scan_values(y, B, T, n_heads):
    y = y.reshape(T, HEAD_DIM // 2, 2, B, n_heads)
    return y.transpose(3, 0, 4, 2, 1).reshape(B, T, n_heads * HEAD_DIM)


def _layer(x, ln_mix_g, w_in, nsa_gate_b, q_norm_g, kc_norm_g, ks_norm_g, kw_norm_g,
           cmp_k_pos, cmp_k_w1, cmp_k_w2, cmp_v_pos, cmp_v_w1, cmp_v_w2,
           rwkv_mu, rwkv_w0, rwkv_w2, rwkv_a0, rwkv_a2, rwkv_g2, rwkv_k_k, rwkv_k_a, rwkv_r_k,
           rwkv_lnx_w, rwkv_lnx_b, w_out, ln_ffn_g, w_ff1, w_ff2):
    B, T, D = x.shape
    nsa_w = D // 2
    n_heads = nsa_w // HEAD_DIM
    gqa = n_heads // NSA_KV_HEADS
    kvw = NSA_KV_HEADS * HEAD_DIM
    rw_w = D - nsa_w
    rw_heads = rw_w // HEAD_DIM
    n_sel = T // SEL_BLOCK
    ncp = T // CMP_STRIDE
    n_cmp = (T - CMP_BLOCK) // CMP_STRIDE + 1
    top_k = min(SEL_TOPK, n_sel)
    assert T % K_TILE == 0 and T % ROW_TILE == 0 and n_sel <= SEL_LANES and n_cmp == ncp - 1
    assert B * rw_heads * 2 == LANES and gqa * N_BRANCH <= LANES and T > WINDOW
    assert DECAY_LORA + AAA_LORA == LANES and GATE_LORA == LANES

    row2 = lambda u: u.reshape(1, -1)
    nsa_main = nsa_w + 6 * kvw
    gl = w_in[:, nsa_main:nsa_main + n_heads * N_BRANCH].reshape(D, NSA_KV_HEADS, gqa * N_BRANCH)
    gl = jnp.pad(gl, ((0, 0), (0, 0), (0, LANES - gqa * N_BRANCH))).reshape(D, NSA_KV_HEADS * LANES)
    w_nsa = jnp.concatenate([w_in[:, :nsa_main], gl], axis=1).astype(BF16)
    w_rwkv = w_in[:, nsa_main + n_heads * N_BRANCH:].astype(BF16)
    gate_b2 = jnp.pad(nsa_gate_b.reshape(NSA_KV_HEADS, gqa * N_BRANCH),
                      ((0, 0), (0, LANES - gqa * N_BRANCH))).reshape(1, NSA_KV_HEADS * LANES)

    q, kvc, ksa, vs, kw, vw, gates, rw_cols = _inproj(
        x, row2(ln_mix_g), w_nsa, w_rwkv, row2(q_norm_g), row2(ks_norm_g), row2(kw_norm_g), gate_b2)

    wa, wb, pa, pb, cw2 = _compress_weights(cmp_k_pos, cmp_k_w1, cmp_k_w2, cmp_v_pos, cmp_v_w1, cmp_v_w2)
    kc, vc = _compress(kvc.reshape(B, ncp, CMP_STRIDE * 2 * kvw), wa, wb, pa, pb, cw2, row2(kc_norm_g))
    y_nsa = _nsa_attention(q, kc, vc, ksa, vs, kw, vw, gates, _slope_lanes(n_heads),
                           _overlap_t(ncp, n_cmp, n_sel), top_k)

    wwa = jnp.zeros((LANES, 2 * rw_w), F32)
    wwa = wwa.at[:DECAY_LORA, :rw_w].set(rwkv_w2).at[DECAY_LORA:, rw_w:].set(rwkv_a2).astype(BF16)
    r, w, k, v, a, b, g, bonus = _rwkv_prep(
        rw_cols, row2(rwkv_mu), wwa, rwkv_g2.astype(BF16), row2(rwkv_w0), row2(rwkv_a0),
        row2(rwkv_k_k), row2(rwkv_k_a), row2(rwkv_r_k))
    keys = [_to_scan_keys(u, B, T, rw_heads) for u in (r, w, k, a, b)]
    y_scan = _wkv_scan(keys[0], keys[1], keys[2], _to_scan_values(v, B, T, rw_heads), keys[3], keys[4])
    ys = _from_scan_values(y_scan, B, T, rw_heads)

    N = B * T
    out = _out_ffn(x.reshape(N, D), y_nsa.reshape(N, nsa_w), ys.reshape(N, rw_w), g.reshape(N, rw_w),
                   bonus.reshape(N, rw_w), row2(rwkv_lnx_w), row2(rwkv_lnx_b), w_out.astype(BF16),
                   row2(ln_ffn_g), w_ff1.astype(BF16), w_ff2.astype(BF16))
    return out.reshape(B, T, D)


def kernel(x, ln_mix_g, w_in, nsa_gate_b, q_norm_g, kc_norm_g, ks_norm_g, kw_norm_g, cmp_k_pos, cmp_k_w1, cmp_k_w2, cmp_v_pos, cmp_v_w1, cmp_v_w2, rwkv_mu, rwkv_w0, rwkv_w2, rwkv_a0, rwkv_a2, rwkv_g2, rwkv_k_k, rwkv_k_a, rwkv_r_k, rwkv_lnx_w, rwkv_lnx_b, w_out, ln_ffn_g, w_ff1, w_ff2):
    params = (ln_mix_g, w_in, nsa_gate_b, q_norm_g, kc_norm_g, ks_norm_g, kw_norm_g, cmp_k_pos, cmp_k_w1,
              cmp_k_w2, cmp_v_pos, cmp_v_w1, cmp_v_w2, rwkv_mu, rwkv_w0, rwkv_w2, rwkv_a0, rwkv_a2, rwkv_g2,
              rwkv_k_k, rwkv_k_a, rwkv_r_k, rwkv_lnx_w, rwkv_lnx_b, w_out, ln_ffn_g, w_ff1, w_ff2)
    for layer in range(ln_mix_g.shape[0]):
        x = _layer(x, *(p[layer] for p in params))
    return x
```

```python
import functools

import numpy as np
import jax
import jax.numpy as jnp
from jax import lax
from jax.experimental import pallas as pl
from jax.experimental.pallas import tpu as pltpu

F32 = jnp.float32
BF16 = jnp.bfloat16

HEAD_DIM = 64
NSA_KV_HEADS = 2
CMP_BLOCK = 32
CMP_STRIDE = 16
CMP_HIDDEN = 2 * HEAD_DIM
SEL_BLOCK = 64
SEL_SHIFT = 6
SEL_TOPK = 16
WINDOW = 512
N_BRANCH = 3
DECAY_LORA = 64
AAA_LORA = 64
GATE_LORA = 128
RMS_EPS = 1e-6
GN_EPS = HEAD_DIM * 1e-5
NEG_BIG = -1e30
FORCED_SCORE = 1e6

LANES = 128
SEL_LANES = 64
MASK_BIG = 32768.0
VMEM_LIMIT = 56 * 1024 * 1024

ROW_TILE = 512
Q_TILE = 256
K_TILE = 512
SCAN_CHUNK = 64


def _rms(u, g):
    return u * lax.rsqrt(jnp.mean(u * u, axis=-1, keepdims=True) + RMS_EPS) * g


def _sigmoid(u):
    return 1.0 / (1.0 + jnp.exp(-u))


def _dot(a, b):
    return jnp.dot(a, b, preferred_element_type=F32)


def _dot_nt(a, b):
    return lax.dot_general(a, b, (((1,), (1,)), ((), ())), preferred_element_type=F32)


def _const_spec(shape):
    nd = len(shape)
    return pl.BlockSpec(shape, lambda *_: (0,) * nd, pipeline_mode=pl.Buffered(1))


def _inproj_kernel(x_ref, g_ref, wn_ref, wr_ref, qg_ref, ksg_ref, kwg_ref, gb_ref,
                   q_ref, kvc_ref, ksa_ref, vs_ref, kw_ref, vw_ref, gates_ref, rw_ref):
    tm = x_ref.shape[1]
    n_q_heads = q_ref.shape[1]
    nsa_w = n_q_heads * HEAD_DIM
    kvw = NSA_KV_HEADS * HEAD_DIM
    xn = _rms(x_ref[0], g_ref[...]).astype(BF16)
    pn = _dot(xn, wn_ref[...])
    rw_ref[0] = _dot(xn, wr_ref[...])

    zeros_hd = jnp.zeros((tm, HEAD_DIM), F32)
    for h in range(n_q_heads):
        qh = _rms(pn[:, h * HEAD_DIM:(h + 1) * HEAD_DIM], qg_ref[...]) * (HEAD_DIM ** -0.5)
        q_ref[0, h] = jnp.concatenate([qh, zeros_hd], axis=-1).astype(BF16)

    kvc_ref[0] = pn[:, nsa_w:nsa_w + 2 * kvw]

    tok = pl.program_id(1) * tm + lax.broadcasted_iota(jnp.int32, (tm, 1), 0)
    blk = lax.shift_right_logical(tok, SEL_SHIFT)
    onehot = jnp.where(blk == lax.broadcasted_iota(jnp.int32, (1, SEL_LANES), 1), MASK_BIG, 0.0)
    pos = _pos_tile(tok)
    off = nsa_w + 2 * kvw
    for h in range(NSA_KV_HEADS):
        ksh = _rms(pn[:, off + h * HEAD_DIM:off + (h + 1) * HEAD_DIM], ksg_ref[...])
        ksa_ref[0, h] = jnp.concatenate([ksh, onehot, pos], axis=-1).astype(BF16)
        o2 = off + kvw
        vs_ref[0, h] = pn[:, o2 + h * HEAD_DIM:o2 + (h + 1) * HEAD_DIM].astype(BF16)
        o3 = off + 2 * kvw
        kwh = _rms(pn[:, o3 + h * HEAD_DIM:o3 + (h + 1) * HEAD_DIM], kwg_ref[...])
        kw_ref[0, h] = (jnp.concatenate([kwh, zeros_hd], axis=-1) + pos).astype(BF16)
        o4 = off + 3 * kvw
        vw_ref[0, h] = pn[:, o4 + h * HEAD_DIM:o4 + (h + 1) * HEAD_DIM].astype(BF16)
        o5 = off + 4 * kvw + h * LANES
        gates_ref[0, h] = _sigmoid(pn[:, o5:o5 + LANES] + gb_ref[:, h * LANES:(h + 1) * LANES])


def _inproj(x, ln_g, w_nsa, w_rwkv, q_g, ks_g, kw_g, gate_b2):
    B, T, D = x.shape
    nsa_cols = w_nsa.shape[1]
    rw_cols = w_rwkv.shape[1]
    n_q_heads = (nsa_cols - 2 * LANES - 6 * NSA_KV_HEADS * HEAD_DIM) // HEAD_DIM
    tm = ROW_TILE
    kvw = NSA_KV_HEADS * HEAD_DIM
    out_shape = (
        jax.ShapeDtypeStruct((B, n_q_heads, T, LANES), BF16),
        jax.ShapeDtypeStruct((B, T, 2 * kvw), F32),
        jax.ShapeDtypeStruct((B, NSA_KV_HEADS, T, 2 * LANES), BF16),
        jax.ShapeDtypeStruct((B, NSA_KV_HEADS, T, HEAD_DIM), BF16),
        jax.ShapeDtypeStruct((B, NSA_KV_HEADS, T, LANES), BF16),
        jax.ShapeDtypeStruct((B, NSA_KV_HEADS, T, HEAD_DIM), BF16),
        jax.ShapeDtypeStruct((B, NSA_KV_HEADS, T, LANES), F32),
        jax.ShapeDtypeStruct((B, T, rw_cols), F32),
    )
    head_spec = lambda n, w: pl.BlockSpec((1, n, tm, w), lambda b, i: (b, 0, i, 0))
    return pl.pallas_call(
        _inproj_kernel,
        grid=(B, T // tm),
        in_specs=[
            pl.BlockSpec((1, tm, D), lambda b, i: (b, i, 0)),
            _const_spec((1, D)),
            _const_spec((D, nsa_cols)),
            _const_spec((D, rw_cols)),
            _const_spec((1, HEAD_DIM)),
            _const_spec((1, HEAD_DIM)),
            _const_spec((1, HEAD_DIM)),
            _const_spec((1, NSA_KV_HEADS * LANES)),
        ],
        out_specs=(
            head_spec(n_q_heads, LANES),
            pl.BlockSpec((1, tm, 2 * kvw), lambda b, i: (b, i, 0)),
            head_spec(NSA_KV_HEADS, 2 * LANES),
            head_spec(NSA_KV_HEADS, HEAD_DIM),
            head_spec(NSA_KV_HEADS, LANES),
            head_spec(NSA_KV_HEADS, HEAD_DIM),
            head_spec(NSA_KV_HEADS, LANES),
            pl.BlockSpec((1, tm, rw_cols), lambda b, i: (b, i, 0)),
        ),
        out_shape=out_shape,
        compiler_params=pltpu.CompilerParams(
            dimension_semantics=("parallel", "parallel"), vmem_limit_bytes=VMEM_LIMIT),
        name="inproj",
    )(x, ln_g, w_nsa, w_rwkv, q_g, ks_g, kw_g, gate_b2)


def _gelu_tanh(u):
    return 0.5 * u * (1.0 + jnp.tanh(np.sqrt(2.0 / np.pi).astype(np.float32) * (u + 0.044715 * (u * u * u))))


def _compress_kernel(seg_ref, wa_ref, wb_ref, pa_ref, pb_ref, w2_ref, kcg_ref, kc_ref, vc_ref):
    ncp = seg_ref.shape[1]
    seg = seg_ref[0]
    first = _dot((seg + pa_ref[...]).astype(BF16), wa_ref[...])
    second = _dot((seg + pb_ref[...]).astype(BF16), wb_ref[...])
    row = lax.broadcasted_iota(jnp.int32, (ncp, 1), 0)
    real = row < ncp - 1
    second = jnp.where(real, pltpu.roll(second, ncp - 1, axis=0), 0.0)
    hid = _gelu_tanh(first + second)
    out = jnp.where(real, _dot(hid.astype(BF16), w2_ref[...]), 0.0)
    zeros_hd = jnp.zeros((ncp, HEAD_DIM), F32)
    pos = _pos_tile(row * CMP_STRIDE + (CMP_BLOCK - 1))
    for h in range(NSA_KV_HEADS):
        kch = _rms(out[:, h * HEAD_DIM:(h + 1) * HEAD_DIM], kcg_ref[...])
        kc_ref[0, h] = (jnp.concatenate([kch, zeros_hd], axis=-1) + pos).astype(BF16)
        o = (NSA_KV_HEADS + h) * HEAD_DIM
        vc_ref[0, h] = out[:, o:o + HEAD_DIM].astype(BF16)


def _compress(seg, wa, wb, pa, pb, w2, kc_g):
    B, ncp, width = seg.shape
    hid = wa.shape[1]
    return pl.pallas_call(
        _compress_kernel,
        grid=(B,),
        in_specs=[
            pl.BlockSpec((1, ncp, width), lambda b: (b, 0, 0)),
            _const_spec((width, hid)),
            _const_spec((width, hid)),
            _const_spec((1, width)),
            _const_spec((1, width)),
            _const_spec((hid, 2 * NSA_KV_HEADS * HEAD_DIM)),
            _const_spec((1, HEAD_DIM)),
        ],
        out_specs=(
            pl.BlockSpec((1, NSA_KV_HEADS, ncp, LANES), lambda b: (b, 0, 0, 0)),
            pl.BlockSpec((1, NSA_KV_HEADS, ncp, HEAD_DIM), lambda b: (b, 0, 0, 0)),
        ),
        out_shape=(
            jax.ShapeDtypeStruct((B, NSA_KV_HEADS, ncp, LANES), BF16),
            jax.ShapeDtypeStruct((B, NSA_KV_HEADS, ncp, HEAD_DIM), BF16),
        ),
        compiler_params=pltpu.CompilerParams(
            dimension_semantics=("parallel",), vmem_limit_bytes=VMEM_LIMIT),
        name="compress",
    )(seg, wa, wb, pa, pb, w2, kc_g)


def _split3_bf16(u):
    hi = u.astype(BF16)
    r1 = u - hi.astype(F32)
    mid = r1.astype(BF16)
    lo = (r1 - mid.astype(F32)).astype(BF16)
    return hi, mid, lo


def _pos_tile(pos):
    lane = lax.broadcasted_iota(jnp.int32, (1, LANES), 1)
    hi = lax.shift_left(lax.shift_right_logical(pos, SEL_SHIFT), SEL_SHIFT).astype(F32)
    lo = (pos & (SEL_BLOCK - 1)).astype(F32)
    first = HEAD_DIM
    return jnp.where((lane >= first) & (lane < first + 3), hi,
                     jnp.where((lane >= first + 3) & (lane < first + 6), lo, 0.0))


def _nsa_kernel(q_ref, kc_ref, vc_ref, ksa_ref, vs_ref, kw_ref, vw_ref, gates_ref, slopeq_ref, ovt_ref,
                o_ref, *, top_k):
    gqa, tq = q_ref.shape[1], q_ref.shape[2]
    ncp = kc_ref.shape[2]
    rows = gqa * tq
    q0 = pl.program_id(2) * tq
    q = q_ref[0].reshape(rows, LANES)
    slope_q = slopeq_ref[0]
    q_pos = q + slope_q
    row = lax.broadcasted_iota(jnp.int32, (rows, 1), 0)
    t_rows = q0 + (row & (tq - 1))

    cend = lax.broadcasted_iota(jnp.int32, (1, ncp), 1) * CMP_STRIDE + (CMP_BLOCK - 1)
    valid_c = t_rows >= cend
    s_c = jnp.where(valid_c, _dot_nt(q_pos, kc_ref[0, 0]), NEG_BIG)
    e_c = jnp.where(valid_c, jnp.exp(s_c - jnp.max(s_c, axis=-1, keepdims=True)), 0.0)
    l_c = jnp.sum(e_c, axis=-1, keepdims=True)
    p_c = e_c * (1.0 / jnp.where(l_c > 0.0, l_c, 1.0))
    o_c = _dot(p_c.astype(BF16), vc_ref[0, 0])

    wk = WINDOW + tq
    start = pl.multiple_of(jnp.maximum(q0 - WINDOW, 0), tq)
    d_w = (t_rows - start) - lax.broadcasted_iota(jnp.int32, (1, wk), 1)
    valid_w = lax.bitcast_convert_type(d_w, jnp.uint32) < jnp.uint32(WINDOW)
    s_w = jnp.where(valid_w, _dot_nt(q_pos, kw_ref[0, 0, pl.ds(start, wk), :]), NEG_BIG)
    e_w = jnp.exp(s_w - jnp.max(s_w, axis=-1, keepdims=True))
    l_w = jnp.sum(e_w, axis=-1, keepdims=True)
    o_w = _dot(e_w.astype(BF16), vw_ref[0, 0, pl.ds(start, wk), :]) * (1.0 / l_w)

    p_sum = p_c[0:tq]
    for g in range(1, gqa):
        p_sum = p_sum + p_c[g * tq:(g + 1) * tq]
    imp_t = _dot_nt(ovt_ref[...], jnp.concatenate(_split3_bf16(p_sum), axis=-1))

    jb = lax.broadcasted_iota(jnp.int32, (SEL_LANES, 1), 0)
    jb_f = jb.astype(F32)
    cur = lax.shift_right_logical(q0 + lax.broadcasted_iota(jnp.int32, (1, tq), 1), SEL_SHIFT)
    forced = (jb == 0) | (jb == cur) | (jb == cur - 1)
    work = jnp.where(forced, FORCED_SCORE, jnp.where(jb > cur, NEG_BIG, imp_t))
    sel_t = jnp.zeros((SEL_LANES, tq), F32)
    for _ in range(top_k):
        mx = jnp.max(work, axis=0, keepdims=True)
        idx = jnp.min(jnp.where(work == mx, jb_f, float(SEL_LANES)), axis=0, keepdims=True)
        pick = jb_f == idx
        sel_t = jnp.where(pick, 1.0, sel_t)
        work = jnp.where(pick, -jnp.inf, work)
    pad_t = jnp.concatenate([jnp.zeros((SEL_LANES, tq), F32), sel_t - 1.0], axis=0)
    sel_q = pad_t.T
    q_sel = (q.astype(F32) + jnp.concatenate([sel_q] * gqa, axis=0)).astype(BF16)
    q_aug = jnp.concatenate([q_sel, slope_q], axis=-1)

    tk = K_TILE
    n_full = q0 // tk

    def scores(kt):
        return _dot_nt(q_aug, ksa_ref[0, 0, pl.ds(pl.multiple_of(kt * tk, tk), tk), :])

    def update(s, kt, carry, causal):
        m_run, l_run, acc = carry
        k0 = pl.multiple_of(kt * tk, tk)
        if causal:
            s = jnp.where(lax.broadcasted_iota(jnp.int32, (1, tk), 1) <= t_rows - k0, s, NEG_BIG)
        m_new = jnp.maximum(m_run, jnp.max(s, axis=-1, keepdims=True))
        alpha = jnp.exp(m_run - m_new)
        p = jnp.exp(s - m_new)
        l_new = alpha * l_run + jnp.sum(p, axis=-1, keepdims=True)
        return m_new, l_new, alpha * acc + _dot(p.astype(BF16), vs_ref[0, 0, pl.ds(k0, tk), :])

    def pair(kt, carry, causal_second):
        s_a, s_b = scores(kt), scores(kt + 1)
        return update(s_b, kt + 1, update(s_a, kt, carry, False), causal_second)

    init = (jnp.full((rows, 1), NEG_BIG, F32), jnp.zeros((rows, 1), F32), jnp.zeros((rows, HEAD_DIM), F32))
    carry = lax.fori_loop(0, n_full // 2, lambda j, c: pair(2 * j, c, False), init)
    _, l_s, acc_s = lax.cond(
        (n_full & 1) == 1,
        lambda c: pair(n_full - 1, c, True),
        lambda c: update(scores(n_full), n_full, c, True),
        carry)
    o_s = acc_s * (1.0 / l_s)

    gt = gates_ref[0, 0]
    outs = []
    for g in range(gqa):
        sl = slice(g * tq, (g + 1) * tq)
        c = g * N_BRANCH
        outs.append(gt[:, c:c + 1] * o_c[sl] + gt[:, c + 1:c + 2] * o_s[sl] + gt[:, c + 2:c + 3] * o_w[sl])
    o_ref[0] = jnp.concatenate(outs, axis=-1).astype(o_ref.dtype)


def _nsa_attention(q, kc, vc, ksa, vs, kw, vw, gates, slope_q, ovt, top_k):
    B, n_heads, T, _ = q.shape
    gqa = n_heads // NSA_KV_HEADS
    ncp = kc.shape[2]
    tq = Q_TILE
    rows = gqa * tq
    kv_spec = lambda n, w: pl.BlockSpec((1, 1, n, w), lambda b, h, i: (b, h, 0, 0))
    return pl.pallas_call(
        functools.partial(_nsa_kernel, top_k=top_k),
        grid=(B, NSA_KV_HEADS, T // tq),
        in_specs=[
            pl.BlockSpec((1, gqa, tq, LANES), lambda b, h, i: (b, h, i, 0)),
            kv_spec(ncp, LANES),
            kv_spec(ncp, HEAD_DIM),
            kv_spec(T, 2 * LANES),
            kv_spec(T, HEAD_DIM),
            kv_spec(T, LANES),
            kv_spec(T, HEAD_DIM),
            pl.BlockSpec((1, 1, tq, LANES), lambda b, h, i: (b, h, i, 0)),
            pl.BlockSpec((1, rows, LANES), lambda b, h, i: (h, 0, 0)),
            pl.BlockSpec((SEL_LANES, 3 * ncp), lambda b, h, i: (0, 0)),
        ],
        out_specs=pl.BlockSpec((1, tq, gqa * HEAD_DIM), lambda b, h, i: (b, i, h)),
        out_shape=jax.ShapeDtypeStruct((B, T, n_heads * HEAD_DIM), BF16),
        compiler_params=pltpu.CompilerParams(
            dimension_semantics=("parallel", "parallel", "arbitrary"), vmem_limit_bytes=VMEM_LIMIT),
        name="nsa_attention",
    )(q, kc, vc, ksa, vs, kw, vw, gates, slope_q, ovt)


def _rwkv_prep_kernel(c_ref, cprev_ref, mu_ref, wwa_ref, g2_ref, w0_ref, a0_ref, kk_ref, ka_ref, rk_ref,
                      r_ref, w_ref, k_ref, v_ref, a_ref, b_ref, g_ref, bonus_ref):
    tt = c_ref.shape[1]
    width = r_ref.shape[2]
    n_heads = width // HEAD_DIM
    c = c_ref[0]
    last = jnp.where(pl.program_id(1) > 0, cprev_ref[0][7:8, :], 0.0)
    row = lax.broadcasted_iota(jnp.int32, (tt, 1), 0)
    prev = jnp.where(row == 0, last, pltpu.roll(c, 1, axis=0))
    z = c + (prev - c) * mu_ref[...]
    r = z[:, 0:width]
    k = z[:, width:2 * width]
    v = z[:, 2 * width:3 * width]
    xwa = z[:, 3 * width:3 * width + LANES]
    xg = z[:, 3 * width + LANES:3 * width + 2 * LANES]
    lane = lax.broadcasted_iota(jnp.int32, (1, LANES), 1)
    lora = _dot(jnp.where(lane < DECAY_LORA, jnp.tanh(xwa), xwa).astype(BF16), wwa_ref[...])
    y = w0_ref[...] + lora[:, 0:width]
    softplus_neg = jnp.maximum(-y, 0.0) + jnp.log(1.0 + jnp.exp(-jnp.abs(y)))
    decay = jnp.exp(-jnp.exp(-softplus_neg - 0.5))
    a = _sigmoid(a0_ref[...] + lora[:, width:2 * width])
    g_ref[0] = _dot(_sigmoid(xg).astype(BF16), g2_ref[...])
    kk = k * kk_ref[...]
    k2 = k * (1.0 + (a - 1.0) * ka_ref[...])
    rk = r * k2 * rk_ref[...]
    r_ref[0] = r
    w_ref[0] = decay
    k_ref[0] = k2
    v_ref[0] = v
    for h in range(n_heads):
        sl = slice(h * HEAD_DIM, (h + 1) * HEAD_DIM)
        kkh = kk[:, sl]
        kkh = kkh * lax.rsqrt(jnp.maximum(jnp.sum(kkh * kkh, axis=-1, keepdims=True), 1e-24))
        a_ref[0, :, sl] = -kkh
        b_ref[0, :, sl] = kkh * a[:, sl]
        bonus_ref[0, :, sl] = jnp.sum(rk[:, sl], axis=-1, keepdims=True) * v[:, sl]


def _rwkv_prep(cols, mu, wwa, g2, w0, a0, k_k, k_a, r_k):
    B, T, ncols = cols.shape
    width = w0.shape[1]
    tt = ROW_TILE
    big = pl.BlockSpec((1, tt, width), lambda b, i: (b, i, 0))
    vec = _const_spec((1, width))
    return pl.pallas_call(
        _rwkv_prep_kernel,
        grid=(B, T // tt),
        in_specs=[
            pl.BlockSpec((1, tt, ncols), lambda b, i: (b, i, 0)),
            pl.BlockSpec((1, 8, ncols), lambda b, i: (b, jnp.maximum(i * (tt // 8) - 1, 0), 0)),
            _const_spec((1, ncols)),
            _const_spec((LANES, 2 * width)),
            _const_spec((GATE_LORA, width)),
            vec, vec, vec, vec, vec,
        ],
        out_specs=(big,) * 8,
        out_shape=(jax.ShapeDtypeStruct((B, T, width), F32),) * 8,
        compiler_params=pltpu.CompilerParams(
            dimension_semantics=("parallel", "parallel"), vmem_limit_bytes=VMEM_LIMIT),
        name="rwkv_prep",
    )(cols, cols, mu, wwa, g2, w0, a0, k_k, k_a, r_k)


def _tree_sum(terms):
    while len(terms) > 1:
        nxt = [terms[i] + terms[i + 1] for i in range(0, len(terms) - 1, 2)]
        if len(terms) % 2:
            nxt.append(terms[-1])
        terms = nxt
    return terms[0]


def _sum_sublanes(u):
    u = u + pltpu.roll(u, 4, axis=0)
    u = u + pltpu.roll(u, 2, axis=0)
    return u + pltpu.roll(u, 1, axis=0)


SCAN_ROWS = 4


def _wkv_scan_kernel(r_in, w_in, k_in, v_ref, a_in, b_in, y_ref, state_ref, r_ref, w_ref, k_ref, a_ref, b_ref):
    tc, nkg = r_in.shape[0], r_in.shape[1]
    n_rows = v_ref.shape[1]

    @pl.when(pl.program_id(0) == 0)
    def _():
        state_ref[...] = jnp.zeros_like(state_ref)

    def widen(t, _):
        for src, dst in ((r_in, r_ref), (w_in, w_ref), (k_in, k_ref), (a_in, a_ref), (b_in, b_ref)):
            u = src[t]
            dst[t] = jnp.concatenate([u, u], axis=-1)
        return 0

    lax.fori_loop(0, tc, widen, 0, unroll=8)

    def row_group(gi, _):
        u0 = gi * SCAN_ROWS

        def step(t, state):
            new_state = []
            for j in range(SCAN_ROWS):
                s = state[j * nkg:(j + 1) * nkg]
                v_row = v_ref[t, pl.ds(u0 + j, 1), :]
                sa = _sum_sublanes(_tree_sum([s[g] * a_ref[t, g] for g in range(nkg)]))
                new = [s[g] * w_ref[t, g] + sa * b_ref[t, g] + v_row * k_ref[t, g] for g in range(nkg)]
                y = _sum_sublanes(_tree_sum([new[g] * r_ref[t, g] for g in range(nkg)]))
                y_ref[t, pl.ds(u0 + j, 1), :] = y[0:1]
                new_state.extend(new)
            return tuple(new_state)

        init = tuple(state_ref[u0 + j, g] for j in range(SCAN_ROWS) for g in range(nkg))
        state = lax.fori_loop(0, tc, step, init, unroll=4)
        for j in range(SCAN_ROWS):
            for g in range(nkg):
                state_ref[u0 + j, g] = state[j * nkg + g]
        return 0

    lax.fori_loop(0, n_rows // SCAN_ROWS, row_group, 0)


def _wkv_scan(r, w, k, v, a, b):
    T, nkg = r.shape[0], r.shape[1]
    n_rows = v.shape[1]
    tc = SCAN_CHUNK
    kspec = pl.BlockSpec((tc, nkg, 8, LANES // 2), lambda i: (i, 0, 0, 0))
    vspec = pl.BlockSpec((tc, n_rows, LANES), lambda i: (i, 0, 0))
    wide = pltpu.VMEM((tc, nkg, 8, LANES), F32)
    return pl.pallas_call(
        _wkv_scan_kernel,
        grid=(T // tc,),
        in_specs=[kspec, kspec, kspec, vspec, kspec, kspec],
        out_specs=vspec,
        out_shape=jax.ShapeDtypeStruct((T, n_rows, LANES), F32),
        scratch_shapes=[pltpu.VMEM((n_rows, nkg, 8, LANES), F32), wide, wide, wide, wide, wide],
        compiler_params=pltpu.CompilerParams(
            dimension_semantics=("arbitrary",), vmem_limit_bytes=VMEM_LIMIT),
        name="wkv_scan",
    )(r, w, k, v, a, b)


def _out_ffn_kernel(x_ref, ynsa_ref, ys_ref, g_ref, bonus_ref, lnw_ref, lnb_ref, wo_ref, fg_ref, w1_ref, w2_ref,
                    o_ref):
    nsa_w = ynsa_ref.shape[1]
    width = ys_ref.shape[1]
    ys = ys_ref[...]
    pieces = []
    for h in range(width // HEAD_DIM):
        yh = ys[:, h * HEAD_DIM:(h + 1) * HEAD_DIM]
        dev = yh - jnp.mean(yh, axis=-1, keepdims=True)
        pieces.append(dev * lax.rsqrt(jnp.mean(dev * dev, axis=-1, keepdims=True) + GN_EPS))
    yn = jnp.concatenate(pieces, axis=-1) * lnw_ref[...] + lnb_ref[...]
    y_rwkv = ((yn + bonus_ref[...]) * g_ref[...]).astype(BF16)
    x1 = x_ref[...] + (_dot(ynsa_ref[...], wo_ref[0:nsa_w, :]) + _dot(y_rwkv, wo_ref[nsa_w:nsa_w + width, :]))
    xn = _rms(x1, fg_ref[...]).astype(BF16)
    d_ff = w1_ref.shape[1]
    chunk = 1024
    ffn = None
    for c in range(d_ff // chunk):
        hid = _dot(xn, w1_ref[:, c * chunk:(c + 1) * chunk])
        hid = jnp.square(jnp.maximum(hid, 0.0)).astype(BF16)
        part = _dot(hid, w2_ref[c * chunk:(c + 1) * chunk, :])
        ffn = part if ffn is None else ffn + part
    o_ref[...] = x1 + ffn


def _out_ffn(x2, ynsa, ys, g, bonus, lnx_w, lnx_b, w_out, ffn_g, w1, w2):
    N, D = x2.shape
    nsa_w = ynsa.shape[1]
    width = ys.shape[1]
    d_ff = w1.shape[1]
    tm = ROW_TILE
    rowspec = lambda w: pl.BlockSpec((tm, w), lambda i: (i, 0))
    return pl.pallas_call(
        _out_ffn_kernel,
        grid=(N // tm,),
        in_specs=[
            rowspec(D), rowspec(nsa_w), rowspec(width), rowspec(width), rowspec(width),
            _const_spec((1, width)), _const_spec((1, width)),
            _const_spec((nsa_w + width, D)),
            _const_spec((1, D)),
            _const_spec((D, d_ff)),
            _const_spec((d_ff, D)),
        ],
        out_specs=rowspec(D),
        out_shape=jax.ShapeDtypeStruct((N, D), F32),
        compiler_params=pltpu.CompilerParams(
            dimension_semantics=("parallel",), vmem_limit_bytes=VMEM_LIMIT),
        name="out_ffn",
    )(x2, ynsa, ys, g, bonus, lnx_w, lnx_b, w_out, ffn_g, w1, w2)


def _alibi_slopes(n):
    start = 2.0 ** (-8.0 / n)
    return (start ** np.arange(1, n + 1)).astype(np.float32)


def _slope_lanes(n_heads):
    m = jnp.asarray(_alibi_slopes(n_heads))
    pieces = jnp.stack(_split3_bf16(m), axis=-1).astype(F32)
    lanes = jnp.zeros((n_heads, LANES), F32).at[:, HEAD_DIM:HEAD_DIM + 3].set(pieces)
    lanes = lanes.at[:, HEAD_DIM + 3:HEAD_DIM + 6].set(pieces)
    gqa = n_heads // NSA_KV_HEADS
    return jnp.repeat(lanes.reshape(NSA_KV_HEADS, gqa, LANES), Q_TILE, axis=1).astype(BF16)


def _overlap_t(ncp, n_cmp, n_sel):
    ci = np.arange(ncp)[None, :] * CMP_STRIDE
    sj = np.arange(SEL_LANES)[:, None] * SEL_BLOCK
    ov = (ci <= sj + SEL_BLOCK - 1) & (ci + CMP_BLOCK - 1 >= sj)
    ov &= (np.arange(ncp)[None, :] < n_cmp) & (np.arange(SEL_LANES)[:, None] < n_sel)
    return jnp.asarray(np.tile(ov.astype(np.float32), (1, 3)), dtype=BF16)


def _compress_weights(k_pos, k_w1, k_w2, v_pos, v_w1, v_w2):
    half = CMP_BLOCK // 2
    groups = 2 * NSA_KV_HEADS

    def first_layer(lo):
        w = jnp.zeros((half, groups, HEAD_DIM, groups, CMP_HIDDEN), F32)
        for grp in range(groups):
            w1 = k_w1 if grp < NSA_KV_HEADS else v_w1
            w = w.at[:, grp, :, grp, :].set(w1.reshape(CMP_BLOCK, HEAD_DIM, CMP_HIDDEN)[lo:lo + half])
        return w.reshape(half * groups * HEAD_DIM, groups * CMP_HIDDEN).astype(BF16)

    def pos_row(lo):
        rows = [(k_pos if grp < NSA_KV_HEADS else v_pos)[lo:lo + half] for grp in range(groups)]
        return jnp.stack(rows, axis=1).reshape(1, half * groups * HEAD_DIM)

    w2 = jnp.zeros((groups, CMP_HIDDEN, groups, HEAD_DIM), F32)
    for grp in range(groups):
        w2 = w2.at[grp, :, grp, :].set(k_w2 if grp < NSA_KV_HEADS else v_w2)
    w2 = w2.reshape(groups * CMP_HIDDEN, groups * HEAD_DIM).astype(BF16)
    return first_layer(0), first_layer(half), pos_row(0), pos_row(half), w2


def _to_scan_keys(u, B, T, n_heads):
    return u.reshape(B, T, n_heads, HEAD_DIM // 8, 8).transpose(1, 3, 4, 0, 2).reshape(T, HEAD_DIM // 8, 8, B * n_heads)


def _to_scan_values(u, B, T, n_heads):
    u = u.reshape(B, T, n_heads, 2, HEAD_DIM // 2)
    return u.transpose(1, 4, 3, 0, 2).reshape(T, HEAD_DIM // 2, 2 * B * n_heads)


def _from_scan_values(y, B, T, n_heads):
    y = y.reshape(T, HEAD_DIM // 2, 2, B, n_heads)
    return y.transpose(3, 0, 4, 2, 1).reshape(B, T, n_heads * HEAD_DIM)


def _layer(x, ln_mix_g, w_in, nsa_gate_b, q_norm_g, kc_norm_g, ks_norm_g, kw_norm_g,
           cmp_k_pos, cmp_k_w1, cmp_k_w2, cmp_v_pos, cmp_v_w1, cmp_v_w2,
           rwkv_mu, rwkv_w0, rwkv_w2, rwkv_a0, rwkv_a2, rwkv_g2, rwkv_k_k, rwkv_k_a, rwkv_r_k,
           rwkv_lnx_w, rwkv_lnx_b, w_out, ln_ffn_g, w_ff1, w_ff2):
    B, T, D = x.shape
    nsa_w = D // 2
    n_heads = nsa_w // HEAD_DIM
    gqa = n_heads // NSA_KV_HEADS
    kvw = NSA_KV_HEADS * HEAD_DIM
    rw_w = D - nsa_w
    rw_heads = rw_w // HEAD_DIM
    n_sel = T // SEL_BLOCK
    ncp = T // CMP_STRIDE
    n_cmp = (T - CMP_BLOCK) // CMP_STRIDE + 1
    top_k = min(SEL_TOPK, n_sel)
    assert T % K_TILE == 0 and T % ROW_TILE == 0 and n_sel <= SEL_LANES and n_cmp == ncp - 1
    assert B * rw_heads * 2 == LANES and gqa * N_BRANCH <= LANES and T > WINDOW
    assert DECAY_LORA + AAA_LORA == LANES and GATE_LORA == LANES

    row2 = lambda u: u.reshape(1, -1)
    nsa_main = nsa_w + 6 * kvw
    gl = w_in[:, nsa_main:nsa_main + n_heads * N_BRANCH].reshape(D, NSA_KV_HEADS, gqa * N_BRANCH)
    gl = jnp.pad(gl, ((0, 0), (0, 0), (0, LANES - gqa * N_BRANCH))).reshape(D, NSA_KV_HEADS * LANES)
    w_nsa = jnp.concatenate([w_in[:, :nsa_main], gl], axis=1).astype(BF16)
    w_rwkv = w_in[:, nsa_main + n_heads * N_BRANCH:].astype(BF16)
    gate_b2 = jnp.pad(nsa_gate_b.reshape(NSA_KV_HEADS, gqa * N_BRANCH),
                      ((0, 0), (0, LANES - gqa * N_BRANCH))).reshape(1, NSA_KV_HEADS * LANES)

    q, kvc, ksa, vs, kw, vw, gates, rw_cols = _inproj(
        x, row2(ln_mix_g), w_nsa, w_rwkv, row2(q_norm_g), row2(ks_norm_g), row2(kw_norm_g), gate_b2)

    wa, wb, pa, pb, cw2 = _compress_weights(cmp_k_pos, cmp_k_w1, cmp_k_w2, cmp_v_pos, cmp_v_w1, cmp_v_w2)
    kc, vc = _compress(kvc.reshape(B, ncp, CMP_STRIDE * 2 * kvw), wa, wb, pa, pb, cw2, row2(kc_norm_g))
    y_nsa = _nsa_attention(q, kc, vc, ksa, vs, kw, vw, gates, _slope_lanes(n_heads),
                           _overlap_t(ncp, n_cmp, n_sel), top_k)

    wwa = jnp.zeros((LANES, 2 * rw_w), F32)
    wwa = wwa.at[:DECAY_LORA, :rw_w].set(rwkv_w2).at[DECAY_LORA:, rw_w:].set(rwkv_a2).astype(BF16)
    r, w, k, v, a, b, g, bonus = _rwkv_prep(
        rw_cols, row2(rwkv_mu), wwa, rwkv_g2.astype(BF16), row2(rwkv_w0), row2(rwkv_a0),
        row2(rwkv_k_k), row2(rwkv_k_a), row2(rwkv_r_k))
    keys = [_to_scan_keys(u, B, T, rw_heads) for u in (r, w, k, a, b)]
    y_scan = _wkv_scan(keys[0], keys[1], keys[2], _to_scan_values(v, B, T, rw_heads), keys[3], keys[4])
    ys = _from_scan_values(y_scan, B, T, rw_heads)

    N = B * T
    out = _out_ffn(x.reshape(N, D), y_nsa.reshape(N, nsa_w), ys.reshape(N, rw_w), g.reshape(N, rw_w),
                   bonus.reshape(N, rw_w), row2(rwkv_lnx_w), row2(rwkv_lnx_b), w_out.astype(BF16),
                   row2(ln_ffn_g), w_ff1.astype(BF16), w_ff2.astype(BF16))
    return out.reshape(B, T, D)


def kernel(x, ln_mix_g, w_in, nsa_gate_b, q_norm_g, kc_norm_g, ks_norm_g, kw_norm_g, cmp_k_pos, cmp_k_w1, cmp_k_w2, cmp_v_pos, cmp_v_w1, cmp_v_w2, rwkv_mu, rwkv_w0, rwkv_w2, rwkv_a0, rwkv_a2, rwkv_g2, rwkv_k_k, rwkv_k_a, rwkv_r_k, rwkv_lnx_w, rwkv_lnx_b, w_out, ln_ffn_g, w_ff1, w_ff2):
    params = (ln_mix_g, w_in, nsa_gate_b, q_norm_g, kc_norm_g, ks_norm_g, kw_norm_g, cmp_k_pos, cmp_k_w1,
              cmp_k_w2, cmp_v_pos, cmp_v_w1, cmp_v_w2, rwkv_mu, rwkv_w0, rwkv_w2, rwkv_a0, rwkv_a2, rwkv_g2,
              rwkv_k_k, rwkv_k_a, rwkv_r_k, rwkv_lnx_w, rwkv_lnx_b, w_out, ln_ffn_g, w_ff1, w_ff2)
    for layer in range(ln_mix_g.shape[0]):
        x = _layer(x, *(p[layer] for p in params))
    return x
```

```python
import functools

import numpy as np
import jax
import jax.numpy as jnp
from jax import lax
from jax.experimental import pallas as pl
from jax.experimental.pallas import tpu as pltpu

F32 = jnp.float32
BF16 = jnp.bfloat16

HEAD_DIM = 64
NSA_KV_HEADS = 2
CMP_BLOCK = 32
CMP_STRIDE = 16
CMP_HIDDEN = 2 * HEAD_DIM
SEL_BLOCK = 64
SEL_SHIFT = 6
SEL_TOPK = 16
WINDOW = 512
N_BRANCH = 3
DECAY_LORA = 64
AAA_LORA = 64
GATE_LORA = 128
RMS_EPS = 1e-6
GN_EPS = HEAD_DIM * 1e-5
NEG_BIG = -1e30
FORCED_SCORE = 1e6

LANES = 128
SEL_LANES = 64
MASK_BIG = 32768.0
VMEM_LIMIT = 56 * 1024 * 1024

ROW_TILE = 512
Q_TILE = 256
K_TILE = 512
SCAN_CHUNK = 64


def _rms(u, g):
    return u * lax.rsqrt(jnp.mean(u * u, axis=-1, keepdims=True) + RMS_EPS) * g


def _sigmoid(u):
    return 1.0 / (1.0 + jnp.exp(-u))


def _dot(a, b):
    return jnp.dot(a, b, preferred_element_type=F32)


def _dot_nt(a, b):
    return lax.dot_general(a, b, (((1,), (1,)), ((), ())), preferred_element_type=F32)


def _const_spec(shape):
    nd = len(shape)
    return pl.BlockSpec(shape, lambda *_: (0,) * nd, pipeline_mode=pl.Buffered(1))


def _inproj_kernel(x_ref, g_ref, wn_ref, wr_ref, qg_ref, ksg_ref, kwg_ref, gb_ref,
                   q_ref, kvc_ref, ksa_ref, vs_ref, kw_ref, vw_ref, gates_ref, rw_ref):
    tm = x_ref.shape[1]
    n_q_heads = q_ref.shape[1]
    nsa_w = n_q_heads * HEAD_DIM
    kvw = NSA_KV_HEADS * HEAD_DIM
    xn = _rms(x_ref[0], g_ref[...]).astype(BF16)
    pn = _dot(xn, wn_ref[...])
    rw_ref[0] = _dot(xn, wr_ref[...])

    zeros_hd = jnp.zeros((tm, HEAD_DIM), F32)
    for h in range(n_q_heads):
        qh = _rms(pn[:, h * HEAD_DIM:(h + 1) * HEAD_DIM], qg_ref[...]) * (HEAD_DIM ** -0.5)
        q_ref[0, h] = jnp.concatenate([qh, zeros_hd], axis=-1).astype(BF16)

    kvc_ref[0] = pn[:, nsa_w:nsa_w + 2 * kvw]

    tok = pl.program_id(1) * tm + lax.broadcasted_iota(jnp.int32, (tm, 1), 0)
    blk = lax.shift_right_logical(tok, SEL_SHIFT)
    onehot = jnp.where(blk == lax.broadcasted_iota(jnp.int32, (1, SEL_LANES), 1), MASK_BIG, 0.0)
    pos = _pos_tile(tok)
    off = nsa_w + 2 * kvw
    for h in range(NSA_KV_HEADS):
        ksh = _rms(pn[:, off + h * HEAD_DIM:off + (h + 1) * HEAD_DIM], ksg_ref[...])
        ksa_ref[0, h] = jnp.concatenate([ksh, onehot, pos], axis=-1).astype(BF16)
        o2 = off + kvw
        vs_ref[0, h] = pn[:, o2 + h * HEAD_DIM:o2 + (h + 1) * HEAD_DIM].astype(BF16)
        o3 = off + 2 * kvw
        kwh = _rms(pn[:, o3 + h * HEAD_DIM:o3 + (h + 1) * HEAD_DIM], kwg_ref[...])
        kw_ref[0, h] = (jnp.concatenate([kwh, zeros_hd], axis=-1) + pos).astype(BF16)
        o4 = off + 3 * kvw
        vw_ref[0, h] = pn[:, o4 + h * HEAD_DIM:o4 + (h + 1) * HEAD_DIM].astype(BF16)
        o5 = off + 4 * kvw + h * LANES
        gates_ref[0, h] = _sigmoid(pn[:, o5:o5 + LANES] + gb_ref[:, h * LANES:(h + 1) * LANES])


def _inproj(x, ln_g, w_nsa, w_rwkv, q_g, ks_g, kw_g, gate_b2):
    B, T, D = x.shape
    nsa_cols = w_nsa.shape[1]
    rw_cols = w_rwkv.shape[1]
    n_q_heads = (nsa_cols - 2 * LANES - 6 * NSA_KV_HEADS * HEAD_DIM) // HEAD_DIM
    tm = ROW_TILE
    kvw = NSA_KV_HEADS * HEAD_DIM
    out_shape = (
        jax.ShapeDtypeStruct((B, n_q_heads, T, LANES), BF16),
        jax.ShapeDtypeStruct((B, T, 2 * kvw), F32),
        jax.ShapeDtypeStruct((B, NSA_KV_HEADS, T, 2 * LANES), BF16),
        jax.ShapeDtypeStruct((B, NSA_KV_HEADS, T, HEAD_DIM), BF16),
        jax.ShapeDtypeStruct((B, NSA_KV_HEADS, T, LANES), BF16),
        jax.ShapeDtypeStruct((B, NSA_KV_HEADS, T, HEAD_DIM), BF16),
        jax.ShapeDtypeStruct((B, NSA_KV_HEADS, T, LANES), F32),
        jax.ShapeDtypeStruct((B, T, rw_cols), F32),
    )
    head_spec = lambda n, w: pl.BlockSpec((1, n, tm, w), lambda b, i: (b, 0, i, 0))
    return pl.pallas_call(
        _inproj_kernel,
        grid=(B, T // tm),
        in_specs=[
            pl.BlockSpec((1, tm, D), lambda b, i: (b, i, 0)),
            _const_spec((1, D)),
            _const_spec((D, nsa_cols)),
            _const_spec((D, rw_cols)),
            _const_spec((1, HEAD_DIM)),
            _const_spec((1, HEAD_DIM)),
            _const_spec((1, HEAD_DIM)),
            _const_spec((1, NSA_KV_HEADS * LANES)),
        ],
        out_specs=(
            head_spec(n_q_heads, LANES),
            pl.BlockSpec((1, tm, 2 * kvw), lambda b, i: (b, i, 0)),
            head_spec(NSA_KV_HEADS, 2 * LANES),
            head_spec(NSA_KV_HEADS, HEAD_DIM),
            head_spec(NSA_KV_HEADS, LANES),
            head_spec(NSA_KV_HEADS, HEAD_DIM),
            head_spec(NSA_KV_HEADS, LANES),
            pl.BlockSpec((1, tm, rw_cols), lambda b, i: (b, i, 0)),
        ),
        out_shape=out_shape,
        compiler_params=pltpu.CompilerParams(
            dimension_semantics=("parallel", "parallel"), vmem_limit_bytes=VMEM_LIMIT),
        name="inproj",
    )(x, ln_g, w_nsa, w_rwkv, q_g, ks_g, kw_g, gate_b2)


def _gelu_tanh(u):
    return 0.5 * u * (1.0 + jnp.tanh(np.sqrt(2.0 / np.pi).astype(np.float32) * (u + 0.044715 * (u * u * u))))


def _compress_kernel(seg_ref, wa_ref, wb_ref, pa_ref, pb_ref, w2_ref, kcg_ref, kc_ref, vc_ref):
    ncp = seg_ref.shape[1]
    seg = seg_ref[0]
    first = _dot((seg + pa_ref[...]).astype(BF16), wa_ref[...])
    second = _dot((seg + pb_ref[...]).astype(BF16), wb_ref[...])
    row = lax.broadcasted_iota(jnp.int32, (ncp, 1), 0)
    real = row < ncp - 1
    second = jnp.where(real, pltpu.roll(second, ncp - 1, axis=0), 0.0)
    hid = _gelu_tanh(first + second)
    out = jnp.where(real, _dot(hid.astype(BF16), w2_ref[...]), 0.0)
    zeros_hd = jnp.zeros((ncp, HEAD_DIM), F32)
    pos = _pos_tile(row * CMP_STRIDE + (CMP_BLOCK - 1))
    for h in range(NSA_KV_HEADS):
        kch = _rms(out[:, h * HEAD_DIM:(h + 1) * HEAD_DIM], kcg_ref[...])
        kc_ref[0, h] = (jnp.concatenate([kch, zeros_hd], axis=-1) + pos).astype(BF16)
        o = (NSA_KV_HEADS + h) * HEAD_DIM
        vc_ref[0, h] = out[:, o:o + HEAD_DIM].astype(BF16)


def _compress(seg, wa, wb, pa, pb, w2, kc_g):
    B, ncp, width = seg.shape
    hid = wa.shape[1]
    return pl.pallas_call(
        _compress_kernel,
        grid=(B,),
        in_specs=[
            pl.BlockSpec((1, ncp, width), lambda b: (b, 0, 0)),
            _const_spec((width, hid)),
            _const_spec((width, hid)),
            _const_spec((1, width)),
            _const_spec((1, width)),
            _const_spec((hid, 2 * NSA_KV_HEADS * HEAD_DIM)),
            _const_spec((1, HEAD_DIM)),
        ],
        out_specs=(
            pl.BlockSpec((1, NSA_KV_HEADS, ncp, LANES), lambda b: (b, 0, 0, 0)),
            pl.BlockSpec((1, NSA_KV_HEADS, ncp, HEAD_DIM), lambda b: (b, 0, 0, 0)),
        ),
        out_shape=(
            jax.ShapeDtypeStruct((B, NSA_KV_HEADS, ncp, LANES), BF16),
            jax.ShapeDtypeStruct((B, NSA_KV_HEADS, ncp, HEAD_DIM), BF16),
        ),
        compiler_params=pltpu.CompilerParams(
            dimension_semantics=("parallel",), vmem_limit_bytes=VMEM_LIMIT),
        name="compress",
    )(seg, wa, wb, pa, pb, w2, kc_g)


def _split3_bf16(u):
    hi = u.astype(BF16)
    r1 = u - hi.astype(F32)
    mid = r1.astype(BF16)
    lo = (r1 - mid.astype(F32)).astype(BF16)
    return hi, mid, lo


def _pos_tile(pos):
    lane = lax.broadcasted_iota(jnp.int32, (1, LANES), 1)
    hi = lax.shift_left(lax.shift_right_logical(pos, SEL_SHIFT), SEL_SHIFT).astype(F32)
    lo = (pos & (SEL_BLOCK - 1)).astype(F32)
    first = HEAD_DIM
    return jnp.where((lane >= first) & (lane < first + 3), hi,
                     jnp.where((lane >= first + 3) & (lane < first + 6), lo, 0.0))


def _nsa_kernel(q_ref, kc_ref, vc_ref, ksa_ref, vs_ref, kw_ref, vw_ref, gates_ref, slopeq_ref, ovt_ref,
                o_ref, *, top_k):
    gqa, tq = q_ref.shape[1], q_ref.shape[2]
    ncp = kc_ref.shape[2]
    rows = gqa * tq
    q0 = pl.program_id(2) * tq
    q = q_ref[0].reshape(rows, LANES)
    slope_q = slopeq_ref[0]
    q_pos = q + slope_q
    row = lax.broadcasted_iota(jnp.int32, (rows, 1), 0)
    t_rows = q0 + (row & (tq - 1))

    cend = lax.broadcasted_iota(jnp.int32, (1, ncp), 1) * CMP_STRIDE + (CMP_BLOCK - 1)
    valid_c = t_rows >= cend
    s_c = jnp.where(valid_c, _dot_nt(q_pos, kc_ref[0, 0]), NEG_BIG)
    e_c = jnp.where(valid_c, jnp.exp(s_c - jnp.max(s_c, axis=-1, keepdims=True)), 0.0)
    l_c = jnp.sum(e_c, axis=-1, keepdims=True)
    p_c = e_c * (1.0 / jnp.where(l_c > 0.0, l_c, 1.0))
    o_c = _dot(p_c.astype(BF16), vc_ref[0, 0])

    wk = WINDOW + tq
    start = pl.multiple_of(jnp.maximum(q0 - WINDOW, 0), tq)
    d_w = (t_rows - start) - lax.broadcasted_iota(jnp.int32, (1, wk), 1)
    valid_w = lax.bitcast_convert_type(d_w, jnp.uint32) < jnp.uint32(WINDOW)
    s_w = jnp.where(valid_w, _dot_nt(q_pos, kw_ref[0, 0, pl.ds(start, wk), :]), NEG_BIG)
    e_w = jnp.exp(s_w - jnp.max(s_w, axis=-1, keepdims=True))
    l_w = jnp.sum(e_w, axis=-1, keepdims=True)
    o_w = _dot(e_w.astype(BF16), vw_ref[0, 0, pl.ds(start, wk), :]) * (1.0 / l_w)

    p_sum = p_c[0:tq]
    for g in range(1, gqa):
        p_sum = p_sum + p_c[g * tq:(g + 1) * tq]
    imp_t = _dot_nt(ovt_ref[...], jnp.concatenate(_split3_bf16(p_sum), axis=-1))

    jb = lax.broadcasted_iota(jnp.int32, (SEL_LANES, 1), 0)
    jb_f = jb.astype(F32)
    cur = lax.shift_right_logical(q0 + lax.broadcasted_iota(jnp.int32, (1, tq), 1), SEL_SHIFT)
    forced = (jb == 0) | (jb == cur) | (jb == cur - 1)
    work = jnp.where(forced, FORCED_SCORE, jnp.where(jb > cur, NEG_BIG, imp_t))
    sel_t = jnp.zeros((SEL_LANES, tq), F32)
    for _ in range(top_k):
        mx = jnp.max(work, axis=0, keepdims=True)
        idx = jnp.min(jnp.where(work == mx, jb_f, float(SEL_LANES)), axis=0, keepdims=True)
        pick = jb_f == idx
        sel_t = jnp.where(pick, 1.0, sel_t)
        work = jnp.where(pick, -jnp.inf, work)
    pad_t = jnp.concatenate([jnp.zeros((SEL_LANES, tq), F32), sel_t - 1.0], axis=0)
    sel_q = pad_t.T
    q_sel = (q.astype(F32) + jnp.concatenate([sel_q] * gqa, axis=0)).astype(BF16)
    q_aug = jnp.concatenate([q_sel, slope_q], axis=-1)

    tk = K_TILE
    n_full = q0 // tk

    def scores(kt):
        return _dot_nt(q_aug, ksa_ref[0, 0, pl.ds(pl.multiple_of(kt * tk, tk), tk), :])

    def update(s, kt, carry, causal):
        m_run, l_run, acc = carry
        k0 = pl.multiple_of(kt * tk, tk)
        if causal:
            s = jnp.where(lax.broadcasted_iota(jnp.int32, (1, tk), 1) <= t_rows - k0, s, NEG_BIG)
        m_new = jnp.maximum(m_run, jnp.max(s, axis=-1, keepdims=True))
        alpha = jnp.exp(m_run - m_new)
        p = jnp.exp(s - m_new)
        l_new = alpha * l_run + jnp.sum(p, axis=-1, keepdims=True)
        return m_new, l_new, alpha * acc + _dot(p.astype(BF16), vs_ref[0, 0, pl.ds(k0, tk), :])

    def pair(kt, carry, causal_second):
        s_a, s_b = scores(kt), scores(kt + 1)
        return update(s_b, kt + 1, update(s_a, kt, carry, False), causal_second)

    init = (jnp.full((rows, 1), NEG_BIG, F32), jnp.zeros((rows, 1), F32), jnp.zeros((rows, HEAD_DIM), F32))
    carry = lax.fori_loop(0, n_full // 2, lambda j, c: pair(2 * j, c, False), init)
    _, l_s, acc_s = lax.cond(
        (n_full & 1) == 1,
        lambda c: pair(n_full - 1, c, True),
        lambda c: update(scores(n_full), n_full, c, True),
        carry)
    o_s = acc_s * (1.0 / l_s)

    gt = gates_ref[0, 0]
    outs = []
    for g in range(gqa):
        sl = slice(g * tq, (g + 1) * tq)
        c = g * N_BRANCH
        outs.append(gt[:, c:c + 1] * o_c[sl] + gt[:, c + 1:c + 2] * o_s[sl] + gt[:, c + 2:c + 3] * o_w[sl])
    o_ref[0] = jnp.concatenate(outs, axis=-1).astype(o_ref.dtype)


def _nsa_attention(q, kc, vc, ksa, vs, kw, vw, gates, slope_q, ovt, top_k):
    B, n_heads, T, _ = q.shape
    gqa = n_heads // NSA_KV_HEADS
    ncp = kc.shape[2]
    tq = Q_TILE
    rows = gqa * tq
    kv_spec = lambda n, w: pl.BlockSpec((1, 1, n, w), lambda b, h, i: (b, h, 0, 0))
    return pl.pallas_call(
        functools.partial(_nsa_kernel, top_k=top_k),
        grid=(B, NSA_KV_HEADS, T // tq),
        in_specs=[
            pl.BlockSpec((1, gqa, tq, LANES), lambda b, h, i: (b, h, i, 0)),
            kv_spec(ncp, LANES),
            kv_spec(ncp, HEAD_DIM),
            kv_spec(T, 2 * LANES),
            kv_spec(T, HEAD_DIM),
            kv_spec(T, LANES),
            kv_spec(T, HEAD_DIM),
            pl.BlockSpec((1, 1, tq, LANES), lambda b, h, i: (b, h, i, 0)),
            pl.BlockSpec((1, rows, LANES), lambda b, h, i: (h, 0, 0)),
            pl.BlockSpec((SEL_LANES, 3 * ncp), lambda b, h, i: (0, 0)),
        ],
        out_specs=pl.BlockSpec((1, tq, gqa * HEAD_DIM), lambda b, h, i: (b, i, h)),
        out_shape=jax.ShapeDtypeStruct((B, T, n_heads * HEAD_DIM), BF16),
        compiler_params=pltpu.CompilerParams(
            dimension_semantics=("parallel", "parallel", "arbitrary"), vmem_limit_bytes=VMEM_LIMIT),
        name="nsa_attention",
    )(q, kc, vc, ksa, vs, kw, vw, gates, slope_q, ovt)


def _rwkv_prep_kernel(c_ref, cprev_ref, mu_ref, wwa_ref, g2_ref, w0_ref, a0_ref, kk_ref, ka_ref, rk_ref,
                      r_ref, w_ref, k_ref, v_ref, a_ref, b_ref, g_ref, bonus_ref):
    tt = c_ref.shape[1]
    width = r_ref.shape[2]
    n_heads = width // HEAD_DIM
    c = c_ref[0]
    last = jnp.where(pl.program_id(1) > 0, cprev_ref[0][7:8, :], 0.0)
    row = lax.broadcasted_iota(jnp.int32, (tt, 1), 0)
    prev = jnp.where(row == 0, last, pltpu.roll(c, 1, axis=0))
    z = c + (prev - c) * mu_ref[...]
    r = z[:, 0:width]
    k = z[:, width:2 * width]
    v = z[:, 2 * width:3 * width]
    xwa = z[:, 3 * width:3 * width + LANES]
    xg = z[:, 3 * width + LANES:3 * width + 2 * LANES]
    lane = lax.broadcasted_iota(jnp.int32, (1, LANES), 1)
    lora = _dot(jnp.where(lane < DECAY_LORA, jnp.tanh(xwa), xwa).astype(BF16), wwa_ref[...])
    y = w0_ref[...] + lora[:, 0:width]
    softplus_neg = jnp.maximum(-y, 0.0) + jnp.log(1.0 + jnp.exp(-jnp.abs(y)))
    decay = jnp.exp(-jnp.exp(-softplus_neg - 0.5))
    a = _sigmoid(a0_ref[...] + lora[:, width:2 * width])
    g_ref[0] = _dot(_sigmoid(xg).astype(BF16), g2_ref[...])
    kk = k * kk_ref[...]
    k2 = k * (1.0 + (a - 1.0) * ka_ref[...])
    rk = r * k2 * rk_ref[...]
    r_ref[0] = r
    w_ref[0] = decay
    k_ref[0] = k2
    v_ref[0] = v
    for h in range(n_heads):
        sl = slice(h * HEAD_DIM, (h + 1) * HEAD_DIM)
        kkh = kk[:, sl]
        kkh = kkh * lax.rsqrt(jnp.maximum(jnp.sum(kkh * kkh, axis=-1, keepdims=True), 1e-24))
        a_ref[0, :, sl] = -kkh
        b_ref[0, :, sl] = kkh * a[:, sl]
        bonus_ref[0, :, sl] = jnp.sum(rk[:, sl], axis=-1, keepdims=True) * v[:, sl]


def _rwkv_prep(cols, mu, wwa, g2, w0, a0, k_k, k_a, r_k):
    B, T, ncols = cols.shape
    width = w0.shape[1]
    tt = ROW_TILE
    big = pl.BlockSpec((1, tt, width), lambda b, i: (b, i, 0))
    vec = _const_spec((1, width))
    return pl.pallas_call(
        _rwkv_prep_kernel,
        grid=(B, T // tt),
        in_specs=[
            pl.BlockSpec((1, tt, ncols), lambda b, i: (b, i, 0)),
            pl.BlockSpec((1, 8, ncols), lambda b, i: (b, jnp.maximum(i * (tt // 8) - 1, 0), 0)),
            _const_spec((1, ncols)),
            _const_spec((LANES, 2 * width)),
            _const_spec((GATE_LORA, width)),
            vec, vec, vec, vec, vec,
        ],
        out_specs=(big,) * 8,
        out_shape=(jax.ShapeDtypeStruct((B, T, width), F32),) * 8,
        compiler_params=pltpu.CompilerParams(
            dimension_semantics=("parallel", "parallel"), vmem_limit_bytes=VMEM_LIMIT),
        name="rwkv_prep",
    )(cols, cols, mu, wwa, g2, w0, a0, k_k, k_a, r_k)


def _tree_sum(terms):
    while len(terms) > 1:
        nxt = [terms[i] + terms[i + 1] for i in range(0, len(terms) - 1, 2)]
        if len(terms) % 2:
            nxt.append(terms[-1])
        terms = nxt
    return terms[0]


def _sum_sublanes(u):
    u = u + pltpu.roll(u, 4, axis=0)
    u = u + pltpu.roll(u, 2, axis=0)
    return u + pltpu.roll(u, 1, axis=0)


SCAN_ROWS = 4


def _heads_to_lanes(xa, xb):
    low = lax.broadcasted_iota(jnp.int32, (1, LANES), 1) < HEAD_DIM
    rows = []
    for j in range(xa.shape[1] // LANES):
        va, vb = xa[:, j * LANES:(j + 1) * LANES], xb[:, j * LANES:(j + 1) * LANES]
        rows.append(jnp.where(low, va, pltpu.roll(vb, HEAD_DIM, axis=1)))
        rows.append(jnp.where(low, pltpu.roll(va, HEAD_DIM, axis=1), vb))
    xt = jnp.concatenate(rows + rows, axis=0).T
    return xt[0:HEAD_DIM], xt[HEAD_DIM:2 * HEAD_DIM]


def _wkv_scan_kernel(r_in, w_in, k_in, v_in, a_in, b_in, y_out, state_ref,
                     r_ref, w_ref, k_ref, a_ref, b_ref, v_ref, y_ref):
    tc = r_in.shape[1]
    nkg = HEAD_DIM // 8
    n_rows = HEAD_DIM // 2
    low = lax.broadcasted_iota(jnp.int32, (1, LANES), 1) < HEAD_DIM

    @pl.when(pl.program_id(0) == 0)
    def _():
        state_ref[...] = jnp.zeros_like(state_ref)

    def retile(t, _):
        r_t, w_t = _heads_to_lanes(r_in[:, t, :], w_in[:, t, :])
        k_t, a_t = _heads_to_lanes(k_in[:, t, :], a_in[:, t, :])
        b_t, v_t = _heads_to_lanes(b_in[:, t, :], v_in[:, t, :])
        for dst, val in ((r_ref, r_t), (w_ref, w_t), (k_ref, k_t), (a_ref, a_t), (b_ref, b_t)):
            dst[t] = val.reshape(nkg, 8, LANES)
        v_ref[t] = jnp.where(low, v_t[0:n_rows], v_t[n_rows:2 * n_rows])
        return 0

    lax.fori_loop(0, tc, retile, 0, unroll=4)

    def row_group(gi, _):
        u0 = gi * SCAN_ROWS

        def step(t, state):
            new_state = []
            for j in range(SCAN_ROWS):
                s = state[j * nkg:(j + 1) * nkg]
                v_row = v_ref[t, pl.ds(u0 + j, 1), :]
                sa = _sum_sublanes(_tree_sum([s[g] * a_ref[t, g] for g in range(nkg)]))
                new = [s[g] * w_ref[t, g] + sa * b_ref[t, g] + v_row * k_ref[t, g] for g in range(nkg)]
                y = _sum_sublanes(_tree_sum([new[g] * r_ref[t, g] for g in range(nkg)]))
                y_ref[t, pl.ds(u0 + j, 1), :] = y[0:1]
                new_state.extend(new)
            return tuple(new_state)

        init = tuple(state_ref[u0 + j, g] for j in range(SCAN_ROWS) for g in range(nkg))
        state = lax.fori_loop(0, tc, step, init, unroll=4)
        for j in range(SCAN_ROWS):
            for g in range(nkg):
                state_ref[u0 + j, g] = state[j * nkg + g]
        return 0

    lax.fori_loop(0, n_rows // SCAN_ROWS, row_group, 0)

    def untile(t, _):
        y_t = y_ref[t]
        m = jnp.concatenate([y_t, pltpu.roll(y_t, HEAD_DIM, axis=1)], axis=0)
        mt = jnp.concatenate([m, m], axis=0).T
        tiles = []
        for j in range(y_out.shape[2] // LANES):
            even, odd = mt[16 * j:16 * j + 8], mt[16 * j + 8:16 * j + 16]
            tiles.append(jnp.where(low, even, pltpu.roll(odd, HEAD_DIM, axis=1)))
        y_out[:, t, :] = jnp.concatenate(tiles, axis=-1)
        return 0

    lax.fori_loop(0, tc, untile, 0, unroll=8)


def _wkv_scan(r, w, k, v, a, b):
    B, T, width = r.shape
    tc = SCAN_CHUNK
    spec = pl.BlockSpec((B, tc, width), lambda i: (0, i, 0))
    wide = pltpu.VMEM((tc, HEAD_DIM // 8, 8, LANES), F32)
    rows = pltpu.VMEM((tc, HEAD_DIM // 2, LANES), F32)
    return pl.pallas_call(
        _wkv_scan_kernel,
        grid=(T // tc,),
        in_specs=[spec] * 6,
        out_specs=spec,
        out_shape=jax.ShapeDtypeStruct((B, T, width), F32),
        scratch_shapes=[pltpu.VMEM((HEAD_DIM // 2, HEAD_DIM // 8, 8, LANES), F32),
                        wide, wide, wide, wide, wide, rows, rows],
        compiler_params=pltpu.CompilerParams(
            dimension_semantics=("arbitrary",), vmem_limit_bytes=VMEM_LIMIT),
        name="wkv_scan",
    )(r, w, k, v, a, b)


def _out_ffn_kernel(x_ref, ynsa_ref, ys_ref, g_ref, bonus_ref, lnw_ref, lnb_ref, wo_ref, fg_ref, w1_ref, w2_ref,
                    o_ref):
    nsa_w = ynsa_ref.shape[1]
    width = ys_ref.shape[1]
    ys = ys_ref[...]
    pieces = []
    for h in range(width // HEAD_DIM):
        yh = ys[:, h * HEAD_DIM:(h + 1) * HEAD_DIM]
        dev = yh - jnp.mean(yh, axis=-1, keepdims=True)
        pieces.append(dev * lax.rsqrt(jnp.mean(dev * dev, axis=-1, keepdims=True) + GN_EPS))
    yn = jnp.concatenate(pieces, axis=-1) * lnw_ref[...] + lnb_ref[...]
    y_rwkv = ((yn + bonus_ref[...]) * g_ref[...]).astype(BF16)
    x1 = x_ref[...] + (_dot(ynsa_ref[...], wo_ref[0:nsa_w, :]) + _dot(y_rwkv, wo_ref[nsa_w:nsa_w + width, :]))
    xn = _rms(x1, fg_ref[...]).astype(BF16)
    d_ff = w1_ref.shape[1]
    chunk = 1024
    ffn = None
    for c in range(d_ff // chunk):
        hid = _dot(xn, w1_ref[:, c * chunk:(c + 1) * chunk])
        hid = jnp.square(jnp.maximum(hid, 0.0)).astype(BF16)
        part = _dot(hid, w2_ref[c * chunk:(c + 1) * chunk, :])
        ffn = part if ffn is None else ffn + part
    o_ref[...] = x1 + ffn


def _out_ffn(x2, ynsa, ys, g, bonus, lnx_w, lnx_b, w_out, ffn_g, w1, w2):
    N, D = x2.shape
    nsa_w = ynsa.shape[1]
    width = ys.shape[1]
    d_ff = w1.shape[1]
    tm = ROW_TILE
    rowspec = lambda w: pl.BlockSpec((tm, w), lambda i: (i, 0))
    return pl.pallas_call(
        _out_ffn_kernel,
        grid=(N // tm,),
        in_specs=[
            rowspec(D), rowspec(nsa_w), rowspec(width), rowspec(width), rowspec(width),
            _const_spec((1, width)), _const_spec((1, width)),
            _const_spec((nsa_w + width, D)),
            _const_spec((1, D)),
            _const_spec((D, d_ff)),
            _const_spec((d_ff, D)),
        ],
        out_specs=rowspec(D),
        out_shape=jax.ShapeDtypeStruct((N, D), F32),
        compiler_params=pltpu.CompilerParams(
            dimension_semantics=("parallel",), vmem_limit_bytes=VMEM_LIMIT),
        name="out_ffn",
    )(x2, ynsa, ys, g, bonus, lnx_w, lnx_b, w_out, ffn_g, w1, w2)


def _alibi_slopes(n):
    start = 2.0 ** (-8.0 / n)
    return (start ** np.arange(1, n + 1)).astype(np.float32)


def _slope_lanes(n_heads):
    m = jnp.asarray(_alibi_slopes(n_heads))
    pieces = jnp.stack(_split3_bf16(m), axis=-1).astype(F32)
    lanes = jnp.zeros((n_heads, LANES), F32).at[:, HEAD_DIM:HEAD_DIM + 3].set(pieces)
    lanes = lanes.at[:, HEAD_DIM + 3:HEAD_DIM + 6].set(pieces)
    gqa = n_heads // NSA_KV_HEADS
    return jnp.repeat(lanes.reshape(NSA_KV_HEADS, gqa, LANES), Q_TILE, axis=1).astype(BF16)


def _overlap_t(ncp, n_cmp, n_sel):
    ci = np.arange(ncp)[None, :] * CMP_STRIDE
    sj = np.arange(SEL_LANES)[:, None] * SEL_BLOCK
    ov = (ci <= sj + SEL_BLOCK - 1) & (ci + CMP_BLOCK - 1 >= sj)
    ov &= (np.arange(ncp)[None, :] < n_cmp) & (np.arange(SEL_LANES)[:, None] < n_sel)
    return jnp.asarray(np.tile(ov.astype(np.float32), (1, 3)), dtype=BF16)


def _compress_weights(k_pos, k_w1, k_w2, v_pos, v_w1, v_w2):
    half = CMP_BLOCK // 2
    groups = 2 * NSA_KV_HEADS

    def first_layer(lo):
        w = jnp.zeros((half, groups, HEAD_DIM, groups, CMP_HIDDEN), F32)
        for grp in range(groups):
            w1 = k_w1 if grp < NSA_KV_HEADS else v_w1
            w = w.at[:, grp, :, grp, :].set(w1.reshape(CMP_BLOCK, HEAD_DIM, CMP_HIDDEN)[lo:lo + half])
        return w.reshape(half * groups * HEAD_DIM, groups * CMP_HIDDEN).astype(BF16)

    def pos_row(lo):
        rows = [(k_pos if grp < NSA_KV_HEADS else v_pos)[lo:lo + half] for grp in range(groups)]
        return jnp.stack(rows, axis=1).reshape(1, half * groups * HEAD_DIM)

    w2 = jnp.zeros((groups, CMP_HIDDEN, groups, HEAD_DIM), F32)
    for grp in range(groups):
        w2 = w2.at[grp, :, grp, :].set(k_w2 if grp < NSA_KV_HEADS else v_w2)
    w2 = w2.reshape(groups * CMP_HIDDEN, groups * HEAD_DIM).astype(BF16)
    return first_layer(0), first_layer(half), pos_row(0), pos_row(half), w2


def _layer(x, ln_mix_g, w_in, nsa_gate_b, q_norm_g, kc_norm_g, ks_norm_g, kw_norm_g,
           cmp_k_pos, cmp_k_w1, cmp_k_w2, cmp_v_pos, cmp_v_w1, cmp_v_w2,
           rwkv_mu, rwkv_w0, rwkv_w2, rwkv_a0, rwkv_a2, rwkv_g2, rwkv_k_k, rwkv_k_a, rwkv_r_k,
           rwkv_lnx_w, rwkv_lnx_b, w_out, ln_ffn_g, w_ff1, w_ff2):
    B, T, D = x.shape
    nsa_w = D // 2
    n_heads = nsa_w // HEAD_DIM
    gqa = n_heads // NSA_KV_HEADS
    kvw = NSA_KV_HEADS * HEAD_DIM
    rw_w = D - nsa_w
    rw_heads = rw_w // HEAD_DIM
    n_sel = T // SEL_BLOCK
    ncp = T // CMP_STRIDE
    n_cmp = (T - CMP_BLOCK) // CMP_STRIDE + 1
    top_k = min(SEL_TOPK, n_sel)
    assert T % K_TILE == 0 and T % ROW_TILE == 0 and n_sel <= SEL_LANES and n_cmp == ncp - 1
    assert B * rw_heads * 2 == LANES and gqa * N_BRANCH <= LANES and T > WINDOW
    assert DECAY_LORA + AAA_LORA == LANES and GATE_LORA == LANES

    row2 = lambda u: u.reshape(1, -1)
    nsa_main = nsa_w + 6 * kvw
    gl = w_in[:, nsa_main:nsa_main + n_heads * N_BRANCH].reshape(D, NSA_KV_HEADS, gqa * N_BRANCH)
    gl = jnp.pad(gl, ((0, 0), (0, 0), (0, LANES - gqa * N_BRANCH))).reshape(D, NSA_KV_HEADS * LANES)
    w_nsa = jnp.concatenate([w_in[:, :nsa_main], gl], axis=1).astype(BF16)
    w_rwkv = w_in[:, nsa_main + n_heads * N_BRANCH:].astype(BF16)
    gate_b2 = jnp.pad(nsa_gate_b.reshape(NSA_KV_HEADS, gqa * N_BRANCH),
                      ((0, 0), (0, LANES - gqa * N_BRANCH))).reshape(1, NSA_KV_HEADS * LANES)

    q, kvc, ksa, vs, kw, vw, gates, rw_cols = _inproj(
        x, row2(ln_mix_g), w_nsa, w_rwkv, row2(q_norm_g), row2(ks_norm_g), row2(kw_norm_g), gate_b2)

    wa, wb, pa, pb, cw2 = _compress_weights(cmp_k_pos, cmp_k_w1, cmp_k_w2, cmp_v_pos, cmp_v_w1, cmp_v_w2)
    kc, vc = _compress(kvc.reshape(B, ncp, CMP_STRIDE * 2 * kvw), wa, wb, pa, pb, cw2, row2(kc_norm_g))
    y_nsa = _nsa_attention(q, kc, vc, ksa, vs, kw, vw, gates, _slope_lanes(n_heads),
                           _overlap_t(ncp, n_cmp, n_sel), top_k)

    wwa = jnp.zeros((LANES, 2 * rw_w), F32)
    wwa = wwa.at[:DECAY_LORA, :rw_w].set(rwkv_w2).at[DECAY_LORA:, rw_w:].set(rwkv_a2).astype(BF16)
    r, w, k, v, a, b, g, bonus = _rwkv_prep(
        rw_cols, row2(rwkv_mu), wwa, rwkv_g2.astype(BF16), row2(rwkv_w0), row2(rwkv_a0),
        row2(rwkv_k_k), row2(rwkv_k_a), row2(rwkv_r_k))
    ys = _wkv_scan(r, w, k, v, a, b)

    N = B * T
    out = _out_ffn(x.reshape(N, D), y_nsa.reshape(N, nsa_w), ys.reshape(N, rw_w), g.reshape(N, rw_w),
                   bonus.reshape(N, rw_w), row2(rwkv_lnx_w), row2(rwkv_lnx_b), w_out.astype(BF16),
                   row2(ln_ffn_g), w_ff1.astype(BF16), w_ff2.astype(BF16))
    return out.reshape(B, T, D)


def kernel(x, ln_mix_g, w_in, nsa_gate_b, q_norm_g, kc_norm_g, ks_norm_g, kw_norm_g, cmp_k_pos, cmp_k_w1, cmp_k_w2, cmp_v_pos, cmp_v_w1, cmp_v_w2, rwkv_mu, rwkv_w0, rwkv_w2, rwkv_a0, rwkv_a2, rwkv_g2, rwkv_k_k, rwkv_k_a, rwkv_r_k, rwkv_lnx_w, rwkv_lnx_b, w_out, ln_ffn_g, w_ff1, w_ff2):
    params = (ln_mix_g, w_in, nsa_gate_b, q_norm_g, kc_norm_g, ks_norm_g, kw_norm_g, cmp_k_pos, cmp_k_w1,
              cmp_k_w2, cmp_v_pos, cmp_v_w1, cmp_v_w2, rwkv_mu, rwkv_w0, rwkv_w2, rwkv_a0, rwkv_a2, rwkv_g2,
              rwkv_k_k, rwkv_k_a, rwkv_r_k, rwkv_lnx_w, rwkv_lnx_b, w_out, ln_ffn_g, w_ff1, w_ff2)
    for layer in range(ln_mix_g.shape[0]):
        x = _layer(x, *(p[layer] for p in params))
    return x
```

```python
import functools

import numpy as np
import jax
import jax.numpy as jnp
from jax import lax
from jax.experimental import pallas as pl
from jax.experimental.pallas import tpu as pltpu

F32 = jnp.float32
BF16 = jnp.bfloat16

HEAD_DIM = 64
NSA_KV_HEADS = 2
CMP_BLOCK = 32
CMP_STRIDE = 16
CMP_HIDDEN = 2 * HEAD_DIM
SEL_BLOCK = 64
SEL_SHIFT = 6
SEL_TOPK = 16
WINDOW = 512
N_BRANCH = 3
DECAY_LORA = 64
AAA_LORA = 64
GATE_LORA = 128
RMS_EPS = 1e-6
GN_EPS = HEAD_DIM * 1e-5
NEG_BIG = -1e30
FORCED_SCORE = 1e6
LOG2E = 1.4426950408889634

LANES = 128
SEL_LANES = 64
MASK_BIG = 32768.0
VMEM_LIMIT = 56 * 1024 * 1024

ROW_TILE = 512
Q_TILE = 256
K_TILE = 512
SCAN_CHUNK = 64


def _rms(u, g):
    return u * lax.rsqrt(jnp.mean(u * u, axis=-1, keepdims=True) + RMS_EPS) * g


def _sigmoid(u):
    return 1.0 / (1.0 + jnp.exp(-u))


def _dot(a, b):
    return jnp.dot(a, b, preferred_element_type=F32)


def _dot_nt(a, b):
    return lax.dot_general(a, b, (((1,), (1,)), ((), ())), preferred_element_type=F32)


def _const_spec(shape):
    nd = len(shape)
    return pl.BlockSpec(shape, lambda *_: (0,) * nd, pipeline_mode=pl.Buffered(1))


def _inproj_kernel(x_ref, g_ref, wn_ref, wr_ref, qg_ref, ksg_ref, kwg_ref, gb_ref,
                   q_ref, kvc_ref, ksa_ref, vs_ref, kw_ref, vw_ref, gates_ref, rw_ref):
    tm = x_ref.shape[1]
    n_q_heads = q_ref.shape[1]
    nsa_w = n_q_heads * HEAD_DIM
    kvw = NSA_KV_HEADS * HEAD_DIM
    xn = _rms(x_ref[0], g_ref[...]).astype(BF16)
    pn = _dot(xn, wn_ref[...])
    rw_ref[0] = _dot(xn, wr_ref[...])

    zeros_hd = jnp.zeros((tm, HEAD_DIM), F32)
    for h in range(n_q_heads):
        qh = _rms(pn[:, h * HEAD_DIM:(h + 1) * HEAD_DIM], qg_ref[...]) * (HEAD_DIM ** -0.5 * LOG2E)
        q_ref[0, h] = jnp.concatenate([qh, zeros_hd], axis=-1).astype(BF16)

    kvc_ref[0] = pn[:, nsa_w:nsa_w + 2 * kvw]

    tok = pl.program_id(1) * tm + lax.broadcasted_iota(jnp.int32, (tm, 1), 0)
    blk = lax.shift_right_logical(tok, SEL_SHIFT)
    onehot = jnp.where(blk == lax.broadcasted_iota(jnp.int32, (1, SEL_LANES), 1), MASK_BIG, 0.0)
    pos = _pos_tile(tok)
    off = nsa_w + 2 * kvw
    for h in range(NSA_KV_HEADS):
        ksh = _rms(pn[:, off + h * HEAD_DIM:off + (h + 1) * HEAD_DIM], ksg_ref[...])
        ksa_ref[0, h] = jnp.concatenate([ksh, onehot, pos], axis=-1).astype(BF16)
        o2 = off + kvw
        vs_ref[0, h] = pn[:, o2 + h * HEAD_DIM:o2 + (h + 1) * HEAD_DIM].astype(BF16)
        o3 = off + 2 * kvw
        kwh = _rms(pn[:, o3 + h * HEAD_DIM:o3 + (h + 1) * HEAD_DIM], kwg_ref[...])
        kw_ref[0, h] = (jnp.concatenate([kwh, zeros_hd], axis=-1) + pos).astype(BF16)
        o4 = off + 3 * kvw
        vw_ref[0, h] = pn[:, o4 + h * HEAD_DIM:o4 + (h + 1) * HEAD_DIM].astype(BF16)
        o5 = off + 4 * kvw + h * LANES
        gates_ref[0, h] = _sigmoid(pn[:, o5:o5 + LANES] + gb_ref[:, h * LANES:(h + 1) * LANES])


def _inproj(x, ln_g, w_nsa, w_rwkv, q_g, ks_g, kw_g, gate_b2):
    B, T, D = x.shape
    nsa_cols = w_nsa.shape[1]
    rw_cols = w_rwkv.shape[1]
    n_q_heads = (nsa_cols - 2 * LANES - 6 * NSA_KV_HEADS * HEAD_DIM) // HEAD_DIM
    tm = ROW_TILE
    kvw = NSA_KV_HEADS * HEAD_DIM
    out_shape = (
        jax.ShapeDtypeStruct((B, n_q_heads, T, LANES), BF16),
        jax.ShapeDtypeStruct((B, T, 2 * kvw), F32),
        jax.ShapeDtypeStruct((B, NSA_KV_HEADS, T, 2 * LANES), BF16),
        jax.ShapeDtypeStruct((B, NSA_KV_HEADS, T, HEAD_DIM), BF16),
        jax.ShapeDtypeStruct((B, NSA_KV_HEADS, T, LANES), BF16),
        jax.ShapeDtypeStruct((B, NSA_KV_HEADS, T, HEAD_DIM), BF16),
        jax.ShapeDtypeStruct((B, NSA_KV_HEADS, T, LANES), F32),
        jax.ShapeDtypeStruct((B, T, rw_cols), F32),
    )
    head_spec = lambda n, w: pl.BlockSpec((1, n, tm, w), lambda b, i: (b, 0, i, 0))
    return pl.pallas_call(
        _inproj_kernel,
        grid=(B, T // tm),
        in_specs=[
            pl.BlockSpec((1, tm, D), lambda b, i: (b, i, 0)),
            _const_spec((1, D)),
            _const_spec((D, nsa_cols)),
            _const_spec((D, rw_cols)),
            _const_spec((1, HEAD_DIM)),
            _const_spec((1, HEAD_DIM)),
            _const_spec((1, HEAD_DIM)),
            _const_spec((1, NSA_KV_HEADS * LANES)),
        ],
        out_specs=(
            head_spec(n_q_heads, LANES),
            pl.BlockSpec((1, tm, 2 * kvw), lambda b, i: (b, i, 0)),
            head_spec(NSA_KV_HEADS, 2 * LANES),
            head_spec(NSA_KV_HEADS, HEAD_DIM),
            head_spec(NSA_KV_HEADS, LANES),
            head_spec(NSA_KV_HEADS, HEAD_DIM),
            head_spec(NSA_KV_HEADS, LANES),
            pl.BlockSpec((1, tm, rw_cols), lambda b, i: (b, i, 0)),
        ),
        out_shape=out_shape,
        compiler_params=pltpu.CompilerParams(
            dimension_semantics=("parallel", "parallel"), vmem_limit_bytes=VMEM_LIMIT),
        name="inproj",
    )(x, ln_g, w_nsa, w_rwkv, q_g, ks_g, kw_g, gate_b2)


def _gelu_tanh(u):
    return 0.5 * u * (1.0 + jnp.tanh(np.sqrt(2.0 / np.pi).astype(np.float32) * (u + 0.044715 * (u * u * u))))


def _compress_kernel(seg_ref, wa_ref, wb_ref, pa_ref, pb_ref, w2_ref, kcg_ref, kc_ref, vc_ref):
    ncp = seg_ref.shape[1]
    seg = seg_ref[0]
    first = _dot((seg + pa_ref[...]).astype(BF16), wa_ref[...])
    second = _dot((seg + pb_ref[...]).astype(BF16), wb_ref[...])
    row = lax.broadcasted_iota(jnp.int32, (ncp, 1), 0)
    real = row < ncp - 1
    second = jnp.where(real, pltpu.roll(second, ncp - 1, axis=0), 0.0)
    hid = _gelu_tanh(first + second)
    out = jnp.where(real, _dot(hid.astype(BF16), w2_ref[...]), 0.0)
    zeros_hd = jnp.zeros((ncp, HEAD_DIM), F32)
    pos = _pos_tile(row * CMP_STRIDE + (CMP_BLOCK - 1))
    for h in range(NSA_KV_HEADS):
        kch = _rms(out[:, h * HEAD_DIM:(h + 1) * HEAD_DIM], kcg_ref[...])
        kc_ref[0, h] = (jnp.concatenate([kch, zeros_hd], axis=-1) + pos).astype(BF16)
        o = (NSA_KV_HEADS + h) * HEAD_DIM
        vc_ref[0, h] = out[:, o:o + HEAD_DIM].astype(BF16)


def _compress(seg, wa, wb, pa, pb, w2, kc_g):
    B, ncp, width = seg.shape
    hid = wa.shape[1]
    return pl.pallas_call(
        _compress_kernel,
        grid=(B,),
        in_specs=[
            pl.BlockSpec((1, ncp, width), lambda b: (b, 0, 0)),
            _const_spec((width, hid)),
            _const_spec((width, hid)),
            _const_spec((1, width)),
            _const_spec((1, width)),
            _const_spec((hid, 2 * NSA_KV_HEADS * HEAD_DIM)),
            _const_spec((1, HEAD_DIM)),
        ],
        out_specs=(
            pl.BlockSpec((1, NSA_KV_HEADS, ncp, LANES), lambda b: (b, 0, 0, 0)),
            pl.BlockSpec((1, NSA_KV_HEADS, ncp, HEAD_DIM), lambda b: (b, 0, 0, 0)),
        ),
        out_shape=(
            jax.ShapeDtypeStruct((B, NSA_KV_HEADS, ncp, LANES), BF16),
            jax.ShapeDtypeStruct((B, NSA_KV_HEADS, ncp, HEAD_DIM), BF16),
        ),
        compiler_params=pltpu.CompilerParams(
            dimension_semantics=("parallel",), vmem_limit_bytes=VMEM_LIMIT),
        name="compress",
    )(seg, wa, wb, pa, pb, w2, kc_g)


def _split3_bf16(u):
    hi = u.astype(BF16)
    r1 = u - hi.astype(F32)
    mid = r1.astype(BF16)
    lo = (r1 - mid.astype(F32)).astype(BF16)
    return hi, mid, lo


def _pos_tile(pos):
    lane = lax.broadcasted_iota(jnp.int32, (1, LANES), 1)
    hi = lax.shift_left(lax.shift_right_logical(pos, SEL_SHIFT), SEL_SHIFT).astype(F32)
    lo = (pos & (SEL_BLOCK - 1)).astype(F32)
    first = HEAD_DIM
    return jnp.where((lane >= first) & (lane < first + 3), hi,
                     jnp.where((lane >= first + 3) & (lane < first + 6), lo, 0.0))


def _nsa_kernel(q_ref, kc_ref, vc_ref, ksa_ref, vs_ref, kw_ref, vw_ref, gates_ref, slopeq_ref, ovt_ref, wbias_ref,
                o_ref, *, top_k):
    gqa, tq = q_ref.shape[1], q_ref.shape[2]
    ncp = kc_ref.shape[2]
    rows = gqa * tq
    q0 = pl.program_id(2) * tq
    q = q_ref[0].reshape(rows, LANES)
    slope_q = slopeq_ref[0]
    q_pos = q + slope_q
    row = lax.broadcasted_iota(jnp.int32, (rows, 1), 0)
    t_rows = q0 + (row & (tq - 1))

    cend = lax.broadcasted_iota(jnp.int32, (1, ncp), 1) * CMP_STRIDE + (CMP_BLOCK - 1)
    valid_c = t_rows >= cend
    s_c = jnp.where(valid_c, _dot_nt(q_pos, kc_ref[0, 0]), NEG_BIG)
    e_c = jnp.where(valid_c, jnp.exp2(s_c - jnp.max(s_c, axis=-1, keepdims=True)), 0.0)
    l_c = jnp.sum(e_c, axis=-1, keepdims=True)
    p_c = e_c * (1.0 / jnp.where(l_c > 0.0, l_c, 1.0))
    o_c = _dot(p_c.astype(BF16), vc_ref[0, 0])

    wk = WINDOW + tq
    start = pl.multiple_of(jnp.maximum(q0 - WINDOW, 0), tq)
    s_w = _dot_nt(q_pos, kw_ref[0, 0, pl.ds(start, wk), :]).reshape(gqa, tq, wk) + wbias_ref[...]
    s_w = s_w.reshape(rows, wk)
    e_w = jnp.exp2(s_w - jnp.max(s_w, axis=-1, keepdims=True))
    l_w = jnp.sum(e_w, axis=-1, keepdims=True)
    o_w = _dot(e_w.astype(BF16), vw_ref[0, 0, pl.ds(start, wk), :]) * (1.0 / l_w)

    p_sum = p_c[0:tq]
    for g in range(1, gqa):
        p_sum = p_sum + p_c[g * tq:(g + 1) * tq]
    imp_t = _dot_nt(ovt_ref[...], jnp.concatenate(_split3_bf16(p_sum), axis=-1))

    jb = lax.broadcasted_iota(jnp.int32, (SEL_LANES, 1), 0)
    jb_f = jb.astype(F32)
    cur = lax.shift_right_logical(q0 + lax.broadcasted_iota(jnp.int32, (1, tq), 1), SEL_SHIFT)
    forced = (jb == 0) | (jb == cur) | (jb == cur - 1)
    work = jnp.where(forced, FORCED_SCORE, jnp.where(jb > cur, NEG_BIG, imp_t))
    sel_t = jnp.zeros((SEL_LANES, tq), F32)
    for _ in range(top_k):
        mx = jnp.max(work, axis=0, keepdims=True)
        idx = jnp.min(jnp.where(work == mx, jb_f, float(SEL_LANES)), axis=0, keepdims=True)
        pick = jb_f == idx
        sel_t = jnp.where(pick, 1.0, sel_t)
        work = jnp.where(pick, -jnp.inf, work)
    pad_t = jnp.concatenate([jnp.zeros((SEL_LANES, tq), F32), sel_t - 1.0], axis=0)
    sel_q = pad_t.T
    q_sel = (q.astype(F32) + jnp.concatenate([sel_q] * gqa, axis=0)).astype(BF16)
    q_aug = jnp.concatenate([q_sel, slope_q], axis=-1)

    tk = K_TILE
    n_full = q0 // tk

    def scores(kt):
        return _dot_nt(q_aug, ksa_ref[0, 0, pl.ds(pl.multiple_of(kt * tk, tk), tk), :])

    def update(s, kt, carry, causal):
        m_run, l_run, acc = carry
        k0 = pl.multiple_of(kt * tk, tk)
        if causal:
            s = jnp.where(lax.broadcasted_iota(jnp.int32, (1, tk), 1) <= t_rows - k0, s, NEG_BIG)
        m_new = jnp.maximum(m_run, jnp.max(s, axis=-1, keepdims=True))
        alpha = jnp.exp2(m_run - m_new)
        p = jnp.exp2(s - m_new)
        l_new = alpha * l_run + jnp.sum(p, axis=-1, keepdims=True)
        return m_new, l_new, alpha * acc + _dot(p.astype(BF16), vs_ref[0, 0, pl.ds(k0, tk), :])

    def pair(kt, carry, causal_second):
        s_a, s_b = scores(kt), scores(kt + 1)
        return update(s_b, kt + 1, update(s_a, kt, carry, False), causal_second)

    init = (jnp.full((rows, 1), NEG_BIG, F32), jnp.zeros((rows, 1), F32), jnp.zeros((rows, HEAD_DIM), F32))
    carry = lax.fori_loop(0, n_full // 2, lambda j, c: pair(2 * j, c, False), init)
    _, l_s, acc_s = lax.cond(
        (n_full & 1) == 1,
        lambda c: pair(n_full - 1, c, True),
        lambda c: update(scores(n_full), n_full, c, True),
        carry)
    o_s = acc_s * (1.0 / l_s)

    gt = gates_ref[0, 0]
    outs = []
    for g in range(gqa):
        sl = slice(g * tq, (g + 1) * tq)
        c = g * N_BRANCH
        outs.append(gt[:, c:c + 1] * o_c[sl] + gt[:, c + 1:c + 2] * o_s[sl] + gt[:, c + 2:c + 3] * o_w[sl])
    o_ref[0] = jnp.concatenate(outs, axis=-1).astype(o_ref.dtype)


def _window_bias(tq):
    off = np.minimum(np.arange(WINDOW // tq + 1) * tq, WINDOW)[:, None, None]
    d = off + np.arange(tq)[None, :, None] - np.arange(WINDOW + tq)[None, None, :]
    return jnp.asarray(np.where((d >= 0) & (d < WINDOW), 0.0, NEG_BIG).astype(np.float32))


def _nsa_attention(q, kc, vc, ksa, vs, kw, vw, gates, slope_q, ovt, top_k):
    B, n_heads, T, _ = q.shape
    gqa = n_heads // NSA_KV_HEADS
    ncp = kc.shape[2]
    tq = Q_TILE
    rows = gqa * tq
    kv_spec = lambda n, w: pl.BlockSpec((1, 1, n, w), lambda b, h, i: (b, h, 0, 0))
    return pl.pallas_call(
        functools.partial(_nsa_kernel, top_k=top_k),
        grid=(B, NSA_KV_HEADS, T // tq),
        in_specs=[
            pl.BlockSpec((1, gqa, tq, LANES), lambda b, h, i: (b, h, i, 0)),
            kv_spec(ncp, LANES),
            kv_spec(ncp, HEAD_DIM),
            kv_spec(T, 2 * LANES),
            kv_spec(T, HEAD_DIM),
            kv_spec(T, LANES),
            kv_spec(T, HEAD_DIM),
            pl.BlockSpec((1, 1, tq, LANES), lambda b, h, i: (b, h, i, 0)),
            pl.BlockSpec((1, rows, LANES), lambda b, h, i: (h, 0, 0)),
            pl.BlockSpec((SEL_LANES, 3 * ncp), lambda b, h, i: (0, 0)),
            pl.BlockSpec((1, tq, WINDOW + tq), lambda b, h, i: (jnp.minimum(i, WINDOW // tq), 0, 0)),
        ],
        out_specs=pl.BlockSpec((1, tq, gqa * HEAD_DIM), lambda b, h, i: (b, i, h)),
        out_shape=jax.ShapeDtypeStruct((B, T, n_heads * HEAD_DIM), BF16),
        compiler_params=pltpu.CompilerParams(
            dimension_semantics=("parallel", "parallel", "arbitrary"), vmem_limit_bytes=VMEM_LIMIT),
        name="nsa_attention",
    )(q, kc, vc, ksa, vs, kw, vw, gates, slope_q, ovt, _window_bias(tq))


def _rwkv_prep_kernel(c_ref, cprev_ref, mu_ref, wwa_ref, g2_ref, w0_ref, a0_ref, kk_ref, ka_ref, rk_ref,
                      r_ref, w_ref, k_ref, v_ref, a_ref, b_ref, g_ref, bonus_ref):
    tt = c_ref.shape[1]
    width = r_ref.shape[2]
    n_heads = width // HEAD_DIM
    c = c_ref[0]
    last = jnp.where(pl.program_id(1) > 0, cprev_ref[0][7:8, :], 0.0)
    row = lax.broadcasted_iota(jnp.int32, (tt, 1), 0)
    prev = jnp.where(row == 0, last, pltpu.roll(c, 1, axis=0))
    z = c + (prev - c) * mu_ref[...]
    r = z[:, 0:width]
    k = z[:, width:2 * width]
    v = z[:, 2 * width:3 * width]
    xwa = z[:, 3 * width:3 * width + LANES]
    xg = z[:, 3 * width + LANES:3 * width + 2 * LANES]
    lane = lax.broadcasted_iota(jnp.int32, (1, LANES), 1)
    lora = _dot(jnp.where(lane < DECAY_LORA, jnp.tanh(xwa), xwa).astype(BF16), wwa_ref[...])
    y = w0_ref[...] + lora[:, 0:width]
    softplus_neg = jnp.maximum(-y, 0.0) + jnp.log(1.0 + jnp.exp(-jnp.abs(y)))
    decay = jnp.exp(-jnp.exp(-softplus_neg - 0.5))
    a = _sigmoid(a0_ref[...] + lora[:, width:2 * width])
    g_ref[0] = _dot(_sigmoid(xg).astype(BF16), g2_ref[...])
    kk = k * kk_ref[...]
    k2 = k * (1.0 + (a - 1.0) * ka_ref[...])
    rk = r * k2 * rk_ref[...]
    r_ref[0] = r
    w_ref[0] = decay
    k_ref[0] = k2
    v_ref[0] = v
    for h in range(n_heads):
        sl = slice(h * HEAD_DIM, (h + 1) * HEAD_DIM)
        kkh = kk[:, sl]
        kkh = kkh * lax.rsqrt(jnp.maximum(jnp.sum(kkh * kkh, axis=-1, keepdims=True), 1e-24))
        a_ref[0, :, sl] = -kkh
        b_ref[0, :, sl] = kkh * a[:, sl]
        bonus_ref[0, :, sl] = jnp.sum(rk[:, sl], axis=-1, keepdims=True) * v[:, sl]


def _rwkv_prep(cols, mu, wwa, g2, w0, a0, k_k, k_a, r_k):
    B, T, ncols = cols.shape
    width = w0.shape[1]
    tt = ROW_TILE
    big = pl.BlockSpec((1, tt, width), lambda b, i: (b, i, 0))
    vec = _const_spec((1, width))
    return pl.pallas_call(
        _rwkv_prep_kernel,
        grid=(B, T // tt),
        in_specs=[
            pl.BlockSpec((1, tt, ncols), lambda b, i: (b, i, 0)),
            pl.BlockSpec((1, 8, ncols), lambda b, i: (b, jnp.maximum(i * (tt // 8) - 1, 0), 0)),
            _const_spec((1, ncols)),
            _const_spec((LANES, 2 * width)),
            _const_spec((GATE_LORA, width)),
            vec, vec, vec, vec, vec,
        ],
        out_specs=(big,) * 8,
        out_shape=(jax.ShapeDtypeStruct((B, T, width), F32),) * 8,
        compiler_params=pltpu.CompilerParams(
            dimension_semantics=("parallel", "parallel"), vmem_limit_bytes=VMEM_LIMIT),
        name="rwkv_prep",
    )(cols, cols, mu, wwa, g2, w0, a0, k_k, k_a, r_k)


def _tree_sum(terms):
    while len(terms) > 1:
        nxt = [terms[i] + terms[i + 1] for i in range(0, len(terms) - 1, 2)]
        if len(terms) % 2:
            nxt.append(terms[-1])
        terms = nxt
    return terms[0]


def _sum_sublanes(u):
    u = u + pltpu.roll(u, 4, axis=0)
    u = u + pltpu.roll(u, 2, axis=0)
    return u + pltpu.roll(u, 1, axis=0)


SCAN_ROWS = 4


def _heads_to_lanes(xa, xb):
    low = lax.broadcasted_iota(jnp.int32, (1, LANES), 1) < HEAD_DIM
    rows = []
    for j in range(xa.shape[1] // LANES):
        va, vb = xa[:, j * LANES:(j + 1) * LANES], xb[:, j * LANES:(j + 1) * LANES]
        rows.append(jnp.where(low, va, pltpu.roll(vb, HEAD_DIM, axis=1)))
        rows.append(jnp.where(low, pltpu.roll(va, HEAD_DIM, axis=1), vb))
    xt = jnp.concatenate(rows + rows, axis=0).T
    return xt[0:HEAD_DIM], xt[HEAD_DIM:2 * HEAD_DIM]


def _wkv_scan_kernel(r_in, w_in, k_in, v_in, a_in, b_in, y_out, state_ref,
                     r_ref, w_ref, k_ref, a_ref, b_ref, v_ref, y_ref):
    tc = r_in.shape[1]
    nkg = HEAD_DIM // 8
    n_rows = HEAD_DIM // 2
    low = lax.broadcasted_iota(jnp.int32, (1, LANES), 1) < HEAD_DIM

    @pl.when(pl.program_id(0) == 0)
    def _():
        state_ref[...] = jnp.zeros_like(state_ref)

    def retile(t, _):
        r_t, w_t = _heads_to_lanes(r_in[:, t, :], w_in[:, t, :])
        k_t, a_t = _heads_to_lanes(k_in[:, t, :], a_in[:, t, :])
        b_t, v_t = _heads_to_lanes(b_in[:, t, :], v_in[:, t, :])
        for dst, val in ((r_ref, r_t), (w_ref, w_t), (k_ref, k_t), (a_ref, a_t), (b_ref, b_t)):
            dst[t] = val.reshape(nkg, 8, LANES)
        v_ref[t] = jnp.where(low, v_t[0:n_rows], v_t[n_rows:2 * n_rows])
        return 0

    lax.fori_loop(0, tc, retile, 0, unroll=4)

    def row_group(gi, _):
        u0 = gi * SCAN_ROWS

        def step(t, state):
            new_state = []
            for j in range(SCAN_ROWS):
                s = state[j * nkg:(j + 1) * nkg]
                v_row = v_ref[t, pl.ds(u0 + j, 1), :]
                sa = _sum_sublanes(_tree_sum([s[g] * a_ref[t, g] for g in range(nkg)]))
                new = [s[g] * w_ref[t, g] + sa * b_ref[t, g] + v_row * k_ref[t, g] for g in range(nkg)]
                y = _sum_sublanes(_tree_sum([new[g] * r_ref[t, g] for g in range(nkg)]))
                y_ref[t, pl.ds(u0 + j, 1), :] = y[0:1]
                new_state.extend(new)
            return tuple(new_state)

        init = tuple(state_ref[u0 + j, g] for j in range(SCAN_ROWS) for g in range(nkg))
        state = lax.fori_loop(0, tc, step, init, unroll=4)
        for j in range(SCAN_ROWS):
            for g in range(nkg):
                state_ref[u0 + j, g] = state[j * nkg + g]
        return 0

    lax.fori_loop(0, n_rows // SCAN_ROWS, row_group, 0)

    def untile(i, _):
        ys = [y_ref[2 * i + d] for d in range(2)]
        m = jnp.concatenate([u for y_t in ys for u in (y_t, pltpu.roll(y_t, HEAD_DIM, axis=1))], axis=0)
        mt = m.T
        first, second = [], []
        for j in range(y_out.shape[2] // LANES):
            even, odd = mt[16 * j:16 * j + 8], mt[16 * j + 8:16 * j + 16]
            first.append(jnp.where(low, even, pltpu.roll(odd, HEAD_DIM, axis=1)))
            second.append(jnp.where(low, pltpu.roll(even, HEAD_DIM, axis=1), odd))
        y_out[:, 2 * i, :] = jnp.concatenate(first, axis=-1)
        y_out[:, 2 * i + 1, :] = jnp.concatenate(second, axis=-1)
        return 0

    lax.fori_loop(0, tc // 2, untile, 0, unroll=4)


def _wkv_scan(r, w, k, v, a, b):
    B, T, width = r.shape
    tc = SCAN_CHUNK
    spec = pl.BlockSpec((B, tc, width), lambda i: (0, i, 0))
    wide = pltpu.VMEM((tc, HEAD_DIM // 8, 8, LANES), F32)
    rows = pltpu.VMEM((tc, HEAD_DIM // 2, LANES), F32)
    return pl.pallas_call(
        _wkv_scan_kernel,
        grid=(T // tc,),
        in_specs=[spec] * 6,
        out_specs=spec,
        out_shape=jax.ShapeDtypeStruct((B, T, width), F32),
        scratch_shapes=[pltpu.VMEM((HEAD_DIM // 2, HEAD_DIM // 8, 8, LANES), F32),
                        wide, wide, wide, wide, wide, rows, rows],
        compiler_params=pltpu.CompilerParams(
            dimension_semantics=("arbitrary",), vmem_limit_bytes=VMEM_LIMIT),
        name="wkv_scan",
    )(r, w, k, v, a, b)


def _out_ffn_kernel(x_ref, ynsa_ref, ys_ref, g_ref, bonus_ref, lnw_ref, lnb_ref, wo_ref, fg_ref, w1_ref, w2_ref,
                    o_ref):
    nsa_w = ynsa_ref.shape[1]
    width = ys_ref.shape[1]
    ys = ys_ref[...]
    pieces = []
    for h in range(width // HEAD_DIM):
        yh = ys[:, h * HEAD_DIM:(h + 1) * HEAD_DIM]
        dev = yh - jnp.mean(yh, axis=-1, keepdims=True)
        pieces.append(dev * lax.rsqrt(jnp.mean(dev * dev, axis=-1, keepdims=True) + GN_EPS))
    yn = jnp.concatenate(pieces, axis=-1) * lnw_ref[...] + lnb_ref[...]
    y_rwkv = ((yn + bonus_ref[...]) * g_ref[...]).astype(BF16)
    x1 = x_ref[...] + (_dot(ynsa_ref[...], wo_ref[0:nsa_w, :]) + _dot(y_rwkv, wo_ref[nsa_w:nsa_w + width, :]))
    xn = _rms(x1, fg_ref[...]).astype(BF16)
    d_ff = w1_ref.shape[1]
    chunk = 1024
    ffn = None
    for c in range(d_ff // chunk):
        hid = _dot(xn, w1_ref[:, c * chunk:(c + 1) * chunk])
        hid = jnp.square(jnp.maximum(hid, 0.0)).astype(BF16)
        part = _dot(hid, w2_ref[c * chunk:(c + 1) * chunk, :])
        ffn = part if ffn is None else ffn + part
    o_ref[...] = x1 + ffn


def _out_ffn(x2, ynsa, ys, g, bonus, lnx_w, lnx_b, w_out, ffn_g, w1, w2):
    N, D = x2.shape
    nsa_w = ynsa.shape[1]
    width = ys.shape[1]
    d_ff = w1.shape[1]
    tm = ROW_TILE
    rowspec = lambda w: pl.BlockSpec((tm, w), lambda i: (i, 0))
    return pl.pallas_call(
        _out_ffn_kernel,
        grid=(N // tm,),
        in_specs=[
            rowspec(D), rowspec(nsa_w), rowspec(width), rowspec(width), rowspec(width),
            _const_spec((1, width)), _const_spec((1, width)),
            _const_spec((nsa_w + width, D)),
            _const_spec((1, D)),
            _const_spec((D, d_ff)),
            _const_spec((d_ff, D)),
        ],
        out_specs=rowspec(D),
        out_shape=jax.ShapeDtypeStruct((N, D), F32),
        compiler_params=pltpu.CompilerParams(
            dimension_semantics=("parallel",), vmem_limit_bytes=VMEM_LIMIT),
        name="out_ffn",
    )(x2, ynsa, ys, g, bonus, lnx_w, lnx_b, w_out, ffn_g, w1, w2)


def _alibi_slopes(n):
    start = 2.0 ** (-8.0 / n)
    return (start ** np.arange(1, n + 1)).astype(np.float32)


def _slope_lanes(n_heads):
    m = jnp.asarray(_alibi_slopes(n_heads)) * LOG2E
    pieces = jnp.stack(_split3_bf16(m), axis=-1).astype(F32)
    lanes = jnp.zeros((n_heads, LANES), F32).at[:, HEAD_DIM:HEAD_DIM + 3].set(pieces)
    lanes = lanes.at[:, HEAD_DIM + 3:HEAD_DIM + 6].set(pieces)
    gqa = n_heads // NSA_KV_HEADS
    return jnp.repeat(lanes.reshape(NSA_KV_HEADS, gqa, LANES), Q_TILE, axis=1).astype(BF16)


def _overlap_t(ncp, n_cmp, n_sel):
    ci = np.arange(ncp)[None, :] * CMP_STRIDE
    sj = np.arange(SEL_LANES)[:, None] * SEL_BLOCK
    ov = (ci <= sj + SEL_BLOCK - 1) & (ci + CMP_BLOCK - 1 >= sj)
    ov &= (np.arange(ncp)[None, :] < n_cmp) & (np.arange(SEL_LANES)[:, None] < n_sel)
    return jnp.asarray(np.tile(ov.astype(np.float32), (1, 3)), dtype=BF16)


def _compress_weights(k_pos, k_w1, k_w2, v_pos, v_w1, v_w2):
    half = CMP_BLOCK // 2
    groups = 2 * NSA_KV_HEADS

    def first_layer(lo):
        w = jnp.zeros((half, groups, HEAD_DIM, groups, CMP_HIDDEN), F32)
        for grp in range(groups):
            w1 = k_w1 if grp < NSA_KV_HEADS else v_w1
            w = w.at[:, grp, :, grp, :].set(w1.reshape(CMP_BLOCK, HEAD_DIM, CMP_HIDDEN)[lo:lo + half])
        return w.reshape(half * groups * HEAD_DIM, groups * CMP_HIDDEN).astype(BF16)

    def pos_row(lo):
        rows = [(k_pos if grp < NSA_KV_HEADS else v_pos)[lo:lo + half] for grp in range(groups)]
        return jnp.stack(rows, axis=1).reshape(1, half * groups * HEAD_DIM)

    w2 = jnp.zeros((groups, CMP_HIDDEN, groups, HEAD_DIM), F32)
    for grp in range(groups):
        w2 = w2.at[grp, :, grp, :].set(k_w2 if grp < NSA_KV_HEADS else v_w2)
    w2 = w2.reshape(groups * CMP_HIDDEN, groups * HEAD_DIM).astype(BF16)
    return first_layer(0), first_layer(half), pos_row(0), pos_row(half), w2


def _layer(x, ln_mix_g, w_in, nsa_gate_b, q_norm_g, kc_norm_g, ks_norm_g, kw_norm_g,
           cmp_k_pos, cmp_k_w1, cmp_k_w2, cmp_v_pos, cmp_v_w1, cmp_v_w2,
           rwkv_mu, rwkv_w0, rwkv_w2, rwkv_a0, rwkv_a2, rwkv_g2, rwkv_k_k, rwkv_k_a, rwkv_r_k,
           rwkv_lnx_w, rwkv_lnx_b, w_out, ln_ffn_g, w_ff1, w_ff2):
    B, T, D = x.shape
    nsa_w = D // 2
    n_heads = nsa_w // HEAD_DIM
    gqa = n_heads // NSA_KV_HEADS
    kvw = NSA_KV_HEADS * HEAD_DIM
    rw_w = D - nsa_w
    rw_heads = rw_w // HEAD_DIM
    n_sel = T // SEL_BLOCK
    ncp = T // CMP_STRIDE
    n_cmp = (T - CMP_BLOCK) // CMP_STRIDE + 1
    top_k = min(SEL_TOPK, n_sel)
    assert T % K_TILE == 0 and T % ROW_TILE == 0 and n_sel <= SEL_LANES and n_cmp == ncp - 1
    assert B * rw_heads * 2 == LANES and gqa * N_BRANCH <= LANES and T > WINDOW
    assert DECAY_LORA + AAA_LORA == LANES and GATE_LORA == LANES

    row2 = lambda u: u.reshape(1, -1)
    nsa_main = nsa_w + 6 * kvw
    gl = w_in[:, nsa_main:nsa_main + n_heads * N_BRANCH].reshape(D, NSA_KV_HEADS, gqa * N_BRANCH)
    gl = jnp.pad(gl, ((0, 0), (0, 0), (0, LANES - gqa * N_BRANCH))).reshape(D, NSA_KV_HEADS * LANES)
    w_nsa = jnp.concatenate([w_in[:, :nsa_main], gl], axis=1).astype(BF16)
    w_rwkv = w_in[:, nsa_main + n_heads * N_BRANCH:].astype(BF16)
    gate_b2 = jnp.pad(nsa_gate_b.reshape(NSA_KV_HEADS, gqa * N_BRANCH),
                      ((0, 0), (0, LANES - gqa * N_BRANCH))).reshape(1, NSA_KV_HEADS * LANES)

    q, kvc, ksa, vs, kw, vw, gates, rw_cols = _inproj(
        x, row2(ln_mix_g), w_nsa, w_rwkv, row2(q_norm_g), row2(ks_norm_g), row2(kw_norm_g), gate_b2)

    wa, wb, pa, pb, cw2 = _compress_weights(cmp_k_pos, cmp_k_w1, cmp_k_w2, cmp_v_pos, cmp_v_w1, cmp_v_w2)
    kc, vc = _compress(kvc.reshape(B, ncp, CMP_STRIDE * 2 * kvw), wa, wb, pa, pb, cw2, row2(kc_norm_g))
    y_nsa = _nsa_attention(q, kc, vc, ksa, vs, kw, vw, gates, _slope_lanes(n_heads),
                           _overlap_t(ncp, n_cmp, n_sel), top_k)

    wwa = jnp.zeros((LANES, 2 * rw_w), F32)
    wwa = wwa.at[:DECAY_LORA, :rw_w].set(rwkv_w2).at[DECAY_LORA:, rw_w:].set(rwkv_a2).astype(BF16)
    r, w, k, v, a, b, g, bonus = _rwkv_prep(
        rw_cols, row2(rwkv_mu), wwa, rwkv_g2.astype(BF16), row2(rwkv_w0), row2(rwkv_a0),
        row2(rwkv_k_k), row2(rwkv_k_a), row2(rwkv_r_k))
    ys = _wkv_scan(r, w, k, v, a, b)

    N = B * T
    out = _out_ffn(x.reshape(N, D), y_nsa.reshape(N, nsa_w), ys.reshape(N, rw_w), g.reshape(N, rw_w),
                   bonus.reshape(N, rw_w), row2(rwkv_lnx_w), row2(rwkv_lnx_b), w_out.astype(BF16),
                   row2(ln_ffn_g), w_ff1.astype(BF16), w_ff2.astype(BF16))
    return out.reshape(B, T, D)


def kernel(x, ln_mix_g, w_in, nsa_gate_b, q_norm_g, kc_norm_g, ks_norm_g, kw_norm_g, cmp_k_pos, cmp_k_w1, cmp_k_w2, cmp_v_pos, cmp_v_w1, cmp_v_w2, rwkv_mu, rwkv_w0, rwkv_w2, rwkv_a0, rwkv_a2, rwkv_g2, rwkv_k_k, rwkv_k_a, rwkv_r_k, rwkv_lnx_w, rwkv_lnx_b, w_out, ln_ffn_g, w_ff1, w_ff2):
    params = (ln_mix_g, w_in, nsa_gate_b, q_norm_g, kc_norm_g, ks_norm_g, kw_norm_g, cmp_k_pos, cmp_k_w1,
              cmp_k_w2, cmp_v_pos, cmp_v_w1, cmp_v_w2, rwkv_mu, rwkv_w0, rwkv_w2, rwkv_a0, rwkv_a2, rwkv_g2,
              rwkv_k_k, rwkv_k_a, rwkv_r_k, rwkv_lnx_w, rwkv_lnx_b, w_out, ln_ffn_g, w_ff1, w_ff2)
    for layer in range(ln_mix_g.shape[0]):
        x = _layer(x, *(p[layer] for p in params))
    return x
```

```python
import functools

import numpy as np
import jax
import jax.numpy as jnp
from jax import lax
from jax.experimental import pallas as pl
from jax.experimental.pallas import tpu as pltpu

F32 = jnp.float32
BF16 = jnp.bfloat16

HEAD_DIM = 64
NSA_KV_HEADS = 2
CMP_BLOCK = 32
CMP_STRIDE = 16
CMP_HIDDEN = 2 * HEAD_DIM
SEL_BLOCK = 64
SEL_SHIFT = 6
SEL_TOPK = 16
WINDOW = 512
N_BRANCH = 3
DECAY_LORA = 64
AAA_LORA = 64
GATE_LORA = 128
RMS_EPS = 1e-6
GN_EPS = HEAD_DIM * 1e-5
NEG_BIG = -1e30
FORCED_SCORE = 1e6
LOG2E = 1.4426950408889634

LANES = 128
SEL_LANES = 64
MASK_BIG = 32768.0
VMEM_LIMIT = 56 * 1024 * 1024

ROW_TILE = 512
Q_TILE = 256
K_TILE = 512
SCAN_CHUNK = 64


def _rms(u, g):
    return u * lax.rsqrt(jnp.mean(u * u, axis=-1, keepdims=True) + RMS_EPS) * g


def _sigmoid(u):
    return 1.0 / (1.0 + jnp.exp(-u))


def _dot(a, b):
    return jnp.dot(a, b, preferred_element_type=F32)


def _dot_nt(a, b):
    return lax.dot_general(a, b, (((1,), (1,)), ((), ())), preferred_element_type=F32)


def _const_spec(shape):
    nd = len(shape)
    return pl.BlockSpec(shape, lambda *_: (0,) * nd, pipeline_mode=pl.Buffered(1))


def _inproj_kernel(x_ref, g_ref, wn_ref, wr_ref, qg_ref, ksg_ref, kwg_ref, gb_ref,
                   q_ref, kvc_ref, ksa_ref, vs_ref, kw_ref, vw_ref, gates_ref, rw_ref):
    tm = x_ref.shape[1]
    n_q_heads = q_ref.shape[1]
    nsa_w = n_q_heads * HEAD_DIM
    kvw = NSA_KV_HEADS * HEAD_DIM
    xn = _rms(x_ref[0], g_ref[...]).astype(BF16)
    pn = _dot(xn, wn_ref[...])
    rw_ref[0] = _dot(xn, wr_ref[...])

    zeros_hd = jnp.zeros((tm, HEAD_DIM), F32)
    for h in range(n_q_heads):
        qh = _rms(pn[:, h * HEAD_DIM:(h + 1) * HEAD_DIM], qg_ref[...]) * (HEAD_DIM ** -0.5 * LOG2E)
        q_ref[0, h] = jnp.concatenate([qh, zeros_hd], axis=-1).astype(BF16)

    kvc_ref[0, 0] = pn[:, nsa_w:nsa_w + kvw]
    kvc_ref[0, 1] = pn[:, nsa_w + kvw:nsa_w + 2 * kvw]

    tok = pl.program_id(1) * tm + lax.broadcasted_iota(jnp.int32, (tm, 1), 0)
    blk = lax.shift_right_logical(tok, SEL_SHIFT)
    onehot = jnp.where(blk == lax.broadcasted_iota(jnp.int32, (1, SEL_LANES), 1), MASK_BIG, 0.0)
    pos = _pos_tile(tok)
    off = nsa_w + 2 * kvw
    for h in range(NSA_KV_HEADS):
        ksh = _rms(pn[:, off + h * HEAD_DIM:off + (h + 1) * HEAD_DIM], ksg_ref[...])
        ksa_ref[0, h] = jnp.concatenate([ksh, onehot, pos], axis=-1).astype(BF16)
        o2 = off + kvw
        vs_ref[0, h] = pn[:, o2 + h * HEAD_DIM:o2 + (h + 1) * HEAD_DIM].astype(BF16)
        o3 = off + 2 * kvw
        kwh = _rms(pn[:, o3 + h * HEAD_DIM:o3 + (h + 1) * HEAD_DIM], kwg_ref[...])
        kw_ref[0, h] = (jnp.concatenate([kwh, zeros_hd], axis=-1) + pos).astype(BF16)
        o4 = off + 3 * kvw
        vw_ref[0, h] = pn[:, o4 + h * HEAD_DIM:o4 + (h + 1) * HEAD_DIM].astype(BF16)
        o5 = off + 4 * kvw + h * LANES
        gates_ref[0, h] = _sigmoid(pn[:, o5:o5 + LANES] + gb_ref[:, h * LANES:(h + 1) * LANES])


def _inproj(x, ln_g, w_nsa, w_rwkv, q_g, ks_g, kw_g, gate_b2):
    B, T, D = x.shape
    nsa_cols = w_nsa.shape[1]
    rw_cols = w_rwkv.shape[1]
    n_q_heads = (nsa_cols - 2 * LANES - 6 * NSA_KV_HEADS * HEAD_DIM) // HEAD_DIM
    tm = ROW_TILE
    kvw = NSA_KV_HEADS * HEAD_DIM
    out_shape = (
        jax.ShapeDtypeStruct((B, n_q_heads, T, LANES), BF16),
        jax.ShapeDtypeStruct((B, 2, T, kvw), F32),
        jax.ShapeDtypeStruct((B, NSA_KV_HEADS, T, 2 * LANES), BF16),
        jax.ShapeDtypeStruct((B, NSA_KV_HEADS, T, HEAD_DIM), BF16),
        jax.ShapeDtypeStruct((B, NSA_KV_HEADS, T, LANES), BF16),
        jax.ShapeDtypeStruct((B, NSA_KV_HEADS, T, HEAD_DIM), BF16),
        jax.ShapeDtypeStruct((B, NSA_KV_HEADS, T, LANES), F32),
        jax.ShapeDtypeStruct((B, T, rw_cols), F32),
    )
    head_spec = lambda n, w: pl.BlockSpec((1, n, tm, w), lambda b, i: (b, 0, i, 0))
    return pl.pallas_call(
        _inproj_kernel,
        grid=(B, T // tm),
        in_specs=[
            pl.BlockSpec((1, tm, D), lambda b, i: (b, i, 0)),
            _const_spec((1, D)),
            _const_spec((D, nsa_cols)),
            _const_spec((D, rw_cols)),
            _const_spec((1, HEAD_DIM)),
            _const_spec((1, HEAD_DIM)),
            _const_spec((1, HEAD_DIM)),
            _const_spec((1, NSA_KV_HEADS * LANES)),
        ],
        out_specs=(
            head_spec(n_q_heads, LANES),
            head_spec(2, kvw),
            head_spec(NSA_KV_HEADS, 2 * LANES),
            head_spec(NSA_KV_HEADS, HEAD_DIM),
            head_spec(NSA_KV_HEADS, LANES),
            head_spec(NSA_KV_HEADS, HEAD_DIM),
            head_spec(NSA_KV_HEADS, LANES),
            pl.BlockSpec((1, tm, rw_cols), lambda b, i: (b, i, 0)),
        ),
        out_shape=out_shape,
        compiler_params=pltpu.CompilerParams(
            dimension_semantics=("parallel", "parallel"), vmem_limit_bytes=VMEM_LIMIT),
        name="inproj",
    )(x, ln_g, w_nsa, w_rwkv, q_g, ks_g, kw_g, gate_b2)


def _gelu_tanh(u):
    return 0.5 * u * (1.0 + jnp.tanh(np.sqrt(2.0 / np.pi).astype(np.float32) * (u + 0.044715 * (u * u * u))))


def _compress_kernel(kv_ref, wa_ref, wb_ref, pa_ref, pb_ref, w2_ref, kcg_ref, kc_ref, vc_ref):
    width = 2 * kv_ref.shape[3]
    ncp = kv_ref.shape[2] // CMP_STRIDE
    firsts, seconds = [], []
    for l in range(CMP_STRIDE):
        tok = jnp.concatenate([kv_ref[0, p, pl.ds(l, ncp, stride=CMP_STRIDE), :] for p in range(2)], axis=-1)
        cols = slice(l * width, (l + 1) * width)
        firsts.append(_dot((tok + pa_ref[:, cols]).astype(BF16), wa_ref[cols, :]))
        seconds.append(_dot((tok + pb_ref[:, cols]).astype(BF16), wb_ref[cols, :]))
    first, second = _tree_sum(firsts), _tree_sum(seconds)
    row = lax.broadcasted_iota(jnp.int32, (ncp, 1), 0)
    real = row < ncp - 1
    second = jnp.where(real, pltpu.roll(second, ncp - 1, axis=0), 0.0)
    hid = _gelu_tanh(first + second)
    out = jnp.where(real, _dot(hid.astype(BF16), w2_ref[...]), 0.0)
    zeros_hd = jnp.zeros((ncp, HEAD_DIM), F32)
    pos = _pos_tile(row * CMP_STRIDE + (CMP_BLOCK - 1))
    for h in range(NSA_KV_HEADS):
        kch = _rms(out[:, h * HEAD_DIM:(h + 1) * HEAD_DIM], kcg_ref[...])
        kc_ref[0, h] = (jnp.concatenate([kch, zeros_hd], axis=-1) + pos).astype(BF16)
        o = (NSA_KV_HEADS + h) * HEAD_DIM
        vc_ref[0, h] = out[:, o:o + HEAD_DIM].astype(BF16)


def _compress(kvc, wa, wb, pa, pb, w2, kc_g):
    B, _, T, kv_cols = kvc.shape
    ncp = T // CMP_STRIDE
    width, hid = wa.shape
    return pl.pallas_call(
        _compress_kernel,
        grid=(B,),
        in_specs=[
            pl.BlockSpec((1, 2, T, kv_cols), lambda b: (b, 0, 0, 0)),
            _const_spec((width, hid)),
            _const_spec((width, hid)),
            _const_spec((1, width)),
            _const_spec((1, width)),
            _const_spec((hid, 2 * NSA_KV_HEADS * HEAD_DIM)),
            _const_spec((1, HEAD_DIM)),
        ],
        out_specs=(
            pl.BlockSpec((1, NSA_KV_HEADS, ncp, LANES), lambda b: (b, 0, 0, 0)),
            pl.BlockSpec((1, NSA_KV_HEADS, ncp, HEAD_DIM), lambda b: (b, 0, 0, 0)),
        ),
        out_shape=(
            jax.ShapeDtypeStruct((B, NSA_KV_HEADS, ncp, LANES), BF16),
            jax.ShapeDtypeStruct((B, NSA_KV_HEADS, ncp, HEAD_DIM), BF16),
        ),
        compiler_params=pltpu.CompilerParams(
            dimension_semantics=("parallel",), vmem_limit_bytes=VMEM_LIMIT),
        name="compress",
    )(kvc, wa, wb, pa, pb, w2, kc_g)


def _split3_bf16(u):
    hi = u.astype(BF16)
    r1 = u - hi.astype(F32)
    mid = r1.astype(BF16)
    lo = (r1 - mid.astype(F32)).astype(BF16)
    return hi, mid, lo


def _pos_tile(pos):
    lane = lax.broadcasted_iota(jnp.int32, (1, LANES), 1)
    hi = lax.shift_left(lax.shift_right_logical(pos, SEL_SHIFT), SEL_SHIFT).astype(F32)
    lo = (pos & (SEL_BLOCK - 1)).astype(F32)
    first = HEAD_DIM
    return jnp.where((lane >= first) & (lane < first + 3), hi,
                     jnp.where((lane >= first + 3) & (lane < first + 6), lo, 0.0))


def _nsa_kernel(q_ref, kc_ref, vc_ref, ksa_ref, vs_ref, kw_ref, vw_ref, gates_ref, slopeq_ref, ovt_ref, wbias_ref,
                o_ref, *, top_k):
    gqa, tq = q_ref.shape[1], q_ref.shape[2]
    ncp = kc_ref.shape[2]
    rows = gqa * tq
    q0 = pl.program_id(2) * tq
    q = q_ref[0].reshape(rows, LANES)
    slope_q = slopeq_ref[0]
    q_pos = q + slope_q
    row = lax.broadcasted_iota(jnp.int32, (rows, 1), 0)
    t_rows = q0 + (row & (tq - 1))

    cend = lax.broadcasted_iota(jnp.int32, (1, ncp), 1) * CMP_STRIDE + (CMP_BLOCK - 1)
    valid_c = t_rows >= cend
    s_c = jnp.where(valid_c, _dot_nt(q_pos, kc_ref[0, 0]), NEG_BIG)
    e_c = jnp.where(valid_c, jnp.exp2(s_c - jnp.max(s_c, axis=-1, keepdims=True)), 0.0)
    l_c = jnp.sum(e_c, axis=-1, keepdims=True)
    p_c = e_c * (1.0 / jnp.where(l_c > 0.0, l_c, 1.0))
    o_c = _dot(p_c.astype(BF16), vc_ref[0, 0])

    wk = WINDOW + tq
    start = pl.multiple_of(jnp.maximum(q0 - WINDOW, 0), tq)
    s_w = _dot_nt(q_pos, kw_ref[0, 0, pl.ds(start, wk), :]).reshape(gqa, tq, wk) + wbias_ref[...]
    s_w = s_w.reshape(rows, wk)
    e_w = jnp.exp2(s_w - jnp.max(s_w, axis=-1, keepdims=True))
    l_w = jnp.sum(e_w, axis=-1, keepdims=True)
    o_w = _dot(e_w.astype(BF16), vw_ref[0, 0, pl.ds(start, wk), :]) * (1.0 / l_w)

    p_sum = p_c[0:tq]
    for g in range(1, gqa):
        p_sum = p_sum + p_c[g * tq:(g + 1) * tq]
    imp_t = _dot_nt(ovt_ref[...], jnp.concatenate(_split3_bf16(p_sum), axis=-1))

    jb = lax.broadcasted_iota(jnp.int32, (SEL_LANES, 1), 0)
    jb_f = jb.astype(F32)
    cur = lax.shift_right_logical(q0 + lax.broadcasted_iota(jnp.int32, (1, tq), 1), SEL_SHIFT)
    forced = (jb == 0) | (jb == cur) | (jb == cur - 1)
    work = jnp.where(forced, FORCED_SCORE, jnp.where(jb > cur, NEG_BIG, imp_t))
    sel_t = jnp.zeros((SEL_LANES, tq), F32)
    for _ in range(top_k):
        mx = jnp.max(work, axis=0, keepdims=True)
        idx = jnp.min(jnp.where(work == mx, jb_f, float(SEL_LANES)), axis=0, keepdims=True)
        pick = jb_f == idx
        sel_t = jnp.where(pick, 1.0, sel_t)
        work = jnp.where(pick, -jnp.inf, work)
    pad_t = jnp.concatenate([jnp.zeros((SEL_LANES, tq), F32), sel_t - 1.0], axis=0)
    sel_q = pad_t.T
    q_sel = (q.astype(F32) + jnp.concatenate([sel_q] * gqa, axis=0)).astype(BF16)
    q_aug = jnp.concatenate([q_sel, slope_q], axis=-1)

    tk = K_TILE
    n_full = q0 // tk

    def scores(kt):
        return _dot_nt(q_aug, ksa_ref[0, 0, pl.ds(pl.multiple_of(kt * tk, tk), tk), :])

    def update(s, kt, carry, causal):
        m_run, l_run, acc = carry
        k0 = pl.multiple_of(kt * tk, tk)
        if causal:
            s = jnp.where(lax.broadcasted_iota(jnp.int32, (1, tk), 1) <= t_rows - k0, s, NEG_BIG)
        m_new = jnp.maximum(m_run, jnp.max(s, axis=-1, keepdims=True))
        alpha = jnp.exp2(m_run - m_new)
        p = jnp.exp2(s - m_new)
        l_new = alpha * l_run + jnp.sum(p, axis=-1, keepdims=True)
        return m_new, l_new, alpha * acc + _dot(p.astype(BF16), vs_ref[0, 0, pl.ds(k0, tk), :])

    def pair(kt, carry, causal_second):
        s_a, s_b = scores(kt), scores(kt + 1)
        return update(s_b, kt + 1, update(s_a, kt, carry, False), causal_second)

    init = (jnp.full((rows, 1), NEG_BIG, F32), jnp.zeros((rows, 1), F32), jnp.zeros((rows, HEAD_DIM), F32))
    carry = lax.fori_loop(0, n_full // 2, lambda j, c: pair(2 * j, c, False), init)
    _, l_s, acc_s = lax.cond(
        (n_full & 1) == 1,
        lambda c: pair(n_full - 1, c, True),
        lambda c: update(scores(n_full), n_full, c, True),
        carry)
    o_s = acc_s * (1.0 / l_s)

    gt = gates_ref[0, 0]
    outs = []
    for g in range(gqa):
        sl = slice(g * tq, (g + 1) * tq)
        c = g * N_BRANCH
        outs.append(gt[:, c:c + 1] * o_c[sl] + gt[:, c + 1:c + 2] * o_s[sl] + gt[:, c + 2:c + 3] * o_w[sl])
    o_ref[0] = jnp.concatenate(outs, axis=-1).astype(o_ref.dtype)


def _window_bias(tq):
    off = np.minimum(np.arange(WINDOW // tq + 1) * tq, WINDOW)[:, None, None]
    d = off + np.arange(tq)[None, :, None] - np.arange(WINDOW + tq)[None, None, :]
    return jnp.asarray(np.where((d >= 0) & (d < WINDOW), 0.0, NEG_BIG).astype(np.float32))


def _nsa_attention(q, kc, vc, ksa, vs, kw, vw, gates, slope_q, ovt, top_k):
    B, n_heads, T, _ = q.shape
    gqa = n_heads // NSA_KV_HEADS
    ncp = kc.shape[2]
    tq = Q_TILE
    rows = gqa * tq
    kv_spec = lambda n, w: pl.BlockSpec((1, 1, n, w), lambda b, h, i: (b, h, 0, 0))
    return pl.pallas_call(
        functools.partial(_nsa_kernel, top_k=top_k),
        grid=(B, NSA_KV_HEADS, T // tq),
        in_specs=[
            pl.BlockSpec((1, gqa, tq, LANES), lambda b, h, i: (b, h, i, 0)),
            kv_spec(ncp, LANES),
            kv_spec(ncp, HEAD_DIM),
            kv_spec(T, 2 * LANES),
            kv_spec(T, HEAD_DIM),
            kv_spec(T, LANES),
            kv_spec(T, HEAD_DIM),
            pl.BlockSpec((1, 1, tq, LANES), lambda b, h, i: (b, h, i, 0)),
            pl.BlockSpec((1, rows, LANES), lambda b, h, i: (h, 0, 0)),
            pl.BlockSpec((SEL_LANES, 3 * ncp), lambda b, h, i: (0, 0)),
            pl.BlockSpec((1, tq, WINDOW + tq), lambda b, h, i: (jnp.minimum(i, WINDOW // tq), 0, 0)),
        ],
        out_specs=pl.BlockSpec((1, tq, gqa * HEAD_DIM), lambda b, h, i: (b, i, h)),
        out_shape=jax.ShapeDtypeStruct((B, T, n_heads * HEAD_DIM), BF16),
        compiler_params=pltpu.CompilerParams(
            dimension_semantics=("parallel", "parallel", "arbitrary"), vmem_limit_bytes=VMEM_LIMIT),
        name="nsa_attention",
    )(q, kc, vc, ksa, vs, kw, vw, gates, slope_q, ovt, _window_bias(tq))


def _rwkv_prep_kernel(c_ref, cprev_ref, mu_ref, wwa_ref, g2_ref, w0_ref, a0_ref, kk_ref, ka_ref, rk_ref,
                      r_ref, w_ref, k_ref, v_ref, a_ref, b_ref, g_ref, bonus_ref):
    tt = c_ref.shape[1]
    width = r_ref.shape[2]
    n_heads = width // HEAD_DIM
    c = c_ref[0]
    last = jnp.where(pl.program_id(1) > 0, cprev_ref[0][7:8, :], 0.0)
    row = lax.broadcasted_iota(jnp.int32, (tt, 1), 0)
    prev = jnp.where(row == 0, last, pltpu.roll(c, 1, axis=0))
    z = c + (prev - c) * mu_ref[...]
    r = z[:, 0:width]
    k = z[:, width:2 * width]
    v = z[:, 2 * width:3 * width]
    xwa = z[:, 3 * width:3 * width + LANES]
    xg = z[:, 3 * width + LANES:3 * width + 2 * LANES]
    lane = lax.broadcasted_iota(jnp.int32, (1, LANES), 1)
    lora = _dot(jnp.where(lane < DECAY_LORA, jnp.tanh(xwa), xwa).astype(BF16), wwa_ref[...])
    y = w0_ref[...] + lora[:, 0:width]
    softplus_neg = jnp.maximum(-y, 0.0) + jnp.log(1.0 + jnp.exp(-jnp.abs(y)))
    decay = jnp.exp(-jnp.exp(-softplus_neg - 0.5))
    a = _sigmoid(a0_ref[...] + lora[:, width:2 * width])
    g_ref[0] = _dot(_sigmoid(xg).astype(BF16), g2_ref[...])
    kk = k * kk_ref[...]
    k2 = k * (1.0 + (a - 1.0) * ka_ref[...])
    rk = r * k2 * rk_ref[...]
    r_ref[0] = r
    w_ref[0] = decay
    k_ref[0] = k2
    v_ref[0] = v
    for h in range(n_heads):
        sl = slice(h * HEAD_DIM, (h + 1) * HEAD_DIM)
        kkh = kk[:, sl]
        kkh = kkh * lax.rsqrt(jnp.maximum(jnp.sum(kkh * kkh, axis=-1, keepdims=True), 1e-24))
        a_ref[0, :, sl] = -kkh
        b_ref[0, :, sl] = kkh * a[:, sl]
        bonus_ref[0, :, sl] = jnp.sum(rk[:, sl], axis=-1, keepdims=True) * v[:, sl]


def _rwkv_prep(cols, mu, wwa, g2, w0, a0, k_k, k_a, r_k):
    B, T, ncols = cols.shape
    width = w0.shape[1]
    tt = ROW_TILE
    big = pl.BlockSpec((1, tt, width), lambda b, i: (b, i, 0))
    vec = _const_spec((1, width))
    return pl.pallas_call(
        _rwkv_prep_kernel,
        grid=(B, T // tt),
        in_specs=[
            pl.BlockSpec((1, tt, ncols), lambda b, i: (b, i, 0)),
            pl.BlockSpec((1, 8, ncols), lambda b, i: (b, jnp.maximum(i * (tt // 8) - 1, 0), 0)),
            _const_spec((1, ncols)),
            _const_spec((LANES, 2 * width)),
            _const_spec((GATE_LORA, width)),
            vec, vec, vec, vec, vec,
        ],
        out_specs=(big,) * 8,
        out_shape=(jax.ShapeDtypeStruct((B, T, width), F32),) * 8,
        compiler_params=pltpu.CompilerParams(
            dimension_semantics=("parallel", "parallel"), vmem_limit_bytes=VMEM_LIMIT),
        name="rwkv_prep",
    )(cols, cols, mu, wwa, g2, w0, a0, k_k, k_a, r_k)


def _tree_sum(terms):
    while len(terms) > 1:
        nxt = [terms[i] + terms[i + 1] for i in range(0, len(terms) - 1, 2)]
        if len(terms) % 2:
            nxt.append(terms[-1])
        terms = nxt
    return terms[0]


def _sum_sublanes(u):
    u = u + pltpu.roll(u, 4, axis=0)
    u = u + pltpu.roll(u, 2, axis=0)
    return u + pltpu.roll(u, 1, axis=0)


SCAN_ROWS = 4


def _heads_to_lanes(xa, xb):
    low = lax.broadcasted_iota(jnp.int32, (1, LANES), 1) < HEAD_DIM
    rows = []
    for j in range(xa.shape[1] // LANES):
        va, vb = xa[:, j * LANES:(j + 1) * LANES], xb[:, j * LANES:(j + 1) * LANES]
        rows.append(jnp.where(low, va, pltpu.roll(vb, HEAD_DIM, axis=1)))
        rows.append(jnp.where(low, pltpu.roll(va, HEAD_DIM, axis=1), vb))
    xt = jnp.concatenate(rows + rows, axis=0).T
    return xt[0:HEAD_DIM], xt[HEAD_DIM:2 * HEAD_DIM]


def _wkv_scan_kernel(r_in, w_in, k_in, v_in, a_in, b_in, y_out, state_ref,
                     r_ref, w_ref, k_ref, a_ref, b_ref, v_ref, y_ref):
    tc = r_in.shape[1]
    nkg = HEAD_DIM // 8
    n_rows = HEAD_DIM // 2
    low = lax.broadcasted_iota(jnp.int32, (1, LANES), 1) < HEAD_DIM

    @pl.when(pl.program_id(0) == 0)
    def _():
        state_ref[...] = jnp.zeros_like(state_ref)

    def retile(t, _):
        r_t, w_t = _heads_to_lanes(r_in[:, t, :], w_in[:, t, :])
        k_t, a_t = _heads_to_lanes(k_in[:, t, :], a_in[:, t, :])
        b_t, v_t = _heads_to_lanes(b_in[:, t, :], v_in[:, t, :])
        for dst, val in ((r_ref, r_t), (w_ref, w_t), (k_ref, k_t), (a_ref, a_t), (b_ref, b_t)):
            dst[t] = val.reshape(nkg, 8, LANES)
        v_ref[t] = jnp.where(low, v_t[0:n_rows], v_t[n_rows:2 * n_rows])
        return 0

    lax.fori_loop(0, tc, retile, 0, unroll=4)

    def row_group(gi, _):
        u0 = gi * SCAN_ROWS

        def step(t, state):
            new_state = []
            for j in range(SCAN_ROWS):
                s = state[j * nkg:(j + 1) * nkg]
                v_row = v_ref[t, pl.ds(u0 + j, 1), :]
                sa = _sum_sublanes(_tree_sum([s[g] * a_ref[t, g] for g in range(nkg)]))
                new = [s[g] * w_ref[t, g] + sa * b_ref[t, g] + v_row * k_ref[t, g] for g in range(nkg)]
                y = _sum_sublanes(_tree_sum([new[g] * r_ref[t, g] for g in range(nkg)]))
                y_ref[t, pl.ds(u0 + j, 1), :] = y[0:1]
                new_state.extend(new)
            return tuple(new_state)

        init = tuple(state_ref[u0 + j, g] for j in range(SCAN_ROWS) for g in range(nkg))
        state = lax.fori_loop(0, tc, step, init, unroll=4)
        for j in range(SCAN_ROWS):
            for g in range(nkg):
                state_ref[u0 + j, g] = state[j * nkg + g]
        return 0

    lax.fori_loop(0, n_rows // SCAN_ROWS, row_group, 0)

    def untile(i, _):
        ys = [y_ref[2 * i + d] for d in range(2)]
        m = jnp.concatenate([u for y_t in ys for u in (y_t, pltpu.roll(y_t, HEAD_DIM, axis=1))], axis=0)
        mt = m.T
        first, second = [], []
        for j in range(y_out.shape[2] // LANES):
            even, odd = mt[16 * j:16 * j + 8], mt[16 * j + 8:16 * j + 16]
            first.append(jnp.where(low, even, pltpu.roll(odd, HEAD_DIM, axis=1)))
            second.append(jnp.where(low, pltpu.roll(even, HEAD_DIM, axis=1), odd))
        y_out[:, 2 * i, :] = jnp.concatenate(first, axis=-1)
        y_out[:, 2 * i + 1, :] = jnp.concatenate(second, axis=-1)
        return 0

    lax.fori_loop(0, tc // 2, untile, 0, unroll=4)


def _wkv_scan(r, w, k, v, a, b):
    B, T, width = r.shape
    tc = SCAN_CHUNK
    spec = pl.BlockSpec((B, tc, width), lambda i: (0, i, 0))
    wide = pltpu.VMEM((tc, HEAD_DIM // 8, 8, LANES), F32)
    rows = pltpu.VMEM((tc, HEAD_DIM // 2, LANES), F32)
    return pl.pallas_call(
        _wkv_scan_kernel,
        grid=(T // tc,),
        in_specs=[spec] * 6,
        out_specs=spec,
        out_shape=jax.ShapeDtypeStruct((B, T, width), F32),
        scratch_shapes=[pltpu.VMEM((HEAD_DIM // 2, HEAD_DIM // 8, 8, LANES), F32),
                        wide, wide, wide, wide, wide, rows, rows],
        compiler_params=pltpu.CompilerParams(
            dimension_semantics=("arbitrary",), vmem_limit_bytes=VMEM_LIMIT),
        name="wkv_scan",
    )(r, w, k, v, a, b)


def _out_ffn_kernel(x_ref, ynsa_ref, ys_ref, g_ref, bonus_ref, lnw_ref, lnb_ref, wo_ref, fg_ref, w1_ref, w2_ref,
                    o_ref):
    nsa_w = ynsa_ref.shape[1]
    width = ys_ref.shape[1]
    ys = ys_ref[...]
    pieces = []
    for h in range(width // HEAD_DIM):
        yh = ys[:, h * HEAD_DIM:(h + 1) * HEAD_DIM]
        dev = yh - jnp.mean(yh, axis=-1, keepdims=True)
        pieces.append(dev * lax.rsqrt(jnp.mean(dev * dev, axis=-1, keepdims=True) + GN_EPS))
    yn = jnp.concatenate(pieces, axis=-1) * lnw_ref[...] + lnb_ref[...]
    y_rwkv = ((yn + bonus_ref[...]) * g_ref[...]).astype(BF16)
    x1 = x_ref[...] + (_dot(ynsa_ref[...], wo_ref[0:nsa_w, :]) + _dot(y_rwkv, wo_ref[nsa_w:nsa_w + width, :]))
    xn = _rms(x1, fg_ref[...]).astype(BF16)
    d_ff = w1_ref.shape[1]
    chunk = 1024
    ffn = None
    for c in range(d_ff // chunk):
        hid = _dot(xn, w1_ref[:, c * chunk:(c + 1) * chunk])
        hid = jnp.square(jnp.maximum(hid, 0.0)).astype(BF16)
        part = _dot(hid, w2_ref[c * chunk:(c + 1) * chunk, :])
        ffn = part if ffn is None else ffn + part
    o_ref[...] = x1 + ffn


def _out_ffn(x2, ynsa, ys, g, bonus, lnx_w, lnx_b, w_out, ffn_g, w1, w2):
    N, D = x2.shape
    nsa_w = ynsa.shape[1]
    width = ys.shape[1]
    d_ff = w1.shape[1]
    tm = ROW_TILE
    rowspec = lambda w: pl.BlockSpec((tm, w), lambda i: (i, 0))
    return pl.pallas_call(
        _out_ffn_kernel,
        grid=(N // tm,),
        in_specs=[
            rowspec(D), rowspec(nsa_w), rowspec(width), rowspec(width), rowspec(width),
            _const_spec((1, width)), _const_spec((1, width)),
            _const_spec((nsa_w + width, D)),
            _const_spec((1, D)),
            _const_spec((D, d_ff)),
            _const_spec((d_ff, D)),
        ],
        out_specs=rowspec(D),
        out_shape=jax.ShapeDtypeStruct((N, D), F32),
        compiler_params=pltpu.CompilerParams(
            dimension_semantics=("parallel",), vmem_limit_bytes=VMEM_LIMIT),
        name="out_ffn",
    )(x2, ynsa, ys, g, bonus, lnx_w, lnx_b, w_out, ffn_g, w1, w2)


def _alibi_slopes(n):
    start = 2.0 ** (-8.0 / n)
    return (start ** np.arange(1, n + 1)).astype(np.float32)


def _slope_lanes(n_heads):
    m = jnp.asarray(_alibi_slopes(n_heads)) * LOG2E
    pieces = jnp.stack(_split3_bf16(m), axis=-1).astype(F32)
    lanes = jnp.zeros((n_heads, LANES), F32).at[:, HEAD_DIM:HEAD_DIM + 3].set(pieces)
    lanes = lanes.at[:, HEAD_DIM + 3:HEAD_DIM + 6].set(pieces)
    gqa = n_heads // NSA_KV_HEADS
    return jnp.repeat(lanes.reshape(NSA_KV_HEADS, gqa, LANES), Q_TILE, axis=1).astype(BF16)


def _overlap_t(ncp, n_cmp, n_sel):
    ci = np.arange(ncp)[None, :] * CMP_STRIDE
    sj = np.arange(SEL_LANES)[:, None] * SEL_BLOCK
    ov = (ci <= sj + SEL_BLOCK - 1) & (ci + CMP_BLOCK - 1 >= sj)
    ov &= (np.arange(ncp)[None, :] < n_cmp) & (np.arange(SEL_LANES)[:, None] < n_sel)
    return jnp.asarray(np.tile(ov.astype(np.float32), (1, 3)), dtype=BF16)


def _compress_weights(k_pos, k_w1, k_w2, v_pos, v_w1, v_w2):
    half = CMP_BLOCK // 2
    groups = 2 * NSA_KV_HEADS

    eye = jnp.eye(groups, dtype=F32)
    per_group = lambda k_part, v_part: jnp.stack([k_part] * NSA_KV_HEADS + [v_part] * NSA_KV_HEADS)

    def first_layer(lo):
        w1 = per_group(k_w1, v_w1).reshape(groups, CMP_BLOCK, HEAD_DIM, CMP_HIDDEN)[:, lo:lo + half]
        w = w1.transpose(1, 0, 2, 3)[:, :, :, None, :] * eye[None, :, None, :, None]
        return w.reshape(half * groups * HEAD_DIM, groups * CMP_HIDDEN).astype(BF16)

    def pos_row(lo):
        return per_group(k_pos, v_pos)[:, lo:lo + half].transpose(1, 0, 2).reshape(1, half * groups * HEAD_DIM)

    w2 = per_group(k_w2, v_w2)[:, :, None, :] * eye[:, None, :, None]
    w2 = w2.reshape(groups * CMP_HIDDEN, groups * HEAD_DIM).astype(BF16)
    return first_layer(0), first_layer(half), pos_row(0), pos_row(half), w2


def _layer(x, ln_mix_g, w_in, nsa_gate_b, q_norm_g, kc_norm_g, ks_norm_g, kw_norm_g,
           cmp_k_pos, cmp_k_w1, cmp_k_w2, cmp_v_pos, cmp_v_w1, cmp_v_w2,
           rwkv_mu, rwkv_w0, rwkv_w2, rwkv_a0, rwkv_a2, rwkv_g2, rwkv_k_k, rwkv_k_a, rwkv_r_k,
           rwkv_lnx_w, rwkv_lnx_b, w_out, ln_ffn_g, w_ff1, w_ff2):
    B, T, D = x.shape
    nsa_w = D // 2
    n_heads = nsa_w // HEAD_DIM
    gqa = n_heads // NSA_KV_HEADS
    kvw = NSA_KV_HEADS * HEAD_DIM
    rw_w = D - nsa_w
    rw_heads = rw_w // HEAD_DIM
    n_sel = T // SEL_BLOCK
    ncp = T // CMP_STRIDE
    n_cmp = (T - CMP_BLOCK) // CMP_STRIDE + 1
    top_k = min(SEL_TOPK, n_sel)
    assert T % K_TILE == 0 and T % ROW_TILE == 0 and n_sel <= SEL_LANES and n_cmp == ncp - 1
    assert B * rw_heads * 2 == LANES and gqa * N_BRANCH <= LANES and T > WINDOW
    assert DECAY_LORA + AAA_LORA == LANES and GATE_LORA == LANES

    row2 = lambda u: u.reshape(1, -1)
    nsa_main = nsa_w + 6 * kvw
    gl = w_in[:, nsa_main:nsa_main + n_heads * N_BRANCH].reshape(D, NSA_KV_HEADS, gqa * N_BRANCH)
    gl = jnp.pad(gl, ((0, 0), (0, 0), (0, LANES - gqa * N_BRANCH))).reshape(D, NSA_KV_HEADS * LANES)
    w_nsa = jnp.concatenate([w_in[:, :nsa_main], gl], axis=1).astype(BF16)
    w_rwkv = w_in[:, nsa_main + n_heads * N_BRANCH:].astype(BF16)
    gate_b2 = jnp.pad(nsa_gate_b.reshape(NSA_KV_HEADS, gqa * N_BRANCH),
                      ((0, 0), (0, LANES - gqa * N_BRANCH))).reshape(1, NSA_KV_HEADS * LANES)

    q, kvc, ksa, vs, kw, vw, gates, rw_cols = _inproj(
        x, row2(ln_mix_g), w_nsa, w_rwkv, row2(q_norm_g), row2(ks_norm_g), row2(kw_norm_g), gate_b2)

    wa, wb, pa, pb, cw2 = _compress_weights(cmp_k_pos, cmp_k_w1, cmp_k_w2, cmp_v_pos, cmp_v_w1, cmp_v_w2)
    kc, vc = _compress(kvc, wa, wb, pa, pb, cw2, row2(kc_norm_g))
    y_nsa = _nsa_attention(q, kc, vc, ksa, vs, kw, vw, gates, _slope_lanes(n_heads),
                           _overlap_t(ncp, n_cmp, n_sel), top_k)

    wwa = jnp.zeros((LANES, 2 * rw_w), F32)
    wwa = wwa.at[:DECAY_LORA, :rw_w].set(rwkv_w2).at[DECAY_LORA:, rw_w:].set(rwkv_a2).astype(BF16)
    r, w, k, v, a, b, g, bonus = _rwkv_prep(
        rw_cols, row2(rwkv_mu), wwa, rwkv_g2.astype(BF16), row2(rwkv_w0), row2(rwkv_a0),
        row2(rwkv_k_k), row2(rwkv_k_a), row2(rwkv_r_k))
    ys = _wkv_scan(r, w, k, v, a, b)

    N = B * T
    out = _out_ffn(x.reshape(N, D), y_nsa.reshape(N, nsa_w), ys.reshape(N, rw_w), g.reshape(N, rw_w),
                   bonus.reshape(N, rw_w), row2(rwkv_lnx_w), row2(rwkv_lnx_b), w_out.astype(BF16),
                   row2(ln_ffn_g), w_ff1.astype(BF16), w_ff2.astype(BF16))
    return out.reshape(B, T, D)


def kernel(x, ln_mix_g, w_in, nsa_gate_b, q_norm_g, kc_norm_g, ks_norm_g, kw_norm_g, cmp_k_pos, cmp_k_w1, cmp_k_w2, cmp_v_pos, cmp_v_w1, cmp_v_w2, rwkv_mu, rwkv_w0, rwkv_w2, rwkv_a0, rwkv_a2, rwkv_g2, rwkv_k_k, rwkv_k_a, rwkv_r_k, rwkv_lnx_w, rwkv_lnx_b, w_out, ln_ffn_g, w_ff1, w_ff2):
    params = (ln_mix_g, w_in, nsa_gate_b, q_norm_g, kc_norm_g, ks_norm_g, kw_norm_g, cmp_k_pos, cmp_k_w1,
              cmp_k_w2, cmp_v_pos, cmp_v_w1, cmp_v_w2, rwkv_mu, rwkv_w0, rwkv_w2, rwkv_a0, rwkv_a2, rwkv_g2,
              rwkv_k_k, rwkv_k_a, rwkv_r_k, rwkv_lnx_w, rwkv_lnx_b, w_out, ln_ffn_g, w_ff1, w_ff2)
    for layer in range(ln_mix_g.shape[0]):
        x = _layer(x, *(p[layer] for p in params))
    return x
```

```python
import functools

import numpy as np
import jax
import jax.numpy as jnp
from jax import lax
from jax.experimental import pallas as pl
from jax.experimental.pallas import tpu as pltpu

F32 = jnp.float32
BF16 = jnp.bfloat16

HEAD_DIM = 64
NSA_KV_HEADS = 2
CMP_BLOCK = 32
CMP_STRIDE = 16
CMP_HIDDEN = 2 * HEAD_DIM
SEL_BLOCK = 64
SEL_SHIFT = 6
SEL_TOPK = 16
WINDOW = 512
N_BRANCH = 3
DECAY_LORA = 64
AAA_LORA = 64
GATE_LORA = 128
RMS_EPS = 1e-6
GN_EPS = HEAD_DIM * 1e-5
NEG_BIG = -1e30
FORCED_SCORE = 1e6
LOG2E = 1.4426950408889634

LANES = 128
SEL_LANES = 64
MASK_BIG = 2.0 ** 60
VMEM_LIMIT = 56 * 1024 * 1024

ROW_TILE = 512
Q_TILE = 256
K_TILE = 512
SCAN_CHUNK = 64


def _rms(u, g):
    return u * lax.rsqrt(jnp.mean(u * u, axis=-1, keepdims=True) + RMS_EPS) * g


def _sigmoid(u):
    return 1.0 / (1.0 + jnp.exp(-u))


def _dot(a, b):
    return jnp.dot(a, b, preferred_element_type=F32)


def _dot_nt(a, b):
    return lax.dot_general(a, b, (((1,), (1,)), ((), ())), preferred_element_type=F32)


def _const_spec(shape):
    nd = len(shape)
    return pl.BlockSpec(shape, lambda *_: (0,) * nd, pipeline_mode=pl.Buffered(1))


def _split_bf16(u, n_pieces):
    pieces, rest = [], u
    for i in range(n_pieces):
        piece = rest.astype(BF16)
        pieces.append(piece)
        if i + 1 < n_pieces:
            rest = rest - piece.astype(F32)
    return pieces


def _head_sums(u, head_ones, n_pieces):
    return _tree_sum([_dot(piece, head_ones) for piece in _split_bf16(u, n_pieces)])


def _head_ones(n_lanes):
    head_id = np.arange(n_lanes) // HEAD_DIM
    return jnp.asarray((head_id[:, None] == head_id[None, :]).astype(np.float32), dtype=BF16)


def _tree_sum(terms):
    while len(terms) > 1:
        nxt = [terms[i] + terms[i + 1] for i in range(0, len(terms) - 1, 2)]
        if len(terms) % 2:
            nxt.append(terms[-1])
        terms = nxt
    return terms[0]


def _inproj_kernel(x_ref, g_ref, wn_ref, wr_ref, qg_ref, ksg_ref, kwg_ref, gb_ref, bd_ref,
                   q_ref, kvc_ref, ksa_ref, vs_ref, kw_ref, vw_ref, gates_ref, rw_ref):
    tm = x_ref.shape[1]
    n_q_heads = q_ref.shape[1]
    nsa_w = n_q_heads * HEAD_DIM
    kvw = NSA_KV_HEADS * HEAD_DIM
    xn = _rms(x_ref[0], g_ref[...]).astype(BF16)
    pn = _dot(xn, wn_ref[...])
    rw_ref[0] = _dot(xn, wr_ref[...])

    def head_rms(u, gain):
        ss = _head_sums(u * u, bd_ref[0:u.shape[1], 0:u.shape[1]], 2)
        return u * lax.rsqrt(ss * (1.0 / HEAD_DIM) + RMS_EPS) * gain

    zeros_hd = jnp.zeros((tm, HEAD_DIM), F32)
    qn = head_rms(pn[:, 0:nsa_w], qg_ref[...]) * (HEAD_DIM ** -0.5 * LOG2E)
    for h in range(n_q_heads):
        q_ref[0, h] = jnp.concatenate([qn[:, h * HEAD_DIM:(h + 1) * HEAD_DIM], zeros_hd], axis=-1).astype(BF16)

    kvc_ref[0, 0] = pn[:, nsa_w:nsa_w + kvw]
    kvc_ref[0, 1] = pn[:, nsa_w + kvw:nsa_w + 2 * kvw]

    tok = pl.program_id(1) * tm + lax.broadcasted_iota(jnp.int32, (tm, 1), 0)
    blk = lax.shift_right_logical(tok, SEL_SHIFT)
    onehot = jnp.where(blk == lax.broadcasted_iota(jnp.int32, (1, SEL_LANES), 1), MASK_BIG, 0.0)
    pos = _pos_tile(tok)
    off = nsa_w + 2 * kvw
    ksn = head_rms(pn[:, off:off + kvw], ksg_ref[...])
    kwn = head_rms(pn[:, off + 2 * kvw:off + 3 * kvw], kwg_ref[...])
    for h in range(NSA_KV_HEADS):
        ksa_ref[0, h] = jnp.concatenate([ksn[:, h * HEAD_DIM:(h + 1) * HEAD_DIM], onehot, pos], axis=-1).astype(BF16)
        o2 = off + kvw
        vs_ref[0, h] = pn[:, o2 + h * HEAD_DIM:o2 + (h + 1) * HEAD_DIM].astype(BF16)
        kw_ref[0, h] = (jnp.concatenate([kwn[:, h * HEAD_DIM:(h + 1) * HEAD_DIM], zeros_hd], axis=-1) + pos).astype(BF16)
        o4 = off + 3 * kvw
        vw_ref[0, h] = pn[:, o4 + h * HEAD_DIM:o4 + (h + 1) * HEAD_DIM].astype(BF16)
        o5 = off + 4 * kvw + h * LANES
        gates_ref[0, h] = _sigmoid(pn[:, o5:o5 + LANES] + gb_ref[:, h * LANES:(h + 1) * LANES])


def _inproj(x, ln_g, w_nsa, w_rwkv, q_g, ks_g, kw_g, gate_b2):
    B, T, D = x.shape
    nsa_cols = w_nsa.shape[1]
    rw_cols = w_rwkv.shape[1]
    n_q_heads = (nsa_cols - 2 * LANES - 6 * NSA_KV_HEADS * HEAD_DIM) // HEAD_DIM
    tm = ROW_TILE
    kvw = NSA_KV_HEADS * HEAD_DIM
    head_ones = _head_ones(n_q_heads * HEAD_DIM)
    out_shape = (
        jax.ShapeDtypeStruct((B, n_q_heads, T, LANES), BF16),
        jax.ShapeDtypeStruct((B, 2, T, kvw), F32),
        jax.ShapeDtypeStruct((B, NSA_KV_HEADS, T, 2 * LANES), BF16),
        jax.ShapeDtypeStruct((B, NSA_KV_HEADS, T, HEAD_DIM), BF16),
        jax.ShapeDtypeStruct((B, NSA_KV_HEADS, T, LANES), BF16),
        jax.ShapeDtypeStruct((B, NSA_KV_HEADS, T, HEAD_DIM), BF16),
        jax.ShapeDtypeStruct((B, NSA_KV_HEADS, T, LANES), F32),
        jax.ShapeDtypeStruct((B, T, rw_cols), F32),
    )
    head_spec = lambda n, w: pl.BlockSpec((1, n, tm, w), lambda b, i: (b, 0, i, 0))
    return pl.pallas_call(
        _inproj_kernel,
        grid=(B, T // tm),
        in_specs=[
            pl.BlockSpec((1, tm, D), lambda b, i: (b, i, 0)),
            _const_spec((1, D)),
            _const_spec((D, nsa_cols)),
            _const_spec((D, rw_cols)),
            _const_spec((1, n_q_heads * HEAD_DIM)),
            _const_spec((1, kvw)),
            _const_spec((1, kvw)),
            _const_spec((1, NSA_KV_HEADS * LANES)),
            _const_spec((n_q_heads * HEAD_DIM, n_q_heads * HEAD_DIM)),
        ],
        out_specs=(
            head_spec(n_q_heads, LANES),
            head_spec(2, kvw),
            head_spec(NSA_KV_HEADS, 2 * LANES),
            head_spec(NSA_KV_HEADS, HEAD_DIM),
            head_spec(NSA_KV_HEADS, LANES),
            head_spec(NSA_KV_HEADS, HEAD_DIM),
            head_spec(NSA_KV_HEADS, LANES),
            pl.BlockSpec((1, tm, rw_cols), lambda b, i: (b, i, 0)),
        ),
        out_shape=out_shape,
        compiler_params=pltpu.CompilerParams(
            dimension_semantics=("parallel", "parallel"), vmem_limit_bytes=VMEM_LIMIT),
        name="inproj",
    )(x, ln_g, w_nsa, w_rwkv, q_g, ks_g, kw_g, gate_b2, head_ones)


def _gelu_tanh(u):
    return 0.5 * u * (1.0 + jnp.tanh(np.sqrt(2.0 / np.pi).astype(np.float32) * (u + 0.044715 * (u * u * u))))


def _compress_kernel(kv_ref, wa_ref, wb_ref, pa_ref, pb_ref, w2_ref, kcg_ref, kc_ref, vc_ref):
    width = 2 * kv_ref.shape[3]
    ncp = kv_ref.shape[2] // CMP_STRIDE
    firsts, seconds = [], []
    for l in range(CMP_STRIDE):
        tok = jnp.concatenate([kv_ref[0, p, pl.ds(l, ncp, stride=CMP_STRIDE), :] for p in range(2)], axis=-1)
        cols = slice(l * width, (l + 1) * width)
        firsts.append(_dot((tok + pa_ref[:, cols]).astype(BF16), wa_ref[cols, :]))
        seconds.append(_dot((tok + pb_ref[:, cols]).astype(BF16), wb_ref[cols, :]))
    first, second = _tree_sum(firsts), _tree_sum(seconds)
    row = lax.broadcasted_iota(jnp.int32, (ncp, 1), 0)
    real = row < ncp - 1
    second = jnp.where(real, pltpu.roll(second, ncp - 1, axis=0), 0.0)
    hid = _gelu_tanh(first + second)
    out = jnp.where(real, _dot(hid.astype(BF16), w2_ref[...]), 0.0)
    zeros_hd = jnp.zeros((ncp, HEAD_DIM), F32)
    pos = _pos_tile(row * CMP_STRIDE + (CMP_BLOCK - 1))
    for h in range(NSA_KV_HEADS):
        kch = _rms(out[:, h * HEAD_DIM:(h + 1) * HEAD_DIM], kcg_ref[...])
        kc_ref[0, h] = (jnp.concatenate([kch, zeros_hd], axis=-1) + pos).astype(BF16)
        o = (NSA_KV_HEADS + h) * HEAD_DIM
        vc_ref[0, h] = out[:, o:o + HEAD_DIM].astype(BF16)


def _compress(kvc, wa, wb, pa, pb, w2, kc_g):
    B, _, T, kv_cols = kvc.shape
    ncp = T // CMP_STRIDE
    width, hid = wa.shape
    return pl.pallas_call(
        _compress_kernel,
        grid=(B,),
        in_specs=[
            pl.BlockSpec((1, 2, T, kv_cols), lambda b: (b, 0, 0, 0)),
            _const_spec((width, hid)),
            _const_spec((width, hid)),
            _const_spec((1, width)),
            _const_spec((1, width)),
            _const_spec((hid, 2 * NSA_KV_HEADS * HEAD_DIM)),
            _const_spec((1, HEAD_DIM)),
        ],
        out_specs=(
            pl.BlockSpec((1, NSA_KV_HEADS, ncp, LANES), lambda b: (b, 0, 0, 0)),
            pl.BlockSpec((1, NSA_KV_HEADS, ncp, HEAD_DIM), lambda b: (b, 0, 0, 0)),
        ),
        out_shape=(
            jax.ShapeDtypeStruct((B, NSA_KV_HEADS, ncp, LANES), BF16),
            jax.ShapeDtypeStruct((B, NSA_KV_HEADS, ncp, HEAD_DIM), BF16),
        ),
        compiler_params=pltpu.CompilerParams(
            dimension_semantics=("parallel",), vmem_limit_bytes=VMEM_LIMIT),
        name="compress",
    )(kvc, wa, wb, pa, pb, w2, kc_g)


def _split3_bf16(u):
    hi = u.astype(BF16)
    r1 = u - hi.astype(F32)
    mid = r1.astype(BF16)
    lo = (r1 - mid.astype(F32)).astype(BF16)
    return hi, mid, lo


def _pos_tile(pos):
    lane = lax.broadcasted_iota(jnp.int32, (1, LANES), 1)
    hi = lax.shift_left(lax.shift_right_logical(pos, SEL_SHIFT), SEL_SHIFT).astype(F32)
    lo = (pos & (SEL_BLOCK - 1)).astype(F32)
    first = HEAD_DIM
    return jnp.where((lane >= first) & (lane < first + 3), hi,
                     jnp.where((lane >= first + 3) & (lane < first + 6), lo, 0.0))


def _nsa_kernel(q_ref, kc_ref, vc_ref, ksa_ref, vs_ref, kw_ref, vw_ref, gates_ref, slopeq_ref, ovt_ref, wbias_ref,
                o_ref, *, top_k):
    gqa, tq = q_ref.shape[1], q_ref.shape[2]
    ncp = kc_ref.shape[2]
    rows = gqa * tq
    q0 = pl.program_id(2) * tq
    q = q_ref[0].reshape(rows, LANES)
    slope_q = slopeq_ref[0]
    q_pos = q + slope_q
    row = lax.broadcasted_iota(jnp.int32, (rows, 1), 0)
    t_rows = q0 + (row & (tq - 1))

    cend = lax.broadcasted_iota(jnp.int32, (1, ncp), 1) * CMP_STRIDE + (CMP_BLOCK - 1)
    valid_c = t_rows >= cend
    s_c = jnp.where(valid_c, _dot_nt(q_pos, kc_ref[0, 0]), NEG_BIG)
    e_c = jnp.where(valid_c, jnp.exp2(s_c - jnp.max(s_c, axis=-1, keepdims=True)), 0.0)
    l_c = jnp.sum(e_c, axis=-1, keepdims=True)
    p_c = e_c * (1.0 / jnp.where(l_c > 0.0, l_c, 1.0))
    o_c = _dot(p_c.astype(BF16), vc_ref[0, 0])

    wk = WINDOW + tq
    start = pl.multiple_of(jnp.maximum(q0 - WINDOW, 0), tq)
    s_w = _dot_nt(q_pos, kw_ref[0, 0, pl.ds(start, wk), :]).reshape(gqa, tq, wk) + wbias_ref[...]
    s_w = s_w.reshape(rows, wk)
    e_w = jnp.exp2(s_w - jnp.max(s_w, axis=-1, keepdims=True))
    l_w = jnp.sum(e_w, axis=-1, keepdims=True)
    o_w = _dot(e_w.astype(BF16), vw_ref[0, 0, pl.ds(start, wk), :]) * (1.0 / l_w)

    p_sum = p_c[0:tq]
    for g in range(1, gqa):
        p_sum = p_sum + p_c[g * tq:(g + 1) * tq]
    imp_t = _dot_nt(ovt_ref[...], jnp.concatenate(_split3_bf16(p_sum), axis=-1))

    jb = lax.broadcasted_iota(jnp.int32, (SEL_LANES, 1), 0)
    jb_f = jb.astype(F32)
    cur = lax.shift_right_logical(q0 + lax.broadcasted_iota(jnp.int32, (1, tq), 1), SEL_SHIFT)
    forced = (jb == 0) | (jb == cur) | (jb == cur - 1)
    work = jnp.where(forced, FORCED_SCORE, jnp.where(jb > cur, NEG_BIG, imp_t))
    sel_t = jnp.zeros((SEL_LANES, tq), F32)
    for _ in range(top_k):
        mx = jnp.max(work, axis=0, keepdims=True)
        idx = jnp.min(jnp.where(work == mx, jb_f, float(SEL_LANES)), axis=0, keepdims=True)
        pick = jb_f == idx
        sel_t = jnp.where(pick, 1.0, sel_t)
        work = jnp.where(pick, -jnp.inf, work)
    pad_t = jnp.concatenate([jnp.zeros((SEL_LANES, tq), F32), sel_t - 1.0], axis=0)
    sel_q = pad_t.T
    q_sel = (q.astype(F32) + jnp.concatenate([sel_q] * gqa, axis=0)).astype(BF16)
    q_aug = jnp.concatenate([q_sel, slope_q], axis=-1)

    tk = K_TILE
    n_full = q0 // tk

    def scores(kt):
        return _dot_nt(q_aug, ksa_ref[0, 0, pl.ds(pl.multiple_of(kt * tk, tk), tk), :])

    def update(s, kt, carry, causal):
        m_run, l_run, acc = carry
        k0 = pl.multiple_of(kt * tk, tk)
        if causal:
            s = jnp.where(lax.broadcasted_iota(jnp.int32, (1, tk), 1) <= t_rows - k0, s, NEG_BIG)
        m_new = jnp.maximum(m_run, jnp.max(s, axis=-1, keepdims=True))
        alpha = jnp.exp2(m_run - m_new)
        p = jnp.exp2(s - m_new)
        l_new = alpha * l_run + jnp.sum(p, axis=-1, keepdims=True)
        return m_new, l_new, alpha * acc + _dot(p.astype(BF16), vs_ref[0, 0, pl.ds(k0, tk), :])

    def pair(kt, carry, causal_second):
        s_a, s_b = scores(kt), scores(kt + 1)
        return update(s_b, kt + 1, update(s_a, kt, carry, False), causal_second)

    init = (jnp.full((rows, 1), NEG_BIG, F32), jnp.zeros((rows, 1), F32), jnp.zeros((rows, HEAD_DIM), F32))
    carry = lax.fori_loop(0, n_full // 2, lambda j, c: pair(2 * j, c, False), init)
    _, l_s, acc_s = lax.cond(
        (n_full & 1) == 1,
        lambda c: pair(n_full - 1, c, True),
        lambda c: update(scores(n_full), n_full, c, True),
        carry)
    o_s = acc_s * (1.0 / l_s)

    gt = gates_ref[0, 0]
    outs = []
    for g in range(gqa):
        sl = slice(g * tq, (g + 1) * tq)
        c = g * N_BRANCH
        outs.append(gt[:, c:c + 1] * o_c[sl] + gt[:, c + 1:c + 2] * o_s[sl] + gt[:, c + 2:c + 3] * o_w[sl])
    o_ref[0] = jnp.concatenate(outs, axis=-1).astype(o_ref.dtype)


def _window_bias(tq):
    off = np.minimum(np.arange(WINDOW // tq + 1) * tq, WINDOW)[:, None, None]
    d = off + np.arange(tq)[None, :, None] - np.arange(WINDOW + tq)[None, None, :]
    return jnp.asarray(np.where((d >= 0) & (d < WINDOW), 0.0, NEG_BIG).astype(np.float32))


def _nsa_attention(q, kc, vc, ksa, vs, kw, vw, gates, slope_q, ovt, top_k):
    B, n_heads, T, _ = q.shape
    gqa = n_heads // NSA_KV_HEADS
    ncp = kc.shape[2]
    tq = Q_TILE
    rows = gqa * tq
    kv_spec = lambda n, w: pl.BlockSpec((1, 1, n, w), lambda b, h, i: (b, h, 0, 0))
    return pl.pallas_call(
        functools.partial(_nsa_kernel, top_k=top_k),
        grid=(B, NSA_KV_HEADS, T // tq),
        in_specs=[
            pl.BlockSpec((1, gqa, tq, LANES), lambda b, h, i: (b, h, i, 0)),
            kv_spec(ncp, LANES),
            kv_spec(ncp, HEAD_DIM),
            kv_spec(T, 2 * LANES),
            kv_spec(T, HEAD_DIM),
            kv_spec(T, LANES),
            kv_spec(T, HEAD_DIM),
            pl.BlockSpec((1, 1, tq, LANES), lambda b, h, i: (b, h, i, 0)),
            pl.BlockSpec((1, rows, LANES), lambda b, h, i: (h, 0, 0)),
            pl.BlockSpec((SEL_LANES, 3 * ncp), lambda b, h, i: (0, 0)),
            pl.BlockSpec((1, tq, WINDOW + tq), lambda b, h, i: (jnp.minimum(i, WINDOW // tq), 0, 0)),
        ],
        out_specs=pl.BlockSpec((1, tq, gqa * HEAD_DIM), lambda b, h, i: (b, i, h)),
        out_shape=jax.ShapeDtypeStruct((B, T, n_heads * HEAD_DIM), BF16),
        compiler_params=pltpu.CompilerParams(
            dimension_semantics=("parallel", "parallel", "arbitrary"), vmem_limit_bytes=VMEM_LIMIT),
        name="nsa_attention",
    )(q, kc, vc, ksa, vs, kw, vw, gates, slope_q, ovt, _window_bias(tq))


def _rwkv_prep_kernel(c_ref, cprev_ref, mu_ref, wwa_ref, g2_ref, w0_ref, a0_ref, kk_ref, ka_ref, rk_ref, bd_ref,
                      r_ref, w_ref, k_ref, v_ref, a_ref, b_ref, g_ref, bonus_ref):
    tt = c_ref.shape[1]
    width = r_ref.shape[2]
    n_heads = width // HEAD_DIM
    c = c_ref[0]
    last = jnp.where(pl.program_id(1) > 0, cprev_ref[0][7:8, :], 0.0)
    row = lax.broadcasted_iota(jnp.int32, (tt, 1), 0)
    prev = jnp.where(row == 0, last, pltpu.roll(c, 1, axis=0))
    z = c + (prev - c) * mu_ref[...]
    r = z[:, 0:width]
    k = z[:, width:2 * width]
    v = z[:, 2 * width:3 * width]
    xwa = z[:, 3 * width:3 * width + LANES]
    xg = z[:, 3 * width + LANES:3 * width + 2 * LANES]
    lane = lax.broadcasted_iota(jnp.int32, (1, LANES), 1)
    lora = _dot(jnp.where(lane < DECAY_LORA, jnp.tanh(xwa), xwa).astype(BF16), wwa_ref[...])
    y = w0_ref[...] + lora[:, 0:width]
    softplus_neg = jnp.maximum(-y, 0.0) + jnp.log(1.0 + jnp.exp(-jnp.abs(y)))
    decay = jnp.exp(-jnp.exp(-softplus_neg - 0.5))
    a = _sigmoid(a0_ref[...] + lora[:, width:2 * width])
    g_ref[0] = _dot(_sigmoid(xg).astype(BF16), g2_ref[...])
    kk = k * kk_ref[...]
    k2 = k * (1.0 + (a - 1.0) * ka_ref[...])
    rk = r * k2 * rk_ref[...]
    r_ref[0] = r
    w_ref[0] = decay
    k_ref[0] = k2
    v_ref[0] = v
    kkn = kk * lax.rsqrt(jnp.maximum(_head_sums(kk * kk, bd_ref[...], 2), 1e-24))
    a_ref[0] = -kkn
    b_ref[0] = kkn * a
    bonus_ref[0] = _head_sums(rk, bd_ref[...], 3) * v


def _rwkv_prep(cols, mu, wwa, g2, w0, a0, k_k, k_a, r_k):
    B, T, ncols = cols.shape
    width = w0.shape[1]
    tt = ROW_TILE
    big = pl.BlockSpec((1, tt, width), lambda b, i: (b, i, 0))
    vec = _const_spec((1, width))
    return pl.pallas_call(
        _rwkv_prep_kernel,
        grid=(B, T // tt),
        in_specs=[
            pl.BlockSpec((1, tt, ncols), lambda b, i: (b, i, 0)),
            pl.BlockSpec((1, 8, ncols), lambda b, i: (b, jnp.maximum(i * (tt // 8) - 1, 0), 0)),
            _const_spec((1, ncols)),
            _const_spec((LANES, 2 * width)),
            _const_spec((GATE_LORA, width)),
            vec, vec, vec, vec, vec,
            _const_spec((width, width)),
        ],
        out_specs=(big,) * 8,
        out_shape=(jax.ShapeDtypeStruct((B, T, width), F32),) * 8,
        compiler_params=pltpu.CompilerParams(
            dimension_semantics=("parallel", "parallel"), vmem_limit_bytes=VMEM_LIMIT),
        name="rwkv_prep",
    )(cols, cols, mu, wwa, g2, w0, a0, k_k, k_a, r_k, _head_ones(width))


def _sum_sublanes(u):
    u = u + pltpu.roll(u, 4, axis=0)
    u = u + pltpu.roll(u, 2, axis=0)
    return u + pltpu.roll(u, 1, axis=0)


SCAN_ROWS = 4


def _heads_to_lanes(xa, xb):
    low = lax.broadcasted_iota(jnp.int32, (1, LANES), 1) < HEAD_DIM
    rows = []
    for j in range(xa.shape[1] // LANES):
        va, vb = xa[:, j * LANES:(j + 1) * LANES], xb[:, j * LANES:(j + 1) * LANES]
        rows.append(jnp.where(low, va, pltpu.roll(vb, HEAD_DIM, axis=1)))
        rows.append(jnp.where(low, pltpu.roll(va, HEAD_DIM, axis=1), vb))
    xt = jnp.concatenate(rows + rows, axis=0).T
    return xt[0:HEAD_DIM], xt[HEAD_DIM:2 * HEAD_DIM]


def _wkv_scan_kernel(r_in, w_in, k_in, v_in, a_in, b_in, y_out, state_ref,
                     r_ref, w_ref, k_ref, a_ref, b_ref, v_ref, y_ref):
    tc = r_in.shape[1]
    nkg = HEAD_DIM // 8
    n_rows = HEAD_DIM // 2
    low = lax.broadcasted_iota(jnp.int32, (1, LANES), 1) < HEAD_DIM

    @pl.when(pl.program_id(0) == 0)
    def _():
        state_ref[...] = jnp.zeros_like(state_ref)

    def retile(t, _):
        r_t, w_t = _heads_to_lanes(r_in[:, t, :], w_in[:, t, :])
        k_t, a_t = _heads_to_lanes(k_in[:, t, :], a_in[:, t, :])
        b_t, v_t = _heads_to_lanes(b_in[:, t, :], v_in[:, t, :])
        for dst, val in ((r_ref, r_t), (w_ref, w_t), (k_ref, k_t), (a_ref, a_t), (b_ref, b_t)):
            dst[t] = val.reshape(nkg, 8, LANES)
        v_ref[t] = jnp.where(low, v_t[0:n_rows], v_t[n_rows:2 * n_rows])
        return 0

    lax.fori_loop(0, tc, retile, 0, unroll=4)

    def row_group(gi, _):
        u0 = gi * SCAN_ROWS

        def step(t, state):
            new_state = []
            for j in range(SCAN_ROWS):
                s = state[j * nkg:(j + 1) * nkg]
                v_row = v_ref[t, pl.ds(u0 + j, 1), :]
                sa = _sum_sublanes(_tree_sum([s[g] * a_ref[t, g] for g in range(nkg)]))
                new = [s[g] * w_ref[t, g] + sa * b_ref[t, g] + v_row * k_ref[t, g] for g in range(nkg)]
                y = _sum_sublanes(_tree_sum([new[g] * r_ref[t, g] for g in range(nkg)]))
                y_ref[t, pl.ds(u0 + j, 1), :] = y[0:1]
                new_state.extend(new)
            return tuple(new_state)

        init = tuple(state_ref[u0 + j, g] for j in range(SCAN_ROWS) for g in range(nkg))
        state = lax.fori_loop(0, tc, step, init, unroll=4)
        for j in range(SCAN_ROWS):
            for g in range(nkg):
                state_ref[u0 + j, g] = state[j * nkg + g]
        return 0

    lax.fori_loop(0, n_rows // SCAN_ROWS, row_group, 0)

    def untile(i, _):
        ys = [y_ref[2 * i + d] for d in range(2)]
        m = jnp.concatenate([u for y_t in ys for u in (y_t, pltpu.roll(y_t, HEAD_DIM, axis=1))], axis=0)
        mt = m.T
        first, second = [], []
        for j in range(y_out.shape[2] // LANES):
            even, odd = mt[16 * j:16 * j + 8], mt[16 * j + 8:16 * j + 16]
            first.append(jnp.where(low, even, pltpu.roll(odd, HEAD_DIM, axis=1)))
            second.append(jnp.where(low, pltpu.roll(even, HEAD_DIM, axis=1), odd))
        y_out[:, 2 * i, :] = jnp.concatenate(first, axis=-1)
        y_out[:, 2 * i + 1, :] = jnp.concatenate(second, axis=-1)
        return 0

    lax.fori_loop(0, tc // 2, untile, 0, unroll=4)


def _wkv_scan(r, w, k, v, a, b):
    B, T, width = r.shape
    tc = SCAN_CHUNK
    spec = pl.BlockSpec((B, tc, width), lambda i: (0, i, 0))
    wide = pltpu.VMEM((tc, HEAD_DIM // 8, 8, LANES), F32)
    rows = pltpu.VMEM((tc, HEAD_DIM // 2, LANES), F32)
    return pl.pallas_call(
        _wkv_scan_kernel,
        grid=(T // tc,),
        in_specs=[spec] * 6,
        out_specs=spec,
        out_shape=jax.ShapeDtypeStruct((B, T, width), F32),
        scratch_shapes=[pltpu.VMEM((HEAD_DIM // 2, HEAD_DIM // 8, 8, LANES), F32),
                        wide, wide, wide, wide, wide, rows, rows],
        compiler_params=pltpu.CompilerParams(
            dimension_semantics=("arbitrary",), vmem_limit_bytes=VMEM_LIMIT),
        name="wkv_scan",
    )(r, w, k, v, a, b)


def _out_ffn_kernel(x_ref, ynsa_ref, ys_ref, g_ref, bonus_ref, lnw_ref, lnb_ref, bd_ref, wo_ref, fg_ref, w1_ref,
                    w2_ref, o_ref):
    nsa_w = ynsa_ref.shape[1]
    width = ys_ref.shape[1]
    ys = ys_ref[...]
    dev = ys - _head_sums(ys, bd_ref[...], 3) * (1.0 / HEAD_DIM)
    var = _head_sums(dev * dev, bd_ref[...], 2) * (1.0 / HEAD_DIM)
    yn = dev * lax.rsqrt(var + GN_EPS) * lnw_ref[...] + lnb_ref[...]
    y_rwkv = ((yn + bonus_ref[...]) * g_ref[...]).astype(BF16)
    x1 = x_ref[...] + (_dot(ynsa_ref[...], wo_ref[0:nsa_w, :]) + _dot(y_rwkv, wo_ref[nsa_w:nsa_w + width, :]))
    xn = _rms(x1, fg_ref[...]).astype(BF16)
    d_ff = w1_ref.shape[1]
    chunk = 1024
    ffn = None
    for c in range(d_ff // chunk):
        hid = _dot(xn, w1_ref[:, c * chunk:(c + 1) * chunk])
        hid = jnp.square(jnp.maximum(hid, 0.0)).astype(BF16)
        part = _dot(hid, w2_ref[c * chunk:(c + 1) * chunk, :])
        ffn = part if ffn is None else ffn + part
    o_ref[...] = x1 + ffn


def _out_ffn(x2, ynsa, ys, g, bonus, lnx_w, lnx_b, w_out, ffn_g, w1, w2):
    N, D = x2.shape
    nsa_w = ynsa.shape[1]
    width = ys.shape[1]
    d_ff = w1.shape[1]
    tm = ROW_TILE
    rowspec = lambda w: pl.BlockSpec((tm, w), lambda i: (i, 0))
    return pl.pallas_call(
        _out_ffn_kernel,
        grid=(N // tm,),
        in_specs=[
            rowspec(D), rowspec(nsa_w), rowspec(width), rowspec(width), rowspec(width),
            _const_spec((1, width)), _const_spec((1, width)),
            _const_spec((width, width)),
            _const_spec((nsa_w + width, D)),
            _const_spec((1, D)),
            _const_spec((D, d_ff)),
            _const_spec((d_ff, D)),
        ],
        out_specs=rowspec(D),
        out_shape=jax.ShapeDtypeStruct((N, D), F32),
        compiler_params=pltpu.CompilerParams(
            dimension_semantics=("parallel",), vmem_limit_bytes=VMEM_LIMIT),
        name="out_ffn",
    )(x2, ynsa, ys, g, bonus, lnx_w, lnx_b, _head_ones(width), w_out, ffn_g, w1, w2)


def _alibi_slopes(n):
    start = 2.0 ** (-8.0 / n)
    return (start ** np.arange(1, n + 1)).astype(np.float32)


def _slope_lanes(n_heads):
    m = jnp.asarray(_alibi_slopes(n_heads)) * LOG2E
    pieces = jnp.stack(_split3_bf16(m), axis=-1).astype(F32)
    lanes = jnp.zeros((n_heads, LANES), F32).at[:, HEAD_DIM:HEAD_DIM + 3].set(pieces)
    lanes = lanes.at[:, HEAD_DIM + 3:HEAD_DIM + 6].set(pieces)
    gqa = n_heads // NSA_KV_HEADS
    return jnp.repeat(lanes.reshape(NSA_KV_HEADS, gqa, LANES), Q_TILE, axis=1).astype(BF16)


def _overlap_t(ncp, n_cmp, n_sel):
    ci = np.arange(ncp)[None, :] * CMP_STRIDE
    sj = np.arange(SEL_LANES)[:, None] * SEL_BLOCK
    ov = (ci <= sj + SEL_BLOCK - 1) & (ci + CMP_BLOCK - 1 >= sj)
    ov &= (np.arange(ncp)[None, :] < n_cmp) & (np.arange(SEL_LANES)[:, None] < n_sel)
    return jnp.asarray(np.tile(ov.astype(np.float32), (1, 3)), dtype=BF16)


def _compress_weights(k_pos, k_w1, k_w2, v_pos, v_w1, v_w2):
    half = CMP_BLOCK // 2
    groups = 2 * NSA_KV_HEADS

    eye = jnp.eye(groups, dtype=F32)
    per_group = lambda k_part, v_part: jnp.stack([k_part] * NSA_KV_HEADS + [v_part] * NSA_KV_HEADS)

    def first_layer(lo):
        w1 = per_group(k_w1, v_w1).reshape(groups, CMP_BLOCK, HEAD_DIM, CMP_HIDDEN)[:, lo:lo + half]
        w = w1.transpose(1, 0, 2, 3)[:, :, :, None, :] * eye[None, :, None, :, None]
        return w.reshape(half * groups * HEAD_DIM, groups * CMP_HIDDEN).astype(BF16)

    def pos_row(lo):
        return per_group(k_pos, v_pos)[:, lo:lo + half].transpose(1, 0, 2).reshape(1, half * groups * HEAD_DIM)

    w2 = per_group(k_w2, v_w2)[:, :, None, :] * eye[:, None, :, None]
    w2 = w2.reshape(groups * CMP_HIDDEN, groups * HEAD_DIM).astype(BF16)
    return first_layer(0), first_layer(half), pos_row(0), pos_row(half), w2


def _layer(x, ln_mix_g, w_in, nsa_gate_b, q_norm_g, kc_norm_g, ks_norm_g, kw_norm_g,
           cmp_k_pos, cmp_k_w1, cmp_k_w2, cmp_v_pos, cmp_v_w1, cmp_v_w2,
           rwkv_mu, rwkv_w0, rwkv_w2, rwkv_a0, rwkv_a2, rwkv_g2, rwkv_k_k, rwkv_k_a, rwkv_r_k,
           rwkv_lnx_w, rwkv_lnx_b, w_out, ln_ffn_g, w_ff1, w_ff2):
    B, T, D = x.shape
    nsa_w = D // 2
    n_heads = nsa_w // HEAD_DIM
    gqa = n_heads // NSA_KV_HEADS
    kvw = NSA_KV_HEADS * HEAD_DIM
    rw_w = D - nsa_w
    rw_heads = rw_w // HEAD_DIM
    n_sel = T // SEL_BLOCK
    ncp = T // CMP_STRIDE
    n_cmp = (T - CMP_BLOCK) // CMP_STRIDE + 1
    top_k = min(SEL_TOPK, n_sel)
    assert T % K_TILE == 0 and T % ROW_TILE == 0 and n_sel <= SEL_LANES and n_cmp == ncp - 1
    assert B * rw_heads * 2 == LANES and gqa * N_BRANCH <= LANES and T > WINDOW
    assert DECAY_LORA + AAA_LORA == LANES and GATE_LORA == LANES

    row2 = lambda u: u.reshape(1, -1)
    nsa_main = nsa_w + 6 * kvw
    gl = w_in[:, nsa_main:nsa_main + n_heads * N_BRANCH].reshape(D, NSA_KV_HEADS, gqa * N_BRANCH)
    gl = jnp.pad(gl, ((0, 0), (0, 0), (0, LANES - gqa * N_BRANCH))).reshape(D, NSA_KV_HEADS * LANES)
    w_nsa = jnp.concatenate([w_in[:, :nsa_main], gl], axis=1).astype(BF16)
    w_rwkv = w_in[:, nsa_main + n_heads * N_BRANCH:].astype(BF16)
    gate_b2 = jnp.pad(nsa_gate_b.reshape(NSA_KV_HEADS, gqa * N_BRANCH),
                      ((0, 0), (0, LANES - gqa * N_BRANCH))).reshape(1, NSA_KV_HEADS * LANES)

    q, kvc, ksa, vs, kw, vw, gates, rw_cols = _inproj(
        x, row2(ln_mix_g), w_nsa, w_rwkv, jnp.tile(row2(q_norm_g), (1, n_heads)),
        jnp.tile(row2(ks_norm_g), (1, NSA_KV_HEADS)), jnp.tile(row2(kw_norm_g), (1, NSA_KV_HEADS)), gate_b2)

    wa, wb, pa, pb, cw2 = _compress_weights(cmp_k_pos, cmp_k_w1, cmp_k_w2, cmp_v_pos, cmp_v_w1, cmp_v_w2)
    kc, vc = _compress(kvc, wa, wb, pa, pb, cw2, row2(kc_norm_g))
    y_nsa = _nsa_attention(q, kc, vc, ksa, vs, kw, vw, gates, _slope_lanes(n_heads),
                           _overlap_t(ncp, n_cmp, n_sel), top_k)

    wwa = jnp.zeros((LANES, 2 * rw_w), F32)
    wwa = wwa.at[:DECAY_LORA, :rw_w].set(rwkv_w2).at[DECAY_LORA:, rw_w:].set(rwkv_a2).astype(BF16)
    r, w, k, v, a, b, g, bonus = _rwkv_prep(
        rw_cols, row2(rwkv_mu), wwa, rwkv_g2.astype(BF16), row2(rwkv_w0), row2(rwkv_a0),
        row2(rwkv_k_k), row2(rwkv_k_a), row2(rwkv_r_k))
    ys = _wkv_scan(r, w, k, v, a, b)

    N = B * T
    out = _out_ffn(x.reshape(N, D), y_nsa.reshape(N, nsa_w), ys.reshape(N, rw_w), g.reshape(N, rw_w),
                   bonus.reshape(N, rw_w), row2(rwkv_lnx_w), row2(rwkv_lnx_b), w_out.astype(BF16),
                   row2(ln_ffn_g), w_ff1.astype(BF16), w_ff2.astype(BF16))
    return out.reshape(B, T, D)


def kernel(x, ln_mix_g, w_in, nsa_gate_b, q_norm_g, kc_norm_g, ks_norm_g, kw_norm_g, cmp_k_pos, cmp_k_w1, cmp_k_w2, cmp_v_pos, cmp_v_w1, cmp_v_w2, rwkv_mu, rwkv_w0, rwkv_w2, rwkv_a0, rwkv_a2, rwkv_g2, rwkv_k_k, rwkv_k_a, rwkv_r_k, rwkv_lnx_w, rwkv_lnx_b, w_out, ln_ffn_g, w_ff1, w_ff2):
    params = (ln_mix_g, w_in, nsa_gate_b, q_norm_g, kc_norm_g, ks_norm_g, kw_norm_g, cmp_k_pos, cmp_k_w1,
              cmp_k_w2, cmp_v_pos, cmp_v_w1, cmp_v_w2, rwkv_mu, rwkv_w0, rwkv_w2, rwkv_a0, rwkv_a2, rwkv_g2,
              rwkv_k_k, rwkv_k_a, rwkv_r_k, rwkv_lnx_w, rwkv_lnx_b, w_out, ln_ffn_g, w_ff1, w_ff2)
    for layer in range(ln_mix_g.shape[0]):
        x = _layer(x, *(p[layer] for p in params))
    return x
```

```python
import functools

import numpy as np
import jax
import jax.numpy as jnp
from jax import lax
from jax.experimental import pallas as pl
from jax.experimental.pallas import tpu as pltpu

F32 = jnp.float32
BF16 = jnp.bfloat16

HEAD_DIM = 64
NSA_KV_HEADS = 2
CMP_BLOCK = 32
CMP_STRIDE = 16
CMP_HIDDEN = 2 * HEAD_DIM
SEL_BLOCK = 64
SEL_SHIFT = 6
SEL_TOPK = 16
WINDOW = 512
N_BRANCH = 3
DECAY_LORA = 64
AAA_LORA = 64
GATE_LORA = 128
RMS_EPS = 1e-6
GN_EPS = HEAD_DIM * 1e-5
NEG_BIG = -1e30
FORCED_SCORE = 1e6
LOG2E = 1.4426950408889634

LANES = 128
SEL_LANES = 64
MASK_BIG = 2.0 ** 60
VMEM_LIMIT = 56 * 1024 * 1024

ROW_TILE = 512
Q_TILE = 256
K_TILE = 512
SCAN_CHUNK = 64


def _rms(u, g):
    return u * lax.rsqrt(jnp.mean(u * u, axis=-1, keepdims=True) + RMS_EPS) * g


def _sigmoid(u):
    return 1.0 / (1.0 + jnp.exp(-u))


def _dot(a, b):
    return jnp.dot(a, b, preferred_element_type=F32)


def _dot_nt(a, b):
    return lax.dot_general(a, b, (((1,), (1,)), ((), ())), preferred_element_type=F32)


def _const_spec(shape):
    nd = len(shape)
    return pl.BlockSpec(shape, lambda *_: (0,) * nd, pipeline_mode=pl.Buffered(1))


def _split_bf16(u, n_pieces):
    pieces, rest = [], u
    for i in range(n_pieces):
        piece = rest.astype(BF16)
        pieces.append(piece)
        if i + 1 < n_pieces:
            rest = rest - piece.astype(F32)
    return pieces


def _head_sums(u, head_ones, n_pieces):
    return _tree_sum([_dot(piece, head_ones) for piece in _split_bf16(u, n_pieces)])


def _head_ones(n_lanes):
    head_id = np.arange(n_lanes) // HEAD_DIM
    return jnp.asarray((head_id[:, None] == head_id[None, :]).astype(np.float32), dtype=BF16)


def _tree_sum(terms):
    while len(terms) > 1:
        nxt = [terms[i] + terms[i + 1] for i in range(0, len(terms) - 1, 2)]
        if len(terms) % 2:
            nxt.append(terms[-1])
        terms = nxt
    return terms[0]


def _inproj_kernel(x_ref, g_ref, wn_ref, wr_ref, qg_ref, ksg_ref, kwg_ref, gb_ref, bd_ref,
                   q_ref, kvc_ref, ksa_ref, vs_ref, kw_ref, vw_ref, gates_ref, rw_ref):
    tm = x_ref.shape[1]
    n_q_heads = q_ref.shape[1]
    nsa_w = n_q_heads * HEAD_DIM
    kvw = NSA_KV_HEADS * HEAD_DIM
    xn = _rms(x_ref[0], g_ref[...]).astype(BF16)
    pn = _dot(xn, wn_ref[...])
    rw_ref[0] = _dot(xn, wr_ref[...])

    def head_rms(u, gain):
        ss = _head_sums(u * u, bd_ref[0:u.shape[1], 0:u.shape[1]], 2)
        return u * lax.rsqrt(ss * (1.0 / HEAD_DIM) + RMS_EPS) * gain

    zeros_hd = jnp.zeros((tm, HEAD_DIM), F32)
    ones_hd = jnp.ones((tm, HEAD_DIM), F32)
    qn = head_rms(pn[:, 0:nsa_w], qg_ref[...]) * (HEAD_DIM ** -0.5 * LOG2E)
    for h in range(n_q_heads):
        q_ref[0, h] = jnp.concatenate([qn[:, h * HEAD_DIM:(h + 1) * HEAD_DIM], zeros_hd], axis=-1).astype(BF16)

    kvc_ref[0, 0] = pn[:, nsa_w:nsa_w + kvw]
    kvc_ref[0, 1] = pn[:, nsa_w + kvw:nsa_w + 2 * kvw]

    tok = pl.program_id(1) * tm + lax.broadcasted_iota(jnp.int32, (tm, 1), 0)
    blk = lax.shift_right_logical(tok, SEL_SHIFT)
    onehot = jnp.where(blk == lax.broadcasted_iota(jnp.int32, (1, SEL_LANES), 1), MASK_BIG, 0.0)
    pos = _pos_tile(tok)
    off = nsa_w + 2 * kvw
    ksn = head_rms(pn[:, off:off + kvw], ksg_ref[...])
    kwn = head_rms(pn[:, off + 2 * kvw:off + 3 * kvw], kwg_ref[...])
    for h in range(NSA_KV_HEADS):
        ksa_ref[0, h] = jnp.concatenate([ksn[:, h * HEAD_DIM:(h + 1) * HEAD_DIM], onehot, pos], axis=-1).astype(BF16)
        o2 = off + kvw
        vs_ref[0, h] = jnp.concatenate([pn[:, o2 + h * HEAD_DIM:o2 + (h + 1) * HEAD_DIM], ones_hd], axis=-1).astype(BF16)
        kw_ref[0, h] = (jnp.concatenate([kwn[:, h * HEAD_DIM:(h + 1) * HEAD_DIM], zeros_hd], axis=-1) + pos).astype(BF16)
        o4 = off + 3 * kvw
        vw_ref[0, h] = jnp.concatenate([pn[:, o4 + h * HEAD_DIM:o4 + (h + 1) * HEAD_DIM], ones_hd], axis=-1).astype(BF16)
        o5 = off + 4 * kvw + h * LANES
        gates_ref[0, h] = _sigmoid(pn[:, o5:o5 + LANES] + gb_ref[:, h * LANES:(h + 1) * LANES])


def _inproj(x, ln_g, w_nsa, w_rwkv, q_g, ks_g, kw_g, gate_b2):
    B, T, D = x.shape
    nsa_cols = w_nsa.shape[1]
    rw_cols = w_rwkv.shape[1]
    n_q_heads = (nsa_cols - 2 * LANES - 6 * NSA_KV_HEADS * HEAD_DIM) // HEAD_DIM
    tm = ROW_TILE
    kvw = NSA_KV_HEADS * HEAD_DIM
    head_ones = _head_ones(n_q_heads * HEAD_DIM)
    out_shape = (
        jax.ShapeDtypeStruct((B, n_q_heads, T, LANES), BF16),
        jax.ShapeDtypeStruct((B, 2, T, kvw), F32),
        jax.ShapeDtypeStruct((B, NSA_KV_HEADS, T, 2 * LANES), BF16),
        jax.ShapeDtypeStruct((B, NSA_KV_HEADS, T, LANES), BF16),
        jax.ShapeDtypeStruct((B, NSA_KV_HEADS, T, LANES), BF16),
        jax.ShapeDtypeStruct((B, NSA_KV_HEADS, T, LANES), BF16),
        jax.ShapeDtypeStruct((B, NSA_KV_HEADS, T, LANES), F32),
        jax.ShapeDtypeStruct((B, T, rw_cols), F32),
    )
    head_spec = lambda n, w: pl.BlockSpec((1, n, tm, w), lambda b, i: (b, 0, i, 0))
    return pl.pallas_call(
        _inproj_kernel,
        grid=(B, T // tm),
        in_specs=[
            pl.BlockSpec((1, tm, D), lambda b, i: (b, i, 0)),
            _const_spec((1, D)),
            _const_spec((D, nsa_cols)),
            _const_spec((D, rw_cols)),
            _const_spec((1, n_q_heads * HEAD_DIM)),
            _const_spec((1, kvw)),
            _const_spec((1, kvw)),
            _const_spec((1, NSA_KV_HEADS * LANES)),
            _const_spec((n_q_heads * HEAD_DIM, n_q_heads * HEAD_DIM)),
        ],
        out_specs=(
            head_spec(n_q_heads, LANES),
            head_spec(2, kvw),
            head_spec(NSA_KV_HEADS, 2 * LANES),
            head_spec(NSA_KV_HEADS, LANES),
            head_spec(NSA_KV_HEADS, LANES),
            head_spec(NSA_KV_HEADS, LANES),
            head_spec(NSA_KV_HEADS, LANES),
            pl.BlockSpec((1, tm, rw_cols), lambda b, i: (b, i, 0)),
        ),
        out_shape=out_shape,
        compiler_params=pltpu.CompilerParams(
            dimension_semantics=("parallel", "parallel"), vmem_limit_bytes=VMEM_LIMIT),
        name="inproj",
    )(x, ln_g, w_nsa, w_rwkv, q_g, ks_g, kw_g, gate_b2, head_ones)


def _gelu_tanh(u):
    return 0.5 * u * (1.0 + jnp.tanh(np.sqrt(2.0 / np.pi).astype(np.float32) * (u + 0.044715 * (u * u * u))))


def _compress_kernel(kv_ref, wa_ref, wb_ref, pa_ref, pb_ref, w2_ref, kcg_ref, kc_ref, vc_ref):
    width = 2 * kv_ref.shape[3]
    ncp = kv_ref.shape[2] // CMP_STRIDE
    firsts, seconds = [], []
    for l in range(CMP_STRIDE):
        tok = jnp.concatenate([kv_ref[0, p, pl.ds(l, ncp, stride=CMP_STRIDE), :] for p in range(2)], axis=-1)
        cols = slice(l * width, (l + 1) * width)
        firsts.append(_dot((tok + pa_ref[:, cols]).astype(BF16), wa_ref[cols, :]))
        seconds.append(_dot((tok + pb_ref[:, cols]).astype(BF16), wb_ref[cols, :]))
    first, second = _tree_sum(firsts), _tree_sum(seconds)
    row = lax.broadcasted_iota(jnp.int32, (ncp, 1), 0)
    real = row < ncp - 1
    second = jnp.where(real, pltpu.roll(second, ncp - 1, axis=0), 0.0)
    hid = _gelu_tanh(first + second)
    out = jnp.where(real, _dot(hid.astype(BF16), w2_ref[...]), 0.0)
    zeros_hd = jnp.zeros((ncp, HEAD_DIM), F32)
    pos = _pos_tile(row * CMP_STRIDE + (CMP_BLOCK - 1))
    for h in range(NSA_KV_HEADS):
        kch = _rms(out[:, h * HEAD_DIM:(h + 1) * HEAD_DIM], kcg_ref[...])
        kc_ref[0, h] = (jnp.concatenate([kch, zeros_hd], axis=-1) + pos).astype(BF16)
        o = (NSA_KV_HEADS + h) * HEAD_DIM
        vc_ref[0, h] = out[:, o:o + HEAD_DIM].astype(BF16)


def _compress(kvc, wa, wb, pa, pb, w2, kc_g):
    B, _, T, kv_cols = kvc.shape
    ncp = T // CMP_STRIDE
    width, hid = wa.shape
    return pl.pallas_call(
        _compress_kernel,
        grid=(B,),
        in_specs=[
            pl.BlockSpec((1, 2, T, kv_cols), lambda b: (b, 0, 0, 0)),
            _const_spec((width, hid)),
            _const_spec((width, hid)),
            _const_spec((1, width)),
            _const_spec((1, width)),
            _const_spec((hid, 2 * NSA_KV_HEADS * HEAD_DIM)),
            _const_spec((1, HEAD_DIM)),
        ],
        out_specs=(
            pl.BlockSpec((1, NSA_KV_HEADS, ncp, LANES), lambda b: (b, 0, 0, 0)),
            pl.BlockSpec((1, NSA_KV_HEADS, ncp, HEAD_DIM), lambda b: (b, 0, 0, 0)),
        ),
        out_shape=(
            jax.ShapeDtypeStruct((B, NSA_KV_HEADS, ncp, LANES), BF16),
            jax.ShapeDtypeStruct((B, NSA_KV_HEADS, ncp, HEAD_DIM), BF16),
        ),
        compiler_params=pltpu.CompilerParams(
            dimension_semantics=("parallel",), vmem_limit_bytes=VMEM_LIMIT),
        name="compress",
    )(kvc, wa, wb, pa, pb, w2, kc_g)


def _split3_bf16(u):
    hi = u.astype(BF16)
    r1 = u - hi.astype(F32)
    mid = r1.astype(BF16)
    lo = (r1 - mid.astype(F32)).astype(BF16)
    return hi, mid, lo


def _pos_tile(pos):
    lane = lax.broadcasted_iota(jnp.int32, (1, LANES), 1)
    hi = lax.shift_left(lax.shift_right_logical(pos, SEL_SHIFT), SEL_SHIFT).astype(F32)
    lo = (pos & (SEL_BLOCK - 1)).astype(F32)
    first = HEAD_DIM
    return jnp.where((lane >= first) & (lane < first + 3), hi,
                     jnp.where((lane >= first + 3) & (lane < first + 6), lo, 0.0))


def _nsa_kernel(q_ref, kc_ref, vc_ref, ksa_ref, vs_ref, kw_ref, vw_ref, gates_ref, slopeq_ref, ovt_ref, wbias_ref,
                o_ref, *, top_k):
    gqa, tq = q_ref.shape[1], q_ref.shape[2]
    ncp = kc_ref.shape[2]
    rows = gqa * tq
    q0 = pl.program_id(2) * tq
    q = q_ref[0].reshape(rows, LANES)
    slope_q = slopeq_ref[0]
    q_pos = q + slope_q
    row = lax.broadcasted_iota(jnp.int32, (rows, 1), 0)
    t_rows = q0 + (row & (tq - 1))

    cend = lax.broadcasted_iota(jnp.int32, (1, ncp), 1) * CMP_STRIDE + (CMP_BLOCK - 1)
    valid_c = t_rows >= cend
    s_c = jnp.where(valid_c, _dot_nt(q_pos, kc_ref[0, 0]), NEG_BIG)
    e_c = jnp.where(valid_c, jnp.exp2(s_c - jnp.max(s_c, axis=-1, keepdims=True)), 0.0)
    l_c = jnp.sum(e_c, axis=-1, keepdims=True)
    p_c = e_c * (1.0 / jnp.where(l_c > 0.0, l_c, 1.0))
    o_c = _dot(p_c.astype(BF16), vc_ref[0, 0])

    wk = WINDOW + tq
    start = pl.multiple_of(jnp.maximum(q0 - WINDOW, 0), tq)
    s_w = _dot_nt(q_pos, kw_ref[0, 0, pl.ds(start, wk), :]).reshape(gqa, tq, wk) + wbias_ref[...]
    s_w = s_w.reshape(rows, wk)
    e_w = jnp.exp2(s_w - jnp.max(s_w, axis=-1, keepdims=True))
    pv_w = _dot(e_w.astype(BF16), vw_ref[0, 0, pl.ds(start, wk), :])
    o_w = pv_w[:, 0:HEAD_DIM] * (1.0 / pv_w[:, HEAD_DIM:HEAD_DIM + 1])

    p_sum = p_c[0:tq]
    for g in range(1, gqa):
        p_sum = p_sum + p_c[g * tq:(g + 1) * tq]
    imp_t = _dot_nt(ovt_ref[...], jnp.concatenate(_split3_bf16(p_sum), axis=-1))

    jb = lax.broadcasted_iota(jnp.int32, (SEL_LANES, 1), 0)
    jb_f = jb.astype(F32)
    cur = lax.shift_right_logical(q0 + lax.broadcasted_iota(jnp.int32, (1, tq), 1), SEL_SHIFT)
    forced = (jb == 0) | (jb == cur) | (jb == cur - 1)
    work = jnp.where(forced, FORCED_SCORE, jnp.where(jb > cur, NEG_BIG, imp_t))
    sel_t = jnp.zeros((SEL_LANES, tq), F32)
    for _ in range(top_k):
        mx = jnp.max(work, axis=0, keepdims=True)
        idx = jnp.min(jnp.where(work == mx, jb_f, float(SEL_LANES)), axis=0, keepdims=True)
        pick = jb_f == idx
        sel_t = jnp.where(pick, 1.0, sel_t)
        work = jnp.where(pick, -jnp.inf, work)
    pad_t = jnp.concatenate([jnp.zeros((SEL_LANES, tq), F32), sel_t - 1.0], axis=0)
    sel_q = pad_t.T
    q_sel = (q.astype(F32) + jnp.concatenate([sel_q] * gqa, axis=0)).astype(BF16)
    q_aug = jnp.concatenate([q_sel, slope_q], axis=-1)

    tk = K_TILE
    n_full = q0 // tk

    def scores(kt):
        return _dot_nt(q_aug, ksa_ref[0, 0, pl.ds(pl.multiple_of(kt * tk, tk), tk), :])

    def update(s, kt, carry, causal):
        m_run, acc = carry
        k0 = pl.multiple_of(kt * tk, tk)
        if causal:
            s = jnp.where(lax.broadcasted_iota(jnp.int32, (1, tk), 1) <= t_rows - k0, s, NEG_BIG)
        m_new = jnp.maximum(m_run, jnp.max(s, axis=-1, keepdims=True))
        p = jnp.exp2(s - m_new)
        return m_new, jnp.exp2(m_run - m_new) * acc + _dot(p.astype(BF16), vs_ref[0, 0, pl.ds(k0, tk), :])

    def pair(kt, carry, causal_second):
        s_a, s_b = scores(kt), scores(kt + 1)
        return update(s_b, kt + 1, update(s_a, kt, carry, False), causal_second)

    init = (jnp.full((rows, 1), NEG_BIG, F32), jnp.zeros((rows, LANES), F32))
    carry = lax.fori_loop(0, n_full // 2, lambda j, c: pair(2 * j, c, False), init)
    _, acc_s = lax.cond(
        (n_full & 1) == 1,
        lambda c: pair(n_full - 1, c, True),
        lambda c: update(scores(n_full), n_full, c, True),
        carry)
    o_s = acc_s[:, 0:HEAD_DIM] * (1.0 / acc_s[:, HEAD_DIM:HEAD_DIM + 1])

    gt = gates_ref[0, 0]
    outs = []
    for g in range(gqa):
        sl = slice(g * tq, (g + 1) * tq)
        c = g * N_BRANCH
        outs.append(gt[:, c:c + 1] * o_c[sl] + gt[:, c + 1:c + 2] * o_s[sl] + gt[:, c + 2:c + 3] * o_w[sl])
    o_ref[0] = jnp.concatenate(outs, axis=-1).astype(o_ref.dtype)


def _window_bias(tq):
    off = np.minimum(np.arange(WINDOW // tq + 1) * tq, WINDOW)[:, None, None]
    d = off + np.arange(tq)[None, :, None] - np.arange(WINDOW + tq)[None, None, :]
    return jnp.asarray(np.where((d >= 0) & (d < WINDOW), 0.0, NEG_BIG).astype(np.float32))


def _nsa_attention(q, kc, vc, ksa, vs, kw, vw, gates, slope_q, ovt, top_k):
    B, n_heads, T, _ = q.shape
    gqa = n_heads // NSA_KV_HEADS
    ncp = kc.shape[2]
    tq = Q_TILE
    rows = gqa * tq
    kv_spec = lambda n, w: pl.BlockSpec((1, 1, n, w), lambda b, h, i: (b, h, 0, 0))
    return pl.pallas_call(
        functools.partial(_nsa_kernel, top_k=top_k),
        grid=(B, NSA_KV_HEADS, T // tq),
        in_specs=[
            pl.BlockSpec((1, gqa, tq, LANES), lambda b, h, i: (b, h, i, 0)),
            kv_spec(ncp, LANES),
            kv_spec(ncp, HEAD_DIM),
            kv_spec(T, 2 * LANES),
            kv_spec(T, LANES),
            kv_spec(T, LANES),
            kv_spec(T, LANES),
            pl.BlockSpec((1, 1, tq, LANES), lambda b, h, i: (b, h, i, 0)),
            pl.BlockSpec((1, rows, LANES), lambda b, h, i: (h, 0, 0)),
            pl.BlockSpec((SEL_LANES, 3 * ncp), lambda b, h, i: (0, 0)),
            pl.BlockSpec((1, tq, WINDOW + tq), lambda b, h, i: (jnp.minimum(i, WINDOW // tq), 0, 0)),
        ],
        out_specs=pl.BlockSpec((1, tq, gqa * HEAD_DIM), lambda b, h, i: (b, i, h)),
        out_shape=jax.ShapeDtypeStruct((B, T, n_heads * HEAD_DIM), BF16),
        compiler_params=pltpu.CompilerParams(
            dimension_semantics=("parallel", "parallel", "arbitrary"), vmem_limit_bytes=VMEM_LIMIT),
        name="nsa_attention",
    )(q, kc, vc, ksa, vs, kw, vw, gates, slope_q, ovt, _window_bias(tq))


def _rwkv_prep_kernel(c_ref, cprev_ref, mu_ref, wwa_ref, g2_ref, w0_ref, a0_ref, kk_ref, ka_ref, rk_ref, bd_ref,
                      r_ref, w_ref, k_ref, v_ref, a_ref, b_ref, g_ref, bonus_ref):
    tt = c_ref.shape[1]
    width = r_ref.shape[2]
    c = c_ref[0]
    last = jnp.where(pl.program_id(1) > 0, cprev_ref[0][7:8, :], 0.0)
    row = lax.broadcasted_iota(jnp.int32, (tt, 1), 0)
    prev = jnp.where(row == 0, last, pltpu.roll(c, 1, axis=0))
    z = c + (prev - c) * mu_ref[...]
    r = z[:, 0:width]
    k = z[:, width:2 * width]
    v = z[:, 2 * width:3 * width]
    xwa = z[:, 3 * width:3 * width + LANES]
    xg = z[:, 3 * width + LANES:3 * width + 2 * LANES]
    lane = lax.broadcasted_iota(jnp.int32, (1, LANES), 1)
    lora = _dot(jnp.where(lane < DECAY_LORA, jnp.tanh(xwa), xwa).astype(BF16), wwa_ref[...])
    y = w0_ref[...] + lora[:, 0:width]
    softplus_neg = jnp.maximum(-y, 0.0) + jnp.log(1.0 + jnp.exp(-jnp.abs(y)))
    decay = jnp.exp(-jnp.exp(-softplus_neg - 0.5))
    a = _sigmoid(a0_ref[...] + lora[:, width:2 * width])
    g_ref[0] = _dot(_sigmoid(xg).astype(BF16), g2_ref[...])
    kk = k * kk_ref[...]
    k2 = k * (1.0 + (a - 1.0) * ka_ref[...])
    rk = r * k2 * rk_ref[...]
    r_ref[0] = r
    w_ref[0] = decay
    k_ref[0] = k2
    v_ref[0] = v
    kkn = kk * lax.rsqrt(jnp.maximum(_head_sums(kk * kk, bd_ref[...], 2), 1e-24))
    a_ref[0] = -kkn
    b_ref[0] = kkn * a
    bonus_ref[0] = _head_sums(rk, bd_ref[...], 3) * v


def _rwkv_prep(cols, mu, wwa, g2, w0, a0, k_k, k_a, r_k):
    B, T, ncols = cols.shape
    width = w0.shape[1]
    tt = ROW_TILE
    big = pl.BlockSpec((1, tt, width), lambda b, i: (b, i, 0))
    vec = _const_spec((1, width))
    return pl.pallas_call(
        _rwkv_prep_kernel,
        grid=(B, T // tt),
        in_specs=[
            pl.BlockSpec((1, tt, ncols), lambda b, i: (b, i, 0)),
            pl.BlockSpec((1, 8, ncols), lambda b, i: (b, jnp.maximum(i * (tt // 8) - 1, 0), 0)),
            _const_spec((1, ncols)),
            _const_spec((LANES, 2 * width)),
            _const_spec((GATE_LORA, width)),
            vec, vec, vec, vec, vec,
            _const_spec((width, width)),
        ],
        out_specs=(big,) * 8,
        out_shape=(jax.ShapeDtypeStruct((B, T, width), F32),) * 8,
        compiler_params=pltpu.CompilerParams(
            dimension_semantics=("parallel", "parallel"), vmem_limit_bytes=VMEM_LIMIT),
        name="rwkv_prep",
    )(cols, cols, mu, wwa, g2, w0, a0, k_k, k_a, r_k, _head_ones(width))


def _sum_sublanes(u):
    u = u + pltpu.roll(u, 4, axis=0)
    u = u + pltpu.roll(u, 2, axis=0)
    return u + pltpu.roll(u, 1, axis=0)


SCAN_ROWS = 4


def _heads_to_lanes(xa, xb):
    low = lax.broadcasted_iota(jnp.int32, (1, LANES), 1) < HEAD_DIM
    rows = []
    for j in range(xa.shape[1] // LANES):
        va, vb = xa[:, j * LANES:(j + 1) * LANES], xb[:, j * LANES:(j + 1) * LANES]
        rows.append(jnp.where(low, va, pltpu.roll(vb, HEAD_DIM, axis=1)))
        rows.append(jnp.where(low, pltpu.roll(va, HEAD_DIM, axis=1), vb))
    xt = jnp.concatenate(rows + rows, axis=0).T
    return xt[0:HEAD_DIM], xt[HEAD_DIM:2 * HEAD_DIM]


def _wkv_scan_kernel(r_in, w_in, k_in, v_in, a_in, b_in, y_out, state_ref,
                     r_ref, w_ref, k_ref, a_ref, b_ref, v_ref, y_ref):
    tc = r_in.shape[1]
    nkg = HEAD_DIM // 8
    n_rows = HEAD_DIM // 2
    low = lax.broadcasted_iota(jnp.int32, (1, LANES), 1) < HEAD_DIM

    @pl.when(pl.program_id(0) == 0)
    def _():
        state_ref[...] = jnp.zeros_like(state_ref)

    def retile(t, _):
        r_t, w_t = _heads_to_lanes(r_in[:, t, :], w_in[:, t, :])
        k_t, a_t = _heads_to_lanes(k_in[:, t, :], a_in[:, t, :])
        b_t, v_t = _heads_to_lanes(b_in[:, t, :], v_in[:, t, :])
        for dst, val in ((r_ref, r_t), (w_ref, w_t), (k_ref, k_t), (a_ref, a_t), (b_ref, b_t)):
            dst[t] = val.reshape(nkg, 8, LANES)
        v_ref[t] = jnp.where(low, v_t[0:n_rows], v_t[n_rows:2 * n_rows])
        return 0

    lax.fori_loop(0, tc, retile, 0, unroll=4)

    def row_group(gi, _):
        u0 = gi * SCAN_ROWS

        def step(t, state):
            new_state = []
            for j in range(SCAN_ROWS):
                s = state[j * nkg:(j + 1) * nkg]
                v_row = v_ref[t, pl.ds(u0 + j, 1), :]
                sa = _sum_sublanes(_tree_sum([s[g] * a_ref[t, g] for g in range(nkg)]))
                new = [s[g] * w_ref[t, g] + sa * b_ref[t, g] + v_row * k_ref[t, g] for g in range(nkg)]
                y = _sum_sublanes(_tree_sum([new[g] * r_ref[t, g] for g in range(nkg)]))
                y_ref[t, pl.ds(u0 + j, 1), :] = y[0:1]
                new_state.extend(new)
            return tuple(new_state)

        init = tuple(state_ref[u0 + j, g] for j in range(SCAN_ROWS) for g in range(nkg))
        state = lax.fori_loop(0, tc, step, init, unroll=4)
        for j in range(SCAN_ROWS):
            for g in range(nkg):
                state_ref[u0 + j, g] = state[j * nkg + g]
        return 0

    lax.fori_loop(0, n_rows // SCAN_ROWS, row_group, 0)

    def untile(i, _):
        ys = [y_ref[2 * i + d] for d in range(2)]
        m = jnp.concatenate([u for y_t in ys for u in (y_t, pltpu.roll(y_t, HEAD_DIM, axis=1))], axis=0)
        mt = m.T
        first, second = [], []
        for j in range(y_out.shape[2] // LANES):
            even, odd = mt[16 * j:16 * j + 8], mt[16 * j + 8:16 * j + 16]
            first.append(jnp.where(low, even, pltpu.roll(odd, HEAD_DIM, axis=1)))
            second.append(jnp.where(low, pltpu.roll(even, HEAD_DIM, axis=1), odd))
        y_out[:, 2 * i, :] = jnp.concatenate(first, axis=-1)
        y_out[:, 2 * i + 1, :] = jnp.concatenate(second, axis=-1)
        return 0

    lax.fori_loop(0, tc // 2, untile, 0, unroll=4)


def _wkv_scan(r, w, k, v, a, b):
    B, T, width = r.shape
    tc = SCAN_CHUNK
    spec = pl.BlockSpec((B, tc, width), lambda i: (0, i, 0))
    wide = pltpu.VMEM((tc, HEAD_DIM // 8, 8, LANES), F32)
    rows = pltpu.VMEM((tc, HEAD_DIM // 2, LANES), F32)
    return pl.pallas_call(
        _wkv_scan_kernel,
        grid=(T // tc,),
        in_specs=[spec] * 6,
        out_specs=spec,
        out_shape=jax.ShapeDtypeStruct((B, T, width), F32),
        scratch_shapes=[pltpu.VMEM((HEAD_DIM // 2, HEAD_DIM // 8, 8, LANES), F32),
                        wide, wide, wide, wide, wide, rows, rows],
        compiler_params=pltpu.CompilerParams(
            dimension_semantics=("arbitrary",), vmem_limit_bytes=VMEM_LIMIT),
        name="wkv_scan",
    )(r, w, k, v, a, b)


def _out_ffn_kernel(x_ref, ynsa_ref, ys_ref, g_ref, bonus_ref, lnw_ref, lnb_ref, bd_ref, wo_ref, fg_ref, w1_ref,
                    w2_ref, o_ref):
    nsa_w = ynsa_ref.shape[1]
    width = ys_ref.shape[1]
    ys = ys_ref[...]
    dev = ys - _head_sums(ys, bd_ref[...], 3) * (1.0 / HEAD_DIM)
    var = _head_sums(dev * dev, bd_ref[...], 2) * (1.0 / HEAD_DIM)
    yn = dev * lax.rsqrt(var + GN_EPS) * lnw_ref[...] + lnb_ref[...]
    y_rwkv = ((yn + bonus_ref[...]) * g_ref[...]).astype(BF16)
    x1 = x_ref[...] + (_dot(ynsa_ref[...], wo_ref[0:nsa_w, :]) + _dot(y_rwkv, wo_ref[nsa_w:nsa_w + width, :]))
    xn = _rms(x1, fg_ref[...]).astype(BF16)
    d_ff = w1_ref.shape[1]
    chunk = 1024
    ffn = None
    for c in range(d_ff // chunk):
        hid = _dot(xn, w1_ref[:, c * chunk:(c + 1) * chunk])
        hid = jnp.square(jnp.maximum(hid, 0.0)).astype(BF16)
        part = _dot(hid, w2_ref[c * chunk:(c + 1) * chunk, :])
        ffn = part if ffn is None else ffn + part
    o_ref[...] = x1 + ffn


def _out_ffn(x2, ynsa, ys, g, bonus, lnx_w, lnx_b, w_out, ffn_g, w1, w2):
    N, D = x2.shape
    nsa_w = ynsa.shape[1]
    width = ys.shape[1]
    d_ff = w1.shape[1]
    tm = ROW_TILE
    rowspec = lambda w: pl.BlockSpec((tm, w), lambda i: (i, 0))
    return pl.pallas_call(
        _out_ffn_kernel,
        grid=(N // tm,),
        in_specs=[
            rowspec(D), rowspec(nsa_w), rowspec(width), rowspec(width), rowspec(width),
            _const_spec((1, width)), _const_spec((1, width)),
            _const_spec((width, width)),
            _const_spec((nsa_w + width, D)),
            _const_spec((1, D)),
            _const_spec((D, d_ff)),
            _const_spec((d_ff, D)),
        ],
        out_specs=rowspec(D),
        out_shape=jax.ShapeDtypeStruct((N, D), F32),
        compiler_params=pltpu.CompilerParams(
            dimension_semantics=("parallel",), vmem_limit_bytes=VMEM_LIMIT),
        name="out_ffn",
    )(x2, ynsa, ys, g, bonus, lnx_w, lnx_b, _head_ones(width), w_out, ffn_g, w1, w2)


def _alibi_slopes(n):
    start = 2.0 ** (-8.0 / n)
    return (start ** np.arange(1, n + 1)).astype(np.float32)


def _slope_lanes(n_heads):
    m = jnp.asarray(_alibi_slopes(n_heads)) * LOG2E
    pieces = jnp.stack(_split3_bf16(m), axis=-1).astype(F32)
    lanes = jnp.zeros((n_heads, LANES), F32).at[:, HEAD_DIM:HEAD_DIM + 3].set(pieces)
    lanes = lanes.at[:, HEAD_DIM + 3:HEAD_DIM + 6].set(pieces)
    gqa = n_heads // NSA_KV_HEADS
    return jnp.repeat(lanes.reshape(NSA_KV_HEADS, gqa, LANES), Q_TILE, axis=1).astype(BF16)


def _overlap_t(ncp, n_cmp, n_sel):
    ci = np.arange(ncp)[None, :] * CMP_STRIDE
    sj = np.arange(SEL_LANES)[:, None] * SEL_BLOCK
    ov = (ci <= sj + SEL_BLOCK - 1) & (ci + CMP_BLOCK - 1 >= sj)
    ov &= (np.arange(ncp)[None, :] < n_cmp) & (np.arange(SEL_LANES)[:, None] < n_sel)
    return jnp.asarray(np.tile(ov.astype(np.float32), (1, 3)), dtype=BF16)


def _compress_weights(k_pos, k_w1, k_w2, v_pos, v_w1, v_w2):
    half = CMP_BLOCK // 2
    groups = 2 * NSA_KV_HEADS

    eye = jnp.eye(groups, dtype=F32)
    per_group = lambda k_part, v_part: jnp.stack([k_part] * NSA_KV_HEADS + [v_part] * NSA_KV_HEADS)

    def first_layer(lo):
        w1 = per_group(k_w1, v_w1).reshape(groups, CMP_BLOCK, HEAD_DIM, CMP_HIDDEN)[:, lo:lo + half]
        w = w1.transpose(1, 0, 2, 3)[:, :, :, None, :] * eye[None, :, None, :, None]
        return w.reshape(half * groups * HEAD_DIM, groups * CMP_HIDDEN).astype(BF16)

    def pos_row(lo):
        return per_group(k_pos, v_pos)[:, lo:lo + half].transpose(1, 0, 2).reshape(1, half * groups * HEAD_DIM)

    w2 = per_group(k_w2, v_w2)[:, :, None, :] * eye[:, None, :, None]
    w2 = w2.reshape(groups * CMP_HIDDEN, groups * HEAD_DIM).astype(BF16)
    return first_layer(0), first_layer(half), pos_row(0), pos_row(half), w2


def _layer(x, ln_mix_g, w_in, nsa_gate_b, q_norm_g, kc_norm_g, ks_norm_g, kw_norm_g,
           cmp_k_pos, cmp_k_w1, cmp_k_w2, cmp_v_pos, cmp_v_w1, cmp_v_w2,
           rwkv_mu, rwkv_w0, rwkv_w2, rwkv_a0, rwkv_a2, rwkv_g2, rwkv_k_k, rwkv_k_a, rwkv_r_k,
           rwkv_lnx_w, rwkv_lnx_b, w_out, ln_ffn_g, w_ff1, w_ff2):
    B, T, D = x.shape
    nsa_w = D // 2
    n_heads = nsa_w // HEAD_DIM
    gqa = n_heads // NSA_KV_HEADS
    kvw = NSA_KV_HEADS * HEAD_DIM
    rw_w = D - nsa_w
    rw_heads = rw_w // HEAD_DIM
    n_sel = T // SEL_BLOCK
    ncp = T // CMP_STRIDE
    n_cmp = (T - CMP_BLOCK) // CMP_STRIDE + 1
    top_k = min(SEL_TOPK, n_sel)
    assert T % K_TILE == 0 and T % ROW_TILE == 0 and n_sel <= SEL_LANES and n_cmp == ncp - 1
    assert B * rw_heads * 2 == LANES and gqa * N_BRANCH <= LANES and T > WINDOW
    assert DECAY_LORA + AAA_LORA == LANES and GATE_LORA == LANES

    row2 = lambda u: u.reshape(1, -1)
    nsa_main = nsa_w + 6 * kvw
    gl = w_in[:, nsa_main:nsa_main + n_heads * N_BRANCH].reshape(D, NSA_KV_HEADS, gqa * N_BRANCH)
    gl = jnp.pad(gl, ((0, 0), (0, 0), (0, LANES - gqa * N_BRANCH))).reshape(D, NSA_KV_HEADS * LANES)
    w_nsa = jnp.concatenate([w_in[:, :nsa_main], gl], axis=1).astype(BF16)
    w_rwkv = w_in[:, nsa_main + n_heads * N_BRANCH:].astype(BF16)
    gate_b2 = jnp.pad(nsa_gate_b.reshape(NSA_KV_HEADS, gqa * N_BRANCH),
                      ((0, 0), (0, LANES - gqa * N_BRANCH))).reshape(1, NSA_KV_HEADS * LANES)

    q, kvc, ksa, vs, kw, vw, gates, rw_cols = _inproj(
        x, row2(ln_mix_g), w_nsa, w_rwkv, jnp.tile(row2(q_norm_g), (1, n_heads)),
        jnp.tile(row2(ks_norm_g), (1, NSA_KV_HEADS)), jnp.tile(row2(kw_norm_g), (1, NSA_KV_HEADS)), gate_b2)

    wa, wb, pa, pb, cw2 = _compress_weights(cmp_k_pos, cmp_k_w1, cmp_k_w2, cmp_v_pos, cmp_v_w1, cmp_v_w2)
    kc, vc = _compress(kvc, wa, wb, pa, pb, cw2, row2(kc_norm_g))
    y_nsa = _nsa_attention(q, kc, vc, ksa, vs, kw, vw, gates, _slope_lanes(n_heads),
                           _overlap_t(ncp, n_cmp, n_sel), top_k)

    wwa = jnp.zeros((LANES, 2 * rw_w), F32)
    wwa = wwa.at[:DECAY_LORA, :rw_w].set(rwkv_w2).at[DECAY_LORA:, rw_w:].set(rwkv_a2).astype(BF16)
    r, w, k, v, a, b, g, bonus = _rwkv_prep(
        rw_cols, row2(rwkv_mu), wwa, rwkv_g2.astype(BF16), row2(rwkv_w0), row2(rwkv_a0),
        row2(rwkv_k_k), row2(rwkv_k_a), row2(rwkv_r_k))
    ys = _wkv_scan(r, w, k, v, a, b)

    N = B * T
    out = _out_ffn(x.reshape(N, D), y_nsa.reshape(N, nsa_w), ys.reshape(N, rw_w), g.reshape(N, rw_w),
                   bonus.reshape(N, rw_w), row2(rwkv_lnx_w), row2(rwkv_lnx_b), w_out.astype(BF16),
                   row2(ln_ffn_g), w_ff1.astype(BF16), w_ff2.astype(BF16))
    return out.reshape(B, T, D)


def kernel(x, ln_mix_g, w_in, nsa_gate_b, q_norm_g, kc_norm_g, ks_norm_g, kw_norm_g, cmp_k_pos, cmp_k_w1, cmp_k_w2, cmp_v_pos, cmp_v_w1, cmp_v_w2, rwkv_mu, rwkv_w0, rwkv_w2, rwkv_a0, rwkv_a2, rwkv_g2, rwkv_k_k, rwkv_k_a, rwkv_r_k, rwkv_lnx_w, rwkv_lnx_b, w_out, ln_ffn_g, w_ff1, w_ff2):
    params = (ln_mix_g, w_in, nsa_gate_b, q_norm_g, kc_norm_g, ks_norm_g, kw_norm_g, cmp_k_pos, cmp_k_w1,
              cmp_k_w2, cmp_v_pos, cmp_v_w1, cmp_v_w2, rwkv_mu, rwkv_w0, rwkv_w2, rwkv_a0, rwkv_a2, rwkv_g2,
              rwkv_k_k, rwkv_k_a, rwkv_r_k, rwkv_lnx_w, rwkv_lnx_b, w_out, ln_ffn_g, w_ff1, w_ff2)
    for layer in range(ln_mix_g.shape[0]):
        x = _layer(x, *(p[layer] for p in params))
    return x
```

```python
import functools

import numpy as np
import jax
import jax.numpy as jnp
from jax import lax
from jax.experimental import pallas as pl
from jax.experimental.pallas import tpu as pltpu

F32 = jnp.float32
BF16 = jnp.bfloat16

HEAD_DIM = 64
NSA_KV_HEADS = 2
CMP_BLOCK = 32
CMP_STRIDE = 16
CMP_HIDDEN = 2 * HEAD_DIM
SEL_BLOCK = 64
SEL_SHIFT = 6
SEL_TOPK = 16
WINDOW = 512
N_BRANCH = 3
DECAY_LORA = 64
AAA_LORA = 64
GATE_LORA = 128
RMS_EPS = 1e-6
GN_EPS = HEAD_DIM * 1e-5
NEG_BIG = -1e30
FORCED_SCORE = 1e6
LOG2E = 1.4426950408889634

LANES = 128
SEL_LANES = 64
MASK_BIG = 2.0 ** 60
VMEM_LIMIT = 56 * 1024 * 1024

ROW_TILE = 512
Q_TILE = 256
K_TILE = 512
SCAN_CHUNK = 64


def _rms(u, g):
    return u * lax.rsqrt(jnp.mean(u * u, axis=-1, keepdims=True) + RMS_EPS) * g


def _sigmoid(u):
    return 1.0 / (1.0 + jnp.exp(-u))


def _dot(a, b):
    return jnp.dot(a, b, preferred_element_type=F32)


def _dot_nt(a, b):
    return lax.dot_general(a, b, (((1,), (1,)), ((), ())), preferred_element_type=F32)


def _const_spec(shape):
    nd = len(shape)
    return pl.BlockSpec(shape, lambda *_: (0,) * nd, pipeline_mode=pl.Buffered(1))


def _split_bf16(u, n_pieces):
    pieces, rest = [], u
    for i in range(n_pieces):
        piece = rest.astype(BF16)
        pieces.append(piece)
        if i + 1 < n_pieces:
            rest = rest - piece.astype(F32)
    return pieces


def _head_sums(u, head_ones, n_pieces):
    return _tree_sum([_dot(piece, head_ones) for piece in _split_bf16(u, n_pieces)])


def _head_ones(n_lanes):
    head_id = np.arange(n_lanes) // HEAD_DIM
    return jnp.asarray((head_id[:, None] == head_id[None, :]).astype(np.float32), dtype=BF16)


def _tree_sum(terms):
    while len(terms) > 1:
        nxt = [terms[i] + terms[i + 1] for i in range(0, len(terms) - 1, 2)]
        if len(terms) % 2:
            nxt.append(terms[-1])
        terms = nxt
    return terms[0]


def _inproj_kernel(x_ref, g_ref, wn_ref, wr_ref, qg_ref, ksg_ref, kwg_ref, gb_ref, bd_ref,
                   q_ref, kvc_ref, ksa_ref, vs_ref, kw_ref, vw_ref, gates_ref, rw_ref):
    tm = x_ref.shape[1]
    n_q_heads = q_ref.shape[1]
    nsa_w = n_q_heads * HEAD_DIM
    kvw = NSA_KV_HEADS * HEAD_DIM
    xn = _rms(x_ref[0], g_ref[...]).astype(BF16)
    pn = _dot(xn, wn_ref[...])
    rw_ref[0] = _dot(xn, wr_ref[...])

    def head_rms(u, gain):
        ss = _head_sums(u * u, bd_ref[0:u.shape[1], 0:u.shape[1]], 2)
        return u * lax.rsqrt(ss * (1.0 / HEAD_DIM) + RMS_EPS) * gain

    zeros_hd = jnp.zeros((tm, HEAD_DIM), F32)
    ones_hd = jnp.ones((tm, HEAD_DIM), F32)
    qn = head_rms(pn[:, 0:nsa_w], qg_ref[...]) * (HEAD_DIM ** -0.5 * LOG2E)
    for h in range(n_q_heads):
        q_ref[0, h] = jnp.concatenate([qn[:, h * HEAD_DIM:(h + 1) * HEAD_DIM], zeros_hd], axis=-1).astype(BF16)

    kvc_ref[0, 0] = pn[:, nsa_w:nsa_w + kvw]
    kvc_ref[0, 1] = pn[:, nsa_w + kvw:nsa_w + 2 * kvw]

    tok = pl.program_id(1) * tm + lax.broadcasted_iota(jnp.int32, (tm, 1), 0)
    blk = lax.shift_right_logical(tok, SEL_SHIFT)
    onehot = jnp.where(blk == lax.broadcasted_iota(jnp.int32, (1, SEL_LANES), 1), MASK_BIG, 0.0)
    pos = _pos_tile(tok)
    off = nsa_w + 2 * kvw
    ksn = head_rms(pn[:, off:off + kvw], ksg_ref[...])
    kwn = head_rms(pn[:, off + 2 * kvw:off + 3 * kvw], kwg_ref[...])
    for h in range(NSA_KV_HEADS):
        ksa_ref[0, h] = jnp.concatenate([ksn[:, h * HEAD_DIM:(h + 1) * HEAD_DIM], onehot, pos], axis=-1).astype(BF16)
        o2 = off + kvw
        vs_ref[0, h] = jnp.concatenate([pn[:, o2 + h * HEAD_DIM:o2 + (h + 1) * HEAD_DIM], ones_hd], axis=-1).astype(BF16)
        kw_ref[0, h] = (jnp.concatenate([kwn[:, h * HEAD_DIM:(h + 1) * HEAD_DIM], zeros_hd], axis=-1) + pos).astype(BF16)
        o4 = off + 3 * kvw
        vw_ref[0, h] = jnp.concatenate([pn[:, o4 + h * HEAD_DIM:o4 + (h + 1) * HEAD_DIM], ones_hd], axis=-1).astype(BF16)
        o5 = off + 4 * kvw + h * LANES
        gates_ref[0, h] = _sigmoid(pn[:, o5:o5 + LANES] + gb_ref[:, h * LANES:(h + 1) * LANES])


def _inproj(x, ln_g, w_nsa, w_rwkv, q_g, ks_g, kw_g, gate_b2):
    B, T, D = x.shape
    nsa_cols = w_nsa.shape[1]
    rw_cols = w_rwkv.shape[1]
    n_q_heads = (nsa_cols - 2 * LANES - 6 * NSA_KV_HEADS * HEAD_DIM) // HEAD_DIM
    tm = ROW_TILE
    kvw = NSA_KV_HEADS * HEAD_DIM
    head_ones = _head_ones(n_q_heads * HEAD_DIM)
    out_shape = (
        jax.ShapeDtypeStruct((B, n_q_heads, T, LANES), BF16),
        jax.ShapeDtypeStruct((B, 2, T, kvw), F32),
        jax.ShapeDtypeStruct((B, NSA_KV_HEADS, T, 2 * LANES), BF16),
        jax.ShapeDtypeStruct((B, NSA_KV_HEADS, T, LANES), BF16),
        jax.ShapeDtypeStruct((B, NSA_KV_HEADS, T, LANES), BF16),
        jax.ShapeDtypeStruct((B, NSA_KV_HEADS, T, LANES), BF16),
        jax.ShapeDtypeStruct((B, NSA_KV_HEADS, T, LANES), F32),
        jax.ShapeDtypeStruct((B, T, rw_cols), F32),
    )
    head_spec = lambda n, w: pl.BlockSpec((1, n, tm, w), lambda b, i: (b, 0, i, 0))
    return pl.pallas_call(
        _inproj_kernel,
        grid=(B, T // tm),
        in_specs=[
            pl.BlockSpec((1, tm, D), lambda b, i: (b, i, 0)),
            _const_spec((1, D)),
            _const_spec((D, nsa_cols)),
            _const_spec((D, rw_cols)),
            _const_spec((1, n_q_heads * HEAD_DIM)),
            _const_spec((1, kvw)),
            _const_spec((1, kvw)),
            _const_spec((1, NSA_KV_HEADS * LANES)),
            _const_spec((n_q_heads * HEAD_DIM, n_q_heads * HEAD_DIM)),
        ],
        out_specs=(
            head_spec(n_q_heads, LANES),
            head_spec(2, kvw),
            head_spec(NSA_KV_HEADS, 2 * LANES),
            head_spec(NSA_KV_HEADS, LANES),
            head_spec(NSA_KV_HEADS, LANES),
            head_spec(NSA_KV_HEADS, LANES),
            head_spec(NSA_KV_HEADS, LANES),
            pl.BlockSpec((1, tm, rw_cols), lambda b, i: (b, i, 0)),
        ),
        out_shape=out_shape,
        compiler_params=pltpu.CompilerParams(
            dimension_semantics=("parallel", "parallel"), vmem_limit_bytes=VMEM_LIMIT),
        name="inproj",
    )(x, ln_g, w_nsa, w_rwkv, q_g, ks_g, kw_g, gate_b2, head_ones)


def _gelu_tanh(u):
    return 0.5 * u * (1.0 + jnp.tanh(np.sqrt(2.0 / np.pi).astype(np.float32) * (u + 0.044715 * (u * u * u))))


def _compress_kernel(kv_ref, wa_ref, wb_ref, pa_ref, pb_ref, w2_ref, kcg_ref, kc_ref, vc_ref):
    width = 2 * kv_ref.shape[3]
    ncp = kv_ref.shape[2] // CMP_STRIDE
    firsts, seconds = [], []
    for l in range(CMP_STRIDE):
        tok = jnp.concatenate([kv_ref[0, p, pl.ds(l, ncp, stride=CMP_STRIDE), :] for p in range(2)], axis=-1)
        cols = slice(l * width, (l + 1) * width)
        firsts.append(_dot((tok + pa_ref[:, cols]).astype(BF16), wa_ref[cols, :]))
        seconds.append(_dot((tok + pb_ref[:, cols]).astype(BF16), wb_ref[cols, :]))
    first, second = _tree_sum(firsts), _tree_sum(seconds)
    row = lax.broadcasted_iota(jnp.int32, (ncp, 1), 0)
    real = row < ncp - 1
    second = jnp.where(real, pltpu.roll(second, ncp - 1, axis=0), 0.0)
    hid = _gelu_tanh(first + second)
    out = jnp.where(real, _dot(hid.astype(BF16), w2_ref[...]), 0.0)
    zeros_hd = jnp.zeros((ncp, HEAD_DIM), F32)
    pos = _pos_tile(row * CMP_STRIDE + (CMP_BLOCK - 1))
    for h in range(NSA_KV_HEADS):
        kch = _rms(out[:, h * HEAD_DIM:(h + 1) * HEAD_DIM], kcg_ref[...])
        kc_ref[0, h] = (jnp.concatenate([kch, zeros_hd], axis=-1) + pos).astype(BF16)
        o = (NSA_KV_HEADS + h) * HEAD_DIM
        vc_ref[0, h] = out[:, o:o + HEAD_DIM].astype(BF16)


def _compress(kvc, wa, wb, pa, pb, w2, kc_g):
    B, _, T, kv_cols = kvc.shape
    ncp = T // CMP_STRIDE
    width, hid = wa.shape
    return pl.pallas_call(
        _compress_kernel,
        grid=(B,),
        in_specs=[
            pl.BlockSpec((1, 2, T, kv_cols), lambda b: (b, 0, 0, 0)),
            _const_spec((width, hid)),
            _const_spec((width, hid)),
            _const_spec((1, width)),
            _const_spec((1, width)),
            _const_spec((hid, 2 * NSA_KV_HEADS * HEAD_DIM)),
            _const_spec((1, HEAD_DIM)),
        ],
        out_specs=(
            pl.BlockSpec((1, NSA_KV_HEADS, ncp, LANES), lambda b: (b, 0, 0, 0)),
            pl.BlockSpec((1, NSA_KV_HEADS, ncp, HEAD_DIM), lambda b: (b, 0, 0, 0)),
        ),
        out_shape=(
            jax.ShapeDtypeStruct((B, NSA_KV_HEADS, ncp, LANES), BF16),
            jax.ShapeDtypeStruct((B, NSA_KV_HEADS, ncp, HEAD_DIM), BF16),
        ),
        compiler_params=pltpu.CompilerParams(
            dimension_semantics=("parallel",), vmem_limit_bytes=VMEM_LIMIT),
        name="compress",
    )(kvc, wa, wb, pa, pb, w2, kc_g)


def _split3_bf16(u):
    hi = u.astype(BF16)
    r1 = u - hi.astype(F32)
    mid = r1.astype(BF16)
    lo = (r1 - mid.astype(F32)).astype(BF16)
    return hi, mid, lo


def _pos_tile(pos):
    lane = lax.broadcasted_iota(jnp.int32, (1, LANES), 1)
    hi = lax.shift_left(lax.shift_right_logical(pos, SEL_SHIFT), SEL_SHIFT).astype(F32)
    lo = (pos & (SEL_BLOCK - 1)).astype(F32)
    first = HEAD_DIM
    return jnp.where((lane >= first) & (lane < first + 3), hi,
                     jnp.where((lane >= first + 3) & (lane < first + 6), lo, 0.0))


def _nsa_kernel(q_ref, kc_ref, vc_ref, ksa_ref, vs_ref, kw_ref, vw_ref, gates_ref, slopeq_ref, ovt_ref, wbias_ref,
                o_ref, *, top_k):
    gqa, tq = q_ref.shape[1], q_ref.shape[2]
    ncp = kc_ref.shape[2]
    rows = gqa * tq
    q0 = pl.program_id(2) * tq
    q = q_ref[0].reshape(rows, LANES)
    slope_q = slopeq_ref[0]
    q_pos = q + slope_q
    row = lax.broadcasted_iota(jnp.int32, (rows, 1), 0)
    t_rows = q0 + (row & (tq - 1))

    cend = lax.broadcasted_iota(jnp.int32, (1, ncp), 1) * CMP_STRIDE + (CMP_BLOCK - 1)
    valid_c = t_rows >= cend
    s_c = jnp.where(valid_c, _dot_nt(q_pos, kc_ref[0, 0]), NEG_BIG)
    e_c = jnp.where(valid_c, jnp.exp2(s_c - jnp.max(s_c, axis=-1, keepdims=True)), 0.0)
    l_c = jnp.sum(e_c, axis=-1, keepdims=True)
    p_c = e_c * (1.0 / jnp.where(l_c > 0.0, l_c, 1.0))
    o_c = _dot(p_c.astype(BF16), vc_ref[0, 0])

    wk = WINDOW + tq
    start = pl.multiple_of(jnp.maximum(q0 - WINDOW, 0), tq)
    s_w = _dot_nt(q_pos, kw_ref[0, 0, pl.ds(start, wk), :]).reshape(gqa, tq, wk) + wbias_ref[...]
    s_w = s_w.reshape(rows, wk)
    e_w = jnp.exp2(s_w - jnp.max(s_w, axis=-1, keepdims=True))
    pv_w = _dot(e_w.astype(BF16), vw_ref[0, 0, pl.ds(start, wk), :])
    o_w = pv_w[:, 0:HEAD_DIM] * (1.0 / pv_w[:, HEAD_DIM:HEAD_DIM + 1])

    p_sum = p_c[0:tq]
    for g in range(1, gqa):
        p_sum = p_sum + p_c[g * tq:(g + 1) * tq]
    imp_t = _dot_nt(ovt_ref[...], jnp.concatenate(_split3_bf16(p_sum), axis=-1))

    jb = lax.broadcasted_iota(jnp.int32, (SEL_LANES, 1), 0)
    jb_f = jb.astype(F32)
    cur = lax.shift_right_logical(q0 + lax.broadcasted_iota(jnp.int32, (1, tq), 1), SEL_SHIFT)
    forced = (jb == 0) | (jb == cur) | (jb == cur - 1)
    work = jnp.where(forced, FORCED_SCORE, jnp.where(jb > cur, NEG_BIG, imp_t))
    sel_t = jnp.zeros((SEL_LANES, tq), F32)
    for _ in range(top_k):
        mx = jnp.max(work, axis=0, keepdims=True)
        idx = jnp.min(jnp.where(work == mx, jb_f, float(SEL_LANES)), axis=0, keepdims=True)
        pick = jb_f == idx
        sel_t = jnp.where(pick, 1.0, sel_t)
        work = jnp.where(pick, -jnp.inf, work)
    pad_t = jnp.concatenate([jnp.zeros((SEL_LANES, tq), F32), sel_t - 1.0], axis=0)
    sel_q = pad_t.T
    q_sel = (q.astype(F32) + jnp.concatenate([sel_q] * gqa, axis=0)).astype(BF16)
    q_aug = jnp.concatenate([q_sel, slope_q], axis=-1)

    tk = K_TILE
    n_full = q0 // tk

    def scores(kt):
        return _dot_nt(q_aug, ksa_ref[0, 0, pl.ds(pl.multiple_of(kt * tk, tk), tk), :])

    def update(s, kt, carry, causal):
        m_run, acc = carry
        k0 = pl.multiple_of(kt * tk, tk)
        if causal:
            s = jnp.where(lax.broadcasted_iota(jnp.int32, (1, tk), 1) <= t_rows - k0, s, NEG_BIG)
        m_new = jnp.maximum(m_run, jnp.max(s, axis=-1, keepdims=True))
        p = jnp.exp2(s - m_new)
        return m_new, jnp.exp2(m_run - m_new) * acc + _dot(p.astype(BF16), vs_ref[0, 0, pl.ds(k0, tk), :])

    def pair(kt, carry, causal_second):
        s_a, s_b = scores(kt), scores(kt + 1)
        return update(s_b, kt + 1, update(s_a, kt, carry, False), causal_second)

    init = (jnp.full((rows, 1), NEG_BIG, F32), jnp.zeros((rows, LANES), F32))
    carry = lax.fori_loop(0, n_full // 2, lambda j, c: pair(2 * j, c, False), init)
    _, acc_s = lax.cond(
        (n_full & 1) == 1,
        lambda c: pair(n_full - 1, c, True),
        lambda c: update(scores(n_full), n_full, c, True),
        carry)
    o_s = acc_s[:, 0:HEAD_DIM] * (1.0 / acc_s[:, HEAD_DIM:HEAD_DIM + 1])

    gt = gates_ref[0, 0]
    outs = []
    for g in range(gqa):
        sl = slice(g * tq, (g + 1) * tq)
        c = g * N_BRANCH
        outs.append(gt[:, c:c + 1] * o_c[sl] + gt[:, c + 1:c + 2] * o_s[sl] + gt[:, c + 2:c + 3] * o_w[sl])
    o_ref[0] = jnp.concatenate(outs, axis=-1).astype(o_ref.dtype)


def _window_bias(tq):
    off = np.minimum(np.arange(WINDOW // tq + 1) * tq, WINDOW)[:, None, None]
    d = off + np.arange(tq)[None, :, None] - np.arange(WINDOW + tq)[None, None, :]
    return jnp.asarray(np.where((d >= 0) & (d < WINDOW), 0.0, NEG_BIG).astype(np.float32))


def _nsa_attention(q, kc, vc, ksa, vs, kw, vw, gates, slope_q, ovt, top_k):
    B, n_heads, T, _ = q.shape
    gqa = n_heads // NSA_KV_HEADS
    ncp = kc.shape[2]
    tq = Q_TILE
    rows = gqa * tq
    kv_spec = lambda n, w: pl.BlockSpec((1, 1, n, w), lambda b, h, i: (b, h, 0, 0))
    return pl.pallas_call(
        functools.partial(_nsa_kernel, top_k=top_k),
        grid=(B, NSA_KV_HEADS, T // tq),
        in_specs=[
            pl.BlockSpec((1, gqa, tq, LANES), lambda b, h, i: (b, h, i, 0)),
            kv_spec(ncp, LANES),
            kv_spec(ncp, HEAD_DIM),
            kv_spec(T, 2 * LANES),
            kv_spec(T, LANES),
            kv_spec(T, LANES),
            kv_spec(T, LANES),
            pl.BlockSpec((1, 1, tq, LANES), lambda b, h, i: (b, h, i, 0)),
            pl.BlockSpec((1, rows, LANES), lambda b, h, i: (h, 0, 0)),
            pl.BlockSpec((SEL_LANES, 3 * ncp), lambda b, h, i: (0, 0)),
            pl.BlockSpec((1, tq, WINDOW + tq), lambda b, h, i: (jnp.minimum(i, WINDOW // tq), 0, 0)),
        ],
        out_specs=pl.BlockSpec((1, tq, gqa * HEAD_DIM), lambda b, h, i: (b, i, h)),
        out_shape=jax.ShapeDtypeStruct((B, T, n_heads * HEAD_DIM), BF16),
        compiler_params=pltpu.CompilerParams(
            dimension_semantics=("parallel", "parallel", "arbitrary"), vmem_limit_bytes=VMEM_LIMIT),
        name="nsa_attention",
    )(q, kc, vc, ksa, vs, kw, vw, gates, slope_q, ovt, _window_bias(tq))


def _rwkv_prep_kernel(c_ref, cprev_ref, mu_ref, wwa_ref, g2_ref, w0_ref, a0_ref, kk_ref, ka_ref, rk_ref, bd_ref,
                      r_ref, w_ref, k_ref, v_ref, a_ref, b_ref, g_ref, bonus_ref):
    tt = c_ref.shape[1]
    width = r_ref.shape[2]
    c = c_ref[0]
    last = jnp.where(pl.program_id(1) > 0, cprev_ref[0][7:8, :], 0.0)
    row = lax.broadcasted_iota(jnp.int32, (tt, 1), 0)
    prev = jnp.where(row == 0, last, pltpu.roll(c, 1, axis=0))
    z = c + (prev - c) * mu_ref[...]
    r = z[:, 0:width]
    k = z[:, width:2 * width]
    v = z[:, 2 * width:3 * width]
    xwa = z[:, 3 * width:3 * width + LANES]
    xg = z[:, 3 * width + LANES:3 * width + 2 * LANES]
    lane = lax.broadcasted_iota(jnp.int32, (1, LANES), 1)
    lora = _dot(jnp.where(lane < DECAY_LORA, jnp.tanh(xwa), xwa).astype(BF16), wwa_ref[...])
    y = w0_ref[...] + lora[:, 0:width]
    softplus_neg = jnp.maximum(-y, 0.0) + jnp.log(1.0 + jnp.exp(-jnp.abs(y)))
    decay = jnp.exp(-jnp.exp(-softplus_neg - 0.5))
    a = _sigmoid(a0_ref[...] + lora[:, width:2 * width])
    g_ref[0] = _dot(_sigmoid(xg).astype(BF16), g2_ref[...])
    kk = k * kk_ref[...]
    k2 = k * (1.0 + (a - 1.0) * ka_ref[...])
    rk = r * k2 * rk_ref[...]
    r_ref[0] = r
    w_ref[0] = decay
    k_ref[0] = k2
    v_ref[0] = v
    kkn = kk * lax.rsqrt(jnp.maximum(_head_sums(kk * kk, bd_ref[...], 2), 1e-24))
    a_ref[0] = -kkn
    b_ref[0] = kkn * a
    bonus_ref[0] = _head_sums(rk, bd_ref[...], 3) * v


def _rwkv_prep(cols, mu, wwa, g2, w0, a0, k_k, k_a, r_k):
    B, T, ncols = cols.shape
    width = w0.shape[1]
    tt = ROW_TILE
    big = pl.BlockSpec((1, tt, width), lambda b, i: (b, i, 0))
    vec = _const_spec((1, width))
    return pl.pallas_call(
        _rwkv_prep_kernel,
        grid=(B, T // tt),
        in_specs=[
            pl.BlockSpec((1, tt, ncols), lambda b, i: (b, i, 0)),
            pl.BlockSpec((1, 8, ncols), lambda b, i: (b, jnp.maximum(i * (tt // 8) - 1, 0), 0)),
            _const_spec((1, ncols)),
            _const_spec((LANES, 2 * width)),
            _const_spec((GATE_LORA, width)),
            vec, vec, vec, vec, vec,
            _const_spec((width, width)),
        ],
        out_specs=(big,) * 8,
        out_shape=(jax.ShapeDtypeStruct((B, T, width), F32),) * 8,
        compiler_params=pltpu.CompilerParams(
            dimension_semantics=("parallel", "parallel"), vmem_limit_bytes=VMEM_LIMIT),
        name="rwkv_prep",
    )(cols, cols, mu, wwa, g2, w0, a0, k_k, k_a, r_k, _head_ones(width))


def _sum_sublanes(u):
    u = u + pltpu.roll(u, 4, axis=0)
    u = u + pltpu.roll(u, 2, axis=0)
    return u + pltpu.roll(u, 1, axis=0)


SCAN_ROWS = 4


def _heads_to_lanes(xa, xb):
    low = lax.broadcasted_iota(jnp.int32, (1, LANES), 1) < HEAD_DIM
    rows = []
    for j in range(xa.shape[1] // LANES):
        va, vb = xa[:, j * LANES:(j + 1) * LANES], xb[:, j * LANES:(j + 1) * LANES]
        rows.append(jnp.where(low, va, pltpu.roll(vb, HEAD_DIM, axis=1)))
        rows.append(jnp.where(low, pltpu.roll(va, HEAD_DIM, axis=1), vb))
    xt = jnp.concatenate(rows + rows, axis=0).T
    return xt[0:HEAD_DIM], xt[HEAD_DIM:2 * HEAD_DIM]


def _wkv_scan_kernel(r_in, w_in, k_in, v_in, a_in, b_in, y_out, state_ref,
                     r_ref, w_ref, k_ref, a_ref, b_ref, v_ref, y_ref):
    tc = r_in.shape[1]
    nkg = HEAD_DIM // 8
    n_rows = HEAD_DIM // 2
    low = lax.broadcasted_iota(jnp.int32, (1, LANES), 1) < HEAD_DIM

    @pl.when(pl.program_id(0) == 0)
    def _():
        state_ref[...] = jnp.zeros_like(state_ref)

    def retile(t, _):
        r_t, w_t = _heads_to_lanes(r_in[:, t, :], w_in[:, t, :])
        k_t, a_t = _heads_to_lanes(k_in[:, t, :], a_in[:, t, :])
        b_t, v_t = _heads_to_lanes(b_in[:, t, :], v_in[:, t, :])
        for dst, val in ((r_ref, r_t), (w_ref, w_t), (k_ref, k_t), (a_ref, a_t), (b_ref, b_t)):
            dst[t] = val.reshape(nkg, 8, LANES)
        v_ref[t] = jnp.where(low, v_t[0:n_rows], v_t[n_rows:2 * n_rows])
        return 0

    lax.fori_loop(0, tc, retile, 0, unroll=8)

    def row_group(gi, _):
        u0 = gi * SCAN_ROWS

        def step(t, state):
            new_state = []
            for j in range(SCAN_ROWS):
                s = state[j * nkg:(j + 1) * nkg]
                v_row = v_ref[t, pl.ds(u0 + j, 1), :]
                sa = _sum_sublanes(_tree_sum([s[g] * a_ref[t, g] for g in range(nkg)]))
                new = [s[g] * w_ref[t, g] + sa * b_ref[t, g] + v_row * k_ref[t, g] for g in range(nkg)]
                y = _sum_sublanes(_tree_sum([new[g] * r_ref[t, g] for g in range(nkg)]))
                y_ref[t, pl.ds(u0 + j, 1), :] = y[0:1]
                new_state.extend(new)
            return tuple(new_state)

        init = tuple(state_ref[u0 + j, g] for j in range(SCAN_ROWS) for g in range(nkg))
        state = lax.fori_loop(0, tc, step, init, unroll=4)
        for j in range(SCAN_ROWS):
            for g in range(nkg):
                state_ref[u0 + j, g] = state[j * nkg + g]
        return 0

    lax.fori_loop(0, n_rows // SCAN_ROWS, row_group, 0)

    def untile(i, _):
        ys = [y_ref[2 * i + d] for d in range(2)]
        m = jnp.concatenate([u for y_t in ys for u in (y_t, pltpu.roll(y_t, HEAD_DIM, axis=1))], axis=0)
        mt = m.T
        first, second = [], []
        for j in range(y_out.shape[2] // LANES):
            even, odd = mt[16 * j:16 * j + 8], mt[16 * j + 8:16 * j + 16]
            first.append(jnp.where(low, even, pltpu.roll(odd, HEAD_DIM, axis=1)))
            second.append(jnp.where(low, pltpu.roll(even, HEAD_DIM, axis=1), odd))
        y_out[:, 2 * i, :] = jnp.concatenate(first, axis=-1)
        y_out[:, 2 * i + 1, :] = jnp.concatenate(second, axis=-1)
        return 0

    lax.fori_loop(0, tc // 2, untile, 0, unroll=4)


def _wkv_scan(r, w, k, v, a, b):
    B, T, width = r.shape
    tc = SCAN_CHUNK
    spec = pl.BlockSpec((B, tc, width), lambda i: (0, i, 0))
    wide = pltpu.VMEM((tc, HEAD_DIM // 8, 8, LANES), F32)
    rows = pltpu.VMEM((tc, HEAD_DIM // 2, LANES), F32)
    return pl.pallas_call(
        _wkv_scan_kernel,
        grid=(T // tc,),
        in_specs=[spec] * 6,
        out_specs=spec,
        out_shape=jax.ShapeDtypeStruct((B, T, width), F32),
        scratch_shapes=[pltpu.VMEM((HEAD_DIM // 2, HEAD_DIM // 8, 8, LANES), F32),
                        wide, wide, wide, wide, wide, rows, rows],
        compiler_params=pltpu.CompilerParams(
            dimension_semantics=("arbitrary",), vmem_limit_bytes=VMEM_LIMIT),
        name="wkv_scan",
    )(r, w, k, v, a, b)


def _out_ffn_kernel(x_ref, ynsa_ref, ys_ref, g_ref, bonus_ref, lnw_ref, lnb_ref, bd_ref, wo_ref, fg_ref, w1_ref,
                    w2_ref, o_ref):
    nsa_w = ynsa_ref.shape[1]
    width = ys_ref.shape[1]
    ys = ys_ref[...]
    dev = ys - _head_sums(ys, bd_ref[...], 3) * (1.0 / HEAD_DIM)
    var = _head_sums(dev * dev, bd_ref[...], 2) * (1.0 / HEAD_DIM)
    yn = dev * lax.rsqrt(var + GN_EPS) * lnw_ref[...] + lnb_ref[...]
    y_rwkv = ((yn + bonus_ref[...]) * g_ref[...]).astype(BF16)
    x1 = x_ref[...] + (_dot(ynsa_ref[...], wo_ref[0:nsa_w, :]) + _dot(y_rwkv, wo_ref[nsa_w:nsa_w + width, :]))
    xn = _rms(x1, fg_ref[...]).astype(BF16)
    d_ff = w1_ref.shape[1]
    chunk = 1024
    ffn = None
    for c in range(d_ff // chunk):
        hid = _dot(xn, w1_ref[:, c * chunk:(c + 1) * chunk])
        hid = jnp.square(jnp.maximum(hid, 0.0)).astype(BF16)
        part = _dot(hid, w2_ref[c * chunk:(c + 1) * chunk, :])
        ffn = part if ffn is None else ffn + part
    o_ref[...] = x1 + ffn


def _out_ffn(x2, ynsa, ys, g, bonus, lnx_w, lnx_b, w_out, ffn_g, w1, w2):
    N, D = x2.shape
    nsa_w = ynsa.shape[1]
    width = ys.shape[1]
    d_ff = w1.shape[1]
    tm = ROW_TILE
    rowspec = lambda w: pl.BlockSpec((tm, w), lambda i: (i, 0))
    return pl.pallas_call(
        _out_ffn_kernel,
        grid=(N // tm,),
        in_specs=[
            rowspec(D), rowspec(nsa_w), rowspec(width), rowspec(width), rowspec(width),
            _const_spec((1, width)), _const_spec((1, width)),
            _const_spec((width, width)),
            _const_spec((nsa_w + width, D)),
            _const_spec((1, D)),
            _const_spec((D, d_ff)),
            _const_spec((d_ff, D)),
        ],
        out_specs=rowspec(D),
        out_shape=jax.ShapeDtypeStruct((N, D), F32),
        compiler_params=pltpu.CompilerParams(
            dimension_semantics=("parallel",), vmem_limit_bytes=VMEM_LIMIT),
        name="out_ffn",
    )(x2, ynsa, ys, g, bonus, lnx_w, lnx_b, _head_ones(width), w_out, ffn_g, w1, w2)


def _alibi_slopes(n):
    start = 2.0 ** (-8.0 / n)
    return (start ** np.arange(1, n + 1)).astype(np.float32)


def _slope_lanes(n_heads):
    m = jnp.asarray(_alibi_slopes(n_heads)) * LOG2E
    pieces = jnp.stack(_split3_bf16(m), axis=-1).astype(F32)
    lanes = jnp.zeros((n_heads, LANES), F32).at[:, HEAD_DIM:HEAD_DIM + 3].set(pieces)
    lanes = lanes.at[:, HEAD_DIM + 3:HEAD_DIM + 6].set(pieces)
    gqa = n_heads // NSA_KV_HEADS
    return jnp.repeat(lanes.reshape(NSA_KV_HEADS, gqa, LANES), Q_TILE, axis=1).astype(BF16)


def _overlap_t(ncp, n_cmp, n_sel):
    ci = np.arange(ncp)[None, :] * CMP_STRIDE
    sj = np.arange(SEL_LANES)[:, None] * SEL_BLOCK
    ov = (ci <= sj + SEL_BLOCK - 1) & (ci + CMP_BLOCK - 1 >= sj)
    ov &= (np.arange(ncp)[None, :] < n_cmp) & (np.arange(SEL_LANES)[:, None] < n_sel)
    return jnp.asarray(np.tile(ov.astype(np.float32), (1, 3)), dtype=BF16)


def _compress_weights(k_pos, k_w1, k_w2, v_pos, v_w1, v_w2):
    half = CMP_BLOCK // 2
    groups = 2 * NSA_KV_HEADS

    eye = jnp.eye(groups, dtype=F32)
    per_group = lambda k_part, v_part: jnp.stack([k_part] * NSA_KV_HEADS + [v_part] * NSA_KV_HEADS)

    def first_layer(lo):
        w1 = per_group(k_w1, v_w1).reshape(groups, CMP_BLOCK, HEAD_DIM, CMP_HIDDEN)[:, lo:lo + half]
        w = w1.transpose(1, 0, 2, 3)[:, :, :, None, :] * eye[None, :, None, :, None]
        return w.reshape(half * groups * HEAD_DIM, groups * CMP_HIDDEN).astype(BF16)

    def pos_row(lo):
        return per_group(k_pos, v_pos)[:, lo:lo + half].transpose(1, 0, 2).reshape(1, half * groups * HEAD_DIM)

    w2 = per_group(k_w2, v_w2)[:, :, None, :] * eye[:, None, :, None]
    w2 = w2.reshape(groups * CMP_HIDDEN, groups * HEAD_DIM).astype(BF16)
    return first_layer(0), first_layer(half), pos_row(0), pos_row(half), w2


def _layer(x, ln_mix_g, w_in, nsa_gate_b, q_norm_g, kc_norm_g, ks_norm_g, kw_norm_g,
           cmp_k_pos, cmp_k_w1, cmp_k_w2, cmp_v_pos, cmp_v_w1, cmp_v_w2,
           rwkv_mu, rwkv_w0, rwkv_w2, rwkv_a0, rwkv_a2, rwkv_g2, rwkv_k_k, rwkv_k_a, rwkv_r_k,
           rwkv_lnx_w, rwkv_lnx_b, w_out, ln_ffn_g, w_ff1, w_ff2):
    B, T, D = x.shape
    nsa_w = D // 2
    n_heads = nsa_w // HEAD_DIM
    gqa = n_heads // NSA_KV_HEADS
    kvw = NSA_KV_HEADS * HEAD_DIM
    rw_w = D - nsa_w
    rw_heads = rw_w // HEAD_DIM
    n_sel = T // SEL_BLOCK
    ncp = T // CMP_STRIDE
    n_cmp = (T - CMP_BLOCK) // CMP_STRIDE + 1
    top_k = min(SEL_TOPK, n_sel)
    assert T % K_TILE == 0 and T % ROW_TILE == 0 and n_sel <= SEL_LANES and n_cmp == ncp - 1
    assert B * rw_heads * 2 == LANES and gqa * N_BRANCH <= LANES and T > WINDOW
    assert DECAY_LORA + AAA_LORA == LANES and GATE_LORA == LANES

    row2 = lambda u: u.reshape(1, -1)
    nsa_main = nsa_w + 6 * kvw
    gl = w_in[:, nsa_main:nsa_main + n_heads * N_BRANCH].reshape(D, NSA_KV_HEADS, gqa * N_BRANCH)
    gl = jnp.pad(gl, ((0, 0), (0, 0), (0, LANES - gqa * N_BRANCH))).reshape(D, NSA_KV_HEADS * LANES)
    w_nsa = jnp.concatenate([w_in[:, :nsa_main], gl], axis=1).astype(BF16)
    w_rwkv = w_in[:, nsa_main + n_heads * N_BRANCH:].astype(BF16)
    gate_b2 = jnp.pad(nsa_gate_b.reshape(NSA_KV_HEADS, gqa * N_BRANCH),
                      ((0, 0), (0, LANES - gqa * N_BRANCH))).reshape(1, NSA_KV_HEADS * LANES)

    q, kvc, ksa, vs, kw, vw, gates, rw_cols = _inproj(
        x, row2(ln_mix_g), w_nsa, w_rwkv, jnp.tile(row2(q_norm_g), (1, n_heads)),
        jnp.tile(row2(ks_norm_g), (1, NSA_KV_HEADS)), jnp.tile(row2(kw_norm_g), (1, NSA_KV_HEADS)), gate_b2)

    wa, wb, pa, pb, cw2 = _compress_weights(cmp_k_pos, cmp_k_w1, cmp_k_w2, cmp_v_pos, cmp_v_w1, cmp_v_w2)
    kc, vc = _compress(kvc, wa, wb, pa, pb, cw2, row2(kc_norm_g))
    y_nsa = _nsa_attention(q, kc, vc, ksa, vs, kw, vw, gates, _slope_lanes(n_heads),
                           _overlap_t(ncp, n_cmp, n_sel), top_k)

    wwa = jnp.zeros((LANES, 2 * rw_w), F32)
    wwa = wwa.at[:DECAY_LORA, :rw_w].set(rwkv_w2).at[DECAY_LORA:, rw_w:].set(rwkv_a2).astype(BF16)
    r, w, k, v, a, b, g, bonus = _rwkv_prep(
        rw_cols, row2(rwkv_mu), wwa, rwkv_g2.astype(BF16), row2(rwkv_w0), row2(rwkv_a0),
        row2(rwkv_k_k), row2(rwkv_k_a), row2(rwkv_r_k))
    ys = _wkv_scan(r, w, k, v, a, b)

    N = B * T
    out = _out_ffn(x.reshape(N, D), y_nsa.reshape(N, nsa_w), ys.reshape(N, rw_w), g.reshape(N, rw_w),
                   bonus.reshape(N, rw_w), row2(rwkv_lnx_w), row2(rwkv_lnx_b), w_out.astype(BF16),
                   row2(ln_ffn_g), w_ff1.astype(BF16), w_ff2.astype(BF16))
    return out.reshape(B, T, D)


def kernel(x, ln_mix_g, w_in, nsa_gate_b, q_norm_g, kc_norm_g, ks_norm_g, kw_norm_g, cmp_k_pos, cmp_k_w1, cmp_k_w2, cmp_v_pos, cmp_v_w1, cmp_v_w2, rwkv_mu, rwkv_w0, rwkv_w2, rwkv_a0, rwkv_a2, rwkv_g2, rwkv_k_k, rwkv_k_a, rwkv_r_k, rwkv_lnx_w, rwkv_lnx_b, w_out, ln_ffn_g, w_ff1, w_ff2):
    params = (ln_mix_g, w_in, nsa_gate_b, q_norm_g, kc_norm_g, ks_norm_g, kw_norm_g, cmp_k_pos, cmp_k_w1,
              cmp_k_w2, cmp_v_pos, cmp_v_w1, cmp_v_w2, rwkv_mu, rwkv_w0, rwkv_w2, rwkv_a0, rwkv_a2, rwkv_g2,
              rwkv_k_k, rwkv_k_a, rwkv_r_k, rwkv_lnx_w, rwkv_lnx_b, w_out, ln_ffn_g, w_ff1, w_ff2)
    for layer in range(ln_mix_g.shape[0]):
        x = _layer(x, *(p[layer] for p in params))
    return x
```
